```python
import math
import jax, jax.numpy as jnp
from jax import lax
import numpy as np

D_MODEL = 4096
BATCH = 4
SEQ = 2048
DEPTH = 1
DEC_BATCH = 128
DEC_SEQ = 8
PAST_LEN = 2048
PAGE_SIZE = 128

D_MIX = D_MODEL
ATT_W = D_MIX // 2
RWKV_W = D_MIX - ATT_W
DV = 128
N_ATT_HEADS = ATT_W // DV
DQK = DV // 2
ROT_DIM = DQK // 4
ROPE_THETA = 500000.0
RWKV_HEAD = 64
N_RWKV_HEADS = RWKV_W // RWKV_HEAD
DECAY_LORA = 64
A_LORA = 64
SHIFT_W = 3 * RWKV_W + DECAY_LORA + A_LORA
OFF_Q = 0
OFF_K = OFF_Q + ATT_W
OFF_V = OFF_K + ATT_W
OFF_SHIFT = OFF_V + ATT_W
OFF_GA = OFF_SHIFT + SHIFT_W
OFF_GR = OFF_GA + ATT_W
IN_W = OFF_GR + RWKV_W
Q_BLOCK = 128
NORM_EPS = 1e-6
SUBLN_EPS = 1e-5
LNX_EPS = 64e-5

kernel_name = 'hymba_diffattn_rwkv7_step'


def _rms_norm(x, g, eps):
    xf = x.astype(jnp.float32)
    y = xf * lax.rsqrt(jnp.mean(xf * xf, axis=-1, keepdims=True) + eps)
    return (y * g.astype(jnp.float32)).astype(x.dtype)


def _rope(x, pos):
    half = ROT_DIM // 2
    inv_freq = 1.0 / (ROPE_THETA ** (jnp.arange(0, ROT_DIM, 2, dtype=jnp.float32) / ROT_DIM))
    ang = pos.astype(jnp.float32)[:, None] * inv_freq[None, :]
    cos = jnp.cos(ang)[:, None, None, :]
    sin = jnp.sin(ang)[:, None, None, :]
    xr = x[..., :ROT_DIM].astype(jnp.float32)
    x1, x2 = xr[..., :half], xr[..., half:]
    rot = jnp.concatenate([x1 * cos - x2 * sin, x2 * cos + x1 * sin], axis=-1)
    return jnp.concatenate([rot.astype(x.dtype), x[..., ROT_DIM:]], axis=-1)


def _diff_attend(q, segments, lam):
    scale = DQK ** -0.5
    scores = []
    for k, v, mask in segments:
        s = jnp.einsum('bthmd,bshmd->bhmts', q, k, preferred_element_type=jnp.float32) * scale
        scores.append(jnp.where(mask, s, -jnp.inf))
    p = jax.nn.softmax(jnp.concatenate(scores, axis=-1), axis=-1)
    wts = p[:, :, 0] - lam * p[:, :, 1]
    out = None
    start = 0
    for k, v, _ in segments:
        n = k.shape[1]
        o = jnp.einsum('bhts,bshd->bthd', wts[..., start:start + n].astype(v.dtype), v,
                       preferred_element_type=jnp.float32)
        out = o if out is None else out + o
        start += n
    return out.astype(q.dtype)


def _prompt_attention(q, k, v, lam):
    b, s = q.shape[0], q.shape[1]
    nb = s // Q_BLOCK
    q_blocks = jnp.moveaxis(q.reshape(b, nb, Q_BLOCK, N_ATT_HEADS, 2, DQK), 1, 0)
    k_pos = jnp.arange(s)

    def one_block(args):
        q_blk, i = args
        q_pos = i * Q_BLOCK + jnp.arange(Q_BLOCK)
        mask = k_pos[None, :] <= q_pos[:, None]
        return _diff_attend(q_blk, ((k, v, mask),), lam)

    out = lax.map(one_block, (q_blocks, jnp.arange(nb)))
    return jnp.moveaxis(out, 0, 1).reshape(b, s, N_ATT_HEADS, DV)


def _wkv_scan(s0, r, w, k, v, a, bvec):
    def step(state, inp):
        r_t, w_t, k_t, v_t, a_t, b_t = inp
        sa = jnp.einsum('bhvk,bhk->bhv', state, a_t)
        state = (state * w_t[:, :, None, :] + sa[..., None] * b_t[:, :, None, :]
                 + v_t[..., None] * k_t[:, :, None, :])
        return state, jnp.einsum('bhvk,bhk->bhv', state, r_t)
    xs = tuple(jnp.moveaxis(z, 1, 0) for z in (r, w, k, v, a, bvec))
    s_final, ys = lax.scan(step, s0, xs)
    return s_final, jnp.moveaxis(ys, 0, 1)


def _rwkv7(mixed, s0, lw):
    b, t, _ = mixed.shape
    f32 = jnp.float32
    o1 = RWKV_W
    o2 = o1 + DECAY_LORA
    o3 = o2 + RWKV_W
    o4 = o3 + RWKV_W
    r, wd, kr, vr, ad = mixed[..., :o1], mixed[..., o1:o2], mixed[..., o2:o3], mixed[..., o3:o4], mixed[..., o4:]
    heads = lambda z: z.astype(f32).reshape(z.shape[:-1] + (N_RWKV_HEADS, RWKV_HEAD))
    w_log = -jax.nn.softplus(-(lw['w0'] + jnp.tanh(wd) @ lw['w2']).astype(f32)) - 0.5
    decay = jnp.exp(-jnp.exp(w_log))
    a = jax.nn.sigmoid((lw['a0'] + ad @ lw['a2']).astype(f32))
    kk = heads(kr * lw['k_k'])
    kk = kk / jnp.maximum(jnp.sqrt(jnp.sum(kk * kk, axis=-1, keepdims=True)), 1e-12)
    a_h = heads(a)
    k_h = heads(kr) * (1.0 + (a_h - 1.0) * heads(lw['k_a']))
    r_h = heads(r)
    v_h = heads(vr)
    s_final, o = _wkv_scan(s0.astype(f32), r_h, heads(decay), k_h, v_h, -kk, kk * a_h)
    mu = jnp.mean(o, axis=-1, keepdims=True)
    var = jnp.mean(jnp.square(o - mu), axis=-1, keepdims=True)
    on = ((o - mu) * lax.rsqrt(var + LNX_EPS)).reshape(b, t, RWKV_W)
    on = on * lw['lnx_g'].astype(f32) + lw['lnx_b'].astype(f32)
    bonus = jnp.sum(r_h * k_h * lw['r_k'].astype(f32), axis=-1, keepdims=True) * v_h
    y = on + bonus.reshape(b, t, RWKV_W)
    return y.astype(mixed.dtype), s_final


def _layer(x, pos, shift_prev, wkv_prev, past_kv, lw, layer_idx):
    b, t, _ = x.shape
    h = _rms_norm(x, lw['ln_g'], NORM_EPS)
    proj = h @ lw['w_in']
    q = proj[..., OFF_Q:OFF_K].reshape(b, t, N_ATT_HEADS, 2, DQK)
    k = proj[..., OFF_K:OFF_V].reshape(b, t, N_ATT_HEADS, 2, DQK)
    v = proj[..., OFF_V:OFF_SHIFT].reshape(b, t, N_ATT_HEADS, DV)
    q = _rope(_rms_norm(q, lw['q_norm_g'], NORM_EPS), pos)
    k = _rope(_rms_norm(k, lw['k_norm_g'], NORM_EPS), pos)
    lam_init = 0.8 - 0.6 * math.exp(-0.3 * layer_idx)
    f32 = jnp.float32
    lam = (jnp.exp(jnp.sum(lw['lambda_q1'].astype(f32) * lw['lambda_k1'].astype(f32)))
           - jnp.exp(jnp.sum(lw['lambda_q2'].astype(f32) * lw['lambda_k2'].astype(f32))) + lam_init)
    if past_kv is None:
        o = _prompt_attention(q, k, v, lam)
    else:
        k_past, v_past = past_kv
        mask_past = jnp.ones((t, k_past.shape[1]), dtype=bool)
        mask_new = jnp.tril(jnp.ones((t, t), dtype=bool))
        o = _diff_attend(q, ((k_past, v_past, mask_past), (k, v, mask_new)), lam)
    o = _rms_norm(o, lw['subln_g'], SUBLN_EPS) * (1.0 - lam_init)
    y_att = o.reshape(b, t, ATT_W) * jax.nn.silu(proj[..., OFF_GA:OFF_GR])
    cur = proj[..., OFF_SHIFT:OFF_GA]
    prev = jnp.concatenate([shift_prev[:, None].astype(cur.dtype), cur[:, :-1]], axis=1)
    mixed = cur + lw['time_mix'] * (prev - cur)
    y_rwkv, wkv_new = _rwkv7(mixed, wkv_prev, lw)
    y_rwkv = y_rwkv * jax.nn.silu(proj[..., OFF_GR:IN_W])
    out = jnp.concatenate([y_att, y_rwkv], axis=-1) @ lw['w_out']
    k_rows = k.reshape(b, t, N_ATT_HEADS, 2 * DQK)
    return x + out, k_rows, v, cur[:, -1], wkv_new


def setup_inputs(seed: int = 0) -> dict:
    key = jax.random.key(seed)
    ks = jax.random.split(key, 32)
    nrm = jax.random.normal
    f32 = jnp.float32
    n_pages = PAST_LEN // PAGE_SIZE
    n_used = DEC_BATCH * n_pages
    n_pool = n_used + max(1, n_used // 4)
    page_table = jax.random.permutation(ks[6], n_pool)[:n_used].reshape(DEC_BATCH, n_pages).astype(jnp.int32)
    return {
        'x_prompt': nrm(ks[0], (BATCH, SEQ, D_MODEL), f32),
        'x_sample': nrm(ks[1], (DEC_BATCH, DEC_SEQ, D_MODEL), f32),
        'cache_k': nrm(ks[2], (DEPTH, n_pool, PAGE_SIZE, N_ATT_HEADS, 2 * DQK), f32),
        'cache_v': nrm(ks[3], (DEPTH, n_pool, PAGE_SIZE, N_ATT_HEADS, DV), f32),
        'state_shift': nrm(ks[4], (DEPTH, DEC_BATCH, SHIFT_W), f32),
        'state_wkv': 0.3 * nrm(ks[5], (DEPTH, DEC_BATCH, N_RWKV_HEADS, RWKV_HEAD, RWKV_HEAD), f32),
        'page_table': page_table,
        'ln_g': 1.0 + 0.01 * nrm(ks[7], (DEPTH, D_MODEL), f32),
        'w_in': nrm(ks[8], (DEPTH, D_MODEL, IN_W), f32) * D_MODEL ** -0.5,
        'q_norm_g': 1.0 + 0.01 * nrm(ks[9], (DEPTH, DQK), f32),
        'k_norm_g': 1.0 + 0.01 * nrm(ks[10], (DEPTH, DQK), f32),
        'lambda_q1': 0.1 * nrm(ks[11], (DEPTH, DQK), f32),
        'lambda_k1': 0.1 * nrm(ks[12], (DEPTH, DQK), f32),
        'lambda_q2': 0.1 * nrm(ks[13], (DEPTH, DQK), f32),
        'lambda_k2': 0.1 * nrm(ks[14], (DEPTH, DQK), f32),
        'subln_g': 1.0 + 0.01 * nrm(ks[15], (DEPTH, DV), f32),
        'time_mix': jax.random.uniform(ks[16], (DEPTH, SHIFT_W), f32),
        'w0': jax.random.uniform(ks[17], (DEPTH, RWKV_W), f32, minval=-4.0, maxval=0.0),
        'w2': nrm(ks[18], (DEPTH, DECAY_LORA, RWKV_W), f32) * (0.5 * DECAY_LORA ** -0.5),
        'a0': 0.1 * nrm(ks[19], (DEPTH, RWKV_W), f32),
        'a2': nrm(ks[20], (DEPTH, A_LORA, RWKV_W), f32) * (0.5 * A_LORA ** -0.5),
        'k_k': 0.85 + 0.05 * nrm(ks[21], (DEPTH, RWKV_W), f32),
        'k_a': 1.0 + 0.05 * nrm(ks[22], (DEPTH, RWKV_W), f32),
        'r_k': 0.1 * nrm(ks[23], (DEPTH, N_RWKV_HEADS, RWKV_HEAD), f32),
        'lnx_g': 1.0 + 0.01 * nrm(ks[24], (DEPTH, RWKV_W), f32),
        'lnx_b': 0.01 * nrm(ks[25], (DEPTH, RWKV_W), f32),
        'w_out': nrm(ks[26], (DEPTH, D_MIX, D_MODEL), f32) * D_MIX ** -0.5,
    }


def reference(x_prompt, x_sample, cache_k, cache_v, state_shift, state_wkv, page_table,
              ln_g, w_in, q_norm_g, k_norm_g, lambda_q1, lambda_k1, lambda_q2, lambda_k2,
              subln_g, time_mix, w0, w2, a0, a2, k_k, k_a, r_k, lnx_g, lnx_b, w_out):
    b_p, s_p = x_prompt.shape[0], x_prompt.shape[1]
    b_s, s_s = x_sample.shape[0], x_sample.shape[1]
    past_len = page_table.shape[1] * PAGE_SIZE
    pos_p = jnp.arange(s_p)
    pos_s = past_len + jnp.arange(s_s)
    yp, ys = x_prompt, x_sample
    kp_l, vp_l, shp_l, wkvp_l = [], [], [], []
    ks_l, vs_l, shs_l, wkvs_l = [], [], [], []
    for l in range(DEPTH):
        lw = {
            'ln_g': ln_g[l], 'w_in': w_in[l], 'q_norm_g': q_norm_g[l], 'k_norm_g': k_norm_g[l],
            'lambda_q1': lambda_q1[l], 'lambda_k1': lambda_k1[l],
            'lambda_q2': lambda_q2[l], 'lambda_k2': lambda_k2[l],
            'subln_g': subln_g[l], 'time_mix': time_mix[l], 'w0': w0[l], 'w2': w2[l],
            'a0': a0[l], 'a2': a2[l], 'k_k': k_k[l], 'k_a': k_a[l], 'r_k': r_k[l],
            'lnx_g': lnx_g[l], 'lnx_b': lnx_b[l], 'w_out': w_out[l],
        }
        shift0 = jnp.zeros((b_p, SHIFT_W), dtype=x_prompt.dtype)
        wkv0 = jnp.zeros((b_p, N_RWKV_HEADS, RWKV_HEAD, RWKV_HEAD), dtype=jnp.float32)
        yp, kr, vr, sh, wkv = _layer(yp, pos_p, shift0, wkv0, None, lw, l)
        kp_l.append(kr)
        vp_l.append(vr)
        shp_l.append(sh)
        wkvp_l.append(wkv.astype(state_wkv.dtype))
        k_past = cache_k[l][page_table].reshape(b_s, past_len, N_ATT_HEADS, 2, DQK)
        v_past = cache_v[l][page_table].reshape(b_s, past_len, N_ATT_HEADS, DV)
        ys, kr, vr, sh, wkv = _layer(ys, pos_s, state_shift[l], state_wkv[l], (k_past, v_past), lw, l)
        ks_l.append(kr)
        vs_l.append(vr)
        shs_l.append(sh)
        wkvs_l.append(wkv.astype(state_wkv.dtype))
    k_prompt = jnp.stack(kp_l, 0)
    v_prompt = jnp.stack(vp_l, 0)
    shift_prompt = jnp.stack(shp_l, 0)
    wkv_prompt = jnp.stack(wkvp_l, 0)
    k_sample = jnp.stack(ks_l, 0)
    v_sample = jnp.stack(vs_l, 0)
    shift_sample = jnp.stack(shs_l, 0)
    wkv_sample = jnp.stack(wkvs_l, 0)
    return (yp, ys, k_prompt, v_prompt, shift_prompt, wkv_prompt, k_sample, v_sample, shift_sample, wkv_sample)
```

```python
import functools
import math

import jax
import jax.numpy as jnp
from jax import lax
from jax.experimental import pallas as pl
from jax.experimental.pallas import tpu as pltpu

F32 = jnp.float32
BF16 = jnp.bfloat16

D_MODEL = 4096
PAGE_SIZE = 128
ATT_W = 2048
RWKV_W = 2048
DV = 128
N_ATT_HEADS = ATT_W // DV
DQK = DV // 2
ROT_DIM = DQK // 4
ROPE_THETA = 500000.0
RWKV_HEAD = 64
N_RWKV_HEADS = RWKV_W // RWKV_HEAD
N_RWKV_PAIRS = N_RWKV_HEADS // 2
LORA = 64
SHIFT_W = 3 * RWKV_W + 2 * LORA
OFF_K = ATT_W
OFF_V = 2 * ATT_W
OFF_SHIFT = 3 * ATT_W
OFF_GA = OFF_SHIFT + SHIFT_W
OFF_GR = OFF_GA + ATT_W
IN_W = OFF_GR + RWKV_W
NORM_EPS = 1e-6
SUBLN_EPS = 1e-5
LNX_EPS = 64e-5

LANES = 128
VMEM_LIMIT_BYTES = 48 * 1024 * 1024
NEG_BIG = -1e30


def _cparams(sem):
    return pltpu.CompilerParams(dimension_semantics=sem, vmem_limit_bytes=VMEM_LIMIT_BYTES)


def _split_dot(x, w_bf16):
    hi = x.astype(BF16)
    lo = (x - hi.astype(F32)).astype(BF16)
    return (jnp.dot(hi, w_bf16, preferred_element_type=F32)
            + jnp.dot(lo, w_bf16, preferred_element_type=F32))


def _half_ones():
    r = lax.broadcasted_iota(jnp.int32, (LANES, LANES), 0) // RWKV_HEAD
    c = lax.broadcasted_iota(jnp.int32, (LANES, LANES), 1) // RWKV_HEAD
    return jnp.where(r == c, 1.0, 0.0).astype(BF16)


def _rmsnorm_kernel(x_ref, g_ref, o_ref):
    x = x_ref[...]
    ms = jnp.mean(x * x, axis=-1, keepdims=True)
    o_ref[...] = (x * lax.rsqrt(ms + NORM_EPS) * g_ref[...]).astype(BF16)


def _rmsnorm(x2, g, tm=256):
    m, d = x2.shape
    tm = min(tm, m)
    return pl.pallas_call(
        _rmsnorm_kernel,
        grid=(m // tm,),
        in_specs=[pl.BlockSpec((tm, d), lambda i: (i, 0)),
                  pl.BlockSpec((1, d), lambda i: (0, 0))],
        out_specs=pl.BlockSpec((tm, d), lambda i: (i, 0)),
        out_shape=jax.ShapeDtypeStruct((m, d), BF16),
        compiler_params=_cparams(("parallel",)),
        name="rmsnorm",
    )(x2, g.reshape(1, d))


def _mm_accumulate(h_ref, w_ref, acc_ref):
    k = pl.program_id(2)

    @pl.when(k == 0)
    def _():
        acc_ref[...] = jnp.zeros_like(acc_ref)

    acc_ref[...] += jnp.dot(h_ref[...], w_ref[...], preferred_element_type=F32)
    return k == pl.num_programs(2) - 1


def _mm_plain_kernel(h_ref, w_ref, o_ref, acc_ref):
    last = _mm_accumulate(h_ref, w_ref, acc_ref)

    @pl.when(last)
    def _():
        o_ref[...] = acc_ref[...].astype(o_ref.dtype)


def _mm_dual_kernel(h_ref, w_ref, o32_ref, o16_ref, acc_ref):
    last = _mm_accumulate(h_ref, w_ref, acc_ref)

    @pl.when(last)
    def _():
        o16_ref[...] = acc_ref[...].astype(BF16)
        for hh in range(acc_ref.shape[1] // LANES):
            o32_ref[:, hh, :] = acc_ref[:, hh * LANES:(hh + 1) * LANES]


def _mm_silu_kernel(h_ref, w_ref, o_ref, acc_ref):
    last = _mm_accumulate(h_ref, w_ref, acc_ref)

    @pl.when(last)
    def _():
        a = acc_ref[...]
        o_ref[...] = (a * jax.nn.sigmoid(a)).astype(o_ref.dtype)


def _mm_residual_kernel(h_ref, w_ref, x_ref, o_ref, acc_ref):
    last = _mm_accumulate(h_ref, w_ref, acc_ref)

    @pl.when(last)
    def _():
        o_ref[...] = x_ref[...] + acc_ref[...]


def _mm_qk_kernel(h_ref, w_ref, g_ref, cos_ref, s1_ref, s2_ref, *rest, scale, rows_per_chunk):
    outs, acc_ref = rest[:-1], rest[-1]
    last = _mm_accumulate(h_ref, w_ref, acc_ref)
    tm, tn = acc_ref.shape

    @pl.when(last)
    def _():
        ones = _half_ones()
        gain = g_ref[...]

        def chunk(r, carry):
            rows = pl.ds(pl.multiple_of(r * rows_per_chunk, rows_per_chunk), rows_per_chunk)
            c = cos_ref[rows, :]
            s1 = s1_ref[rows, :]
            s2 = s2_ref[rows, :]
            for hh in range(tn // LANES):
                cols = slice(hh * LANES, (hh + 1) * LANES)
                x = acc_ref[rows, cols]
                ss = _split_dot(x * x, ones)
                y = x * lax.rsqrt(ss * (1.0 / DQK) + NORM_EPS) * gain
                y = y * c + pltpu.roll(y, LANES - ROT_DIM // 2, 1) * s1 + pltpu.roll(y, ROT_DIM // 2, 1) * s2
                if scale != 1.0:
                    y = y * scale
                for o_ref in outs:
                    if len(o_ref.shape) == 3:
                        o_ref[rows, hh, :] = y.astype(o_ref.dtype)
                    else:
                        o_ref[rows, cols] = y.astype(o_ref.dtype)
            return carry

        lax.fori_loop(0, tm // rows_per_chunk, chunk, 0)


def _matmul(h, w, *, col_block_off, n_cols, kernel, out_dtypes, extra=(), extra_specs=(),
            tm=1024, tn=1024, tk=1024):
    m, kdim = h.shape
    tm = min(tm, m)
    tn = min(tn, n_cols)
    assert m % tm == 0 and n_cols % tn == 0 and kdim % tk == 0
    assert (col_block_off * LANES) % tn == 0
    joff = col_block_off * LANES // tn
    grid = (m // tm, n_cols // tn, kdim // tk)
    in_specs = [pl.BlockSpec((tm, tk), lambda i, j, k: (i, k)),
                pl.BlockSpec((tk, tn), lambda i, j, k: (k, j + joff))]
    in_specs += list(extra_specs)
    out_specs, out_shape = [], []
    for dt in out_dtypes:
        if dt == "f32_head_major":
            out_specs.append(pl.BlockSpec((tm, tn // LANES, LANES), lambda i, j, k: (i, j, 0)))
            out_shape.append(jax.ShapeDtypeStruct((m, n_cols // LANES, LANES), F32))
        else:
            out_specs.append(pl.BlockSpec((tm, tn), lambda i, j, k: (i, j)))
            out_shape.append(jax.ShapeDtypeStruct((m, n_cols), dt))
    res = pl.pallas_call(
        kernel,
        grid=grid,
        in_specs=in_specs,
        out_specs=out_specs,
        out_shape=out_shape,
        scratch_shapes=[pltpu.VMEM((tm, tn), F32)],
        compiler_params=_cparams(("parallel", "parallel", "arbitrary")),
        name="proj_matmul",
    )(h, w, *extra)
    return res


def _rope_tables(pos):
    half = ROT_DIM // 2
    inv_freq = 1.0 / (ROPE_THETA ** (jnp.arange(0, ROT_DIM, 2, dtype=F32) / ROT_DIM))
    ang = pos.astype(F32)[:, None] * inv_freq[None, :]
    cos = jnp.cos(ang)
    sin = jnp.sin(ang)
    t = pos.shape[0]
    one = jnp.ones((t, DQK - ROT_DIM), F32)
    zero = jnp.zeros((t, DQK - ROT_DIM), F32)
    zh = jnp.zeros((t, half), F32)
    c64 = jnp.concatenate([cos, cos, one], axis=1)
    s1_64 = jnp.concatenate([-sin, zh, zero], axis=1)
    s2_64 = jnp.concatenate([zh, sin, zero], axis=1)
    tile2 = lambda z: jnp.concatenate([z, z], axis=1)
    return tile2(c64), tile2(s1_64), tile2(s2_64)


def _stack_maps(q):
    lane = lax.broadcasted_iota(jnp.int32, q.shape, 1)
    z = jnp.zeros_like(q)
    return jnp.concatenate([jnp.where(lane < DQK, q, z), jnp.where(lane >= DQK, q, z)], axis=0)


def _online_softmax_step(s, v, m_prev, l_prev, acc_prev):
    m_new = jnp.maximum(m_prev, jnp.max(s, axis=-1, keepdims=True))
    alpha = jnp.exp(m_prev - m_new)
    p = jnp.exp(s - m_new)
    l_new = alpha * l_prev + jnp.sum(p, axis=-1, keepdims=True)
    acc_new = alpha * acc_prev + jnp.dot(p.astype(BF16), v, preferred_element_type=F32)
    return m_new, l_new, acc_new


def _attn_finish(acc, l, lam, subln_g, gate, out_scale):
    t = acc.shape[0] // 2
    o = acc[:t] / l[:t] - lam * (acc[t:] / l[t:])
    ms = jnp.mean(o * o, axis=-1, keepdims=True)
    o = o * lax.rsqrt(ms + SUBLN_EPS) * subln_g * out_scale
    return (o * gate.astype(F32)).astype(BF16)


def _prompt_attn_kernel(lam_ref, q_ref, k_ref, v_ref, g_ref, sg_ref, o_ref,
                        q12_ref, m_ref, l_ref, acc_ref, *, tq, out_scale):
    qi = pl.program_id(2)
    q12_ref[...] = _stack_maps(q_ref[...])
    m_ref[...] = jnp.full_like(m_ref, NEG_BIG)
    l_ref[...] = jnp.zeros_like(l_ref)
    acc_ref[...] = jnp.zeros_like(acc_ref)

    def tile(kj, masked):
        rows = pl.ds(pl.multiple_of(kj * tq, tq), tq)
        k = k_ref[rows, :]
        v = v_ref[rows, :]
        s = lax.dot_general(q12_ref[...], k, (((1,), (1,)), ((), ())), preferred_element_type=F32)
        if masked:
            r = lax.broadcasted_iota(jnp.int32, s.shape, 0) % tq
            c = lax.broadcasted_iota(jnp.int32, s.shape, 1)
            s = jnp.where(c <= r, s, NEG_BIG)
        m, l, acc = _online_softmax_step(s, v, m_ref[...], l_ref[...], acc_ref[...])
        m_ref[...] = m
        l_ref[...] = l
        acc_ref[...] = acc

    def body(kj, carry):
        tile(kj, False)
        return carry

    lax.fori_loop(0, qi, body, 0)
    tile(qi, True)
    o_ref[...] = _attn_finish(acc_ref[...], l_ref[...], lam_ref[0, 0], sg_ref[...], g_ref[...], out_scale)


def _prompt_attention(q, k16, v16, gates, lam, subln_g, *, batch, seq, out_scale, tq=256):
    nq = seq // tq
    kern = functools.partial(_prompt_attn_kernel, tq=tq, out_scale=out_scale)
    return pl.pallas_call(
        kern,
        grid=(batch, N_ATT_HEADS, nq),
        in_specs=[
            pl.BlockSpec(memory_space=pltpu.SMEM),
            pl.BlockSpec((tq, DV), lambda b, h, i: (b * nq + i, h)),
            pl.BlockSpec((seq, DV), lambda b, h, i: (b, h)),
            pl.BlockSpec((seq, DV), lambda b, h, i: (b, h)),
            pl.BlockSpec((tq, DV), lambda b, h, i: (b * nq + i, h)),
            pl.BlockSpec((1, DV), lambda b, h, i: (0, 0)),
        ],
        out_specs=pl.BlockSpec((tq, DV), lambda b, h, i: (b * nq + i, h)),
        out_shape=jax.ShapeDtypeStruct((batch * seq, ATT_W), BF16),
        scratch_shapes=[pltpu.VMEM((2 * tq, DV), BF16),
                        pltpu.VMEM((2 * tq, 1), F32),
                        pltpu.VMEM((2 * tq, 1), F32),
                        pltpu.VMEM((2 * tq, DV), F32)],
        compiler_params=_cparams(("parallel", "parallel", "arbitrary")),
        name="prompt_attention",
    )(lam, q, k16, v16, gates, subln_g.reshape(1, DV))


def _sample_attn_kernel(pt_ref, lam_ref, q_ref, kc_ref, vc_ref, kn_ref, vn_ref, g_ref, sg_ref, o_ref,
                        q12_ref, m_ref, l_ref, acc_ref, *, t_new, out_scale):
    j = pl.program_id(1)
    n_pages = pl.num_programs(1)
    rows = 2 * t_new

    @pl.when(j == 0)
    def _():
        for h in range(N_ATT_HEADS):
            q12_ref[h] = _stack_maps(q_ref[0, :, h * DV:(h + 1) * DV])
        m_ref[...] = jnp.full_like(m_ref, NEG_BIG)
        l_ref[...] = jnp.zeros_like(l_ref)
        acc_ref[...] = jnp.zeros_like(acc_ref)

    def update(h, k, v, mask):
        s = lax.dot_general(q12_ref[h], k, (((1,), (1,)), ((), ())), preferred_element_type=F32)
        if mask is not None:
            s = jnp.where(mask, s, NEG_BIG)
        m, l, acc = _online_softmax_step(s, v, m_ref[h], l_ref[h], acc_ref[h])
        m_ref[h] = m
        l_ref[h] = l
        acc_ref[h] = acc

    for h in range(N_ATT_HEADS):
        update(h, kc_ref[0, 0, :, h, :].astype(BF16), vc_ref[0, 0, :, h, :].astype(BF16), None)

    @pl.when(j == n_pages - 1)
    def _():
        r = lax.broadcasted_iota(jnp.int32, (rows, t_new), 0) % t_new
        c = lax.broadcasted_iota(jnp.int32, (rows, t_new), 1)
        causal = c <= r
        lam = lam_ref[0, 0]
        for h in range(N_ATT_HEADS):
            cols = slice(h * DV, (h + 1) * DV)
            update(h, kn_ref[0, :, cols], vn_ref[0, :, cols], causal)
            o_ref[0, :, cols] = _attn_finish(acc_ref[h], l_ref[h], lam, sg_ref[...],
                                             g_ref[0, :, cols], out_scale)


def _sample_attention(q, k16, v16, gates, cache_k, cache_v, page_table, lam, subln_g, *, layer, out_scale):
    b, t_new, _ = q.shape
    n_pages = page_table.shape[1]
    kern = functools.partial(_sample_attn_kernel, t_new=t_new, out_scale=out_scale)
    tok_spec = pl.BlockSpec((1, t_new, ATT_W), lambda i, j, pt: (i, 0, 0))
    page_spec = pl.BlockSpec((1, 1, PAGE_SIZE, N_ATT_HEADS, DV), lambda i, j, pt: (layer, pt[i, j], 0, 0, 0))
    grid_spec = pltpu.PrefetchScalarGridSpec(
        num_scalar_prefetch=1,
        grid=(b, n_pages),
        in_specs=[
            pl.BlockSpec(memory_space=pltpu.SMEM),
            tok_spec, page_spec, page_spec, tok_spec, tok_spec, tok_spec,
            pl.BlockSpec((1, DV), lambda i, j, pt: (0, 0)),
        ],
        out_specs=tok_spec,
        scratch_shapes=[pltpu.VMEM((N_ATT_HEADS, 2 * t_new, DV), BF16),
                        pltpu.VMEM((N_ATT_HEADS, 2 * t_new, 1), F32),
                        pltpu.VMEM((N_ATT_HEADS, 2 * t_new, 1), F32),
                        pltpu.VMEM((N_ATT_HEADS, 2 * t_new, DV), F32)],
    )
    return pl.pallas_call(
        kern,
        grid_spec=grid_spec,
        out_shape=jax.ShapeDtypeStruct((b, t_new, ATT_W), BF16),
        compiler_params=_cparams(("parallel", "arbitrary")),
        name="sample_attention",
    )(page_table, lam, q, cache_k, cache_v, k16, v16, gates, subln_g.reshape(1, DV))


def _stack_heads(x):
    lane = lax.broadcasted_iota(jnp.int32, x.shape, 1)
    z = jnp.zeros_like(x)
    return jnp.concatenate([jnp.where(lane < RWKV_HEAD, x, z), jnp.where(lane >= RWKV_HEAD, x, z)], axis=0)


def _unit_lower_inverse(a, n_steps):
    n = a.shape[0]
    eye = (lax.broadcasted_iota(jnp.int32, (n, n), 0) == lax.broadcasted_iota(jnp.int32, (n, n), 1))
    p = jnp.where(eye, 1.0, 0.0) + a
    pw = a
    for _ in range(n_steps - 1):
        pw16 = pw.astype(BF16)
        pw = jnp.dot(pw16, pw16, preferred_element_type=F32)
        p = p + jnp.dot(p.astype(BF16), pw.astype(BF16), preferred_element_type=F32)
    return p


def _rwkv_kernel(r_ref, k_ref, v_ref, lo_ref, pr_ref, pk_ref, pv_ref, plo_ref,
                 tmr_ref, tmk_ref, tmv_ref, tmlo_ref, wl_ref, w0_ref, a0_ref, kk_ref, ka_ref, rk_ref,
                 lg_ref, lb_ref, gate_ref, s0_ref, y_ref, sout_ref,
                 h_ref, cr_ref, ck_ref, cv_ref, clo_ref, *, chunk, use_s0):
    c = pl.program_id(2)
    n_chunks = pl.num_programs(2)
    C = chunk
    ones = _half_ones()

    @pl.when(c == 0)
    def _():
        cr_ref[...] = pr_ref[0]
        ck_ref[...] = pk_ref[0]
        cv_ref[...] = pv_ref[0]
        clo_ref[...] = plo_ref[0]
        if use_s0:
            z = jnp.zeros((RWKV_HEAD, RWKV_HEAD), F32)
            st = jnp.concatenate([jnp.concatenate([s0_ref[0, 0], z], axis=1),
                                  jnp.concatenate([z, s0_ref[0, 1]], axis=1)], axis=0)
            h_ref[...] = st.T
        else:
            h_ref[...] = jnp.zeros_like(h_ref)

    row = lax.broadcasted_iota(jnp.int32, (C, LANES), 0)

    def token_shift(cur_ref3, carry_ref, tm_ref):
        cur = cur_ref3[0]
        prev = jnp.where(row == 0, carry_ref[...], pltpu.roll(cur, 1, 0))
        carry_ref[...] = cur[C - 1:C, :]
        return cur + tm_ref[...] * (prev - cur)

    r = token_shift(r_ref, cr_ref, tmr_ref)
    kr = token_shift(k_ref, ck_ref, tmk_ref)
    v = token_shift(v_ref, cv_ref, tmv_ref)
    lo = token_shift(lo_ref, clo_ref, tmlo_ref)

    lane = lax.broadcasted_iota(jnp.int32, (C, LANES), 1)
    lo_act = jnp.where(lane < LORA, jnp.tanh(lo), lo)
    pre = jnp.dot(lo_act.astype(BF16), wl_ref[0], preferred_element_type=F32)
    logw = -math.exp(-0.5) * jax.nn.sigmoid(w0_ref[...] + pre[:, :LANES])
    a_sig = jax.nn.sigmoid(a0_ref[...] + pre[:, LANES:])

    kk = kr * kk_ref[...]
    kk = kk / jnp.maximum(jnp.sqrt(_split_dot(kk * kk, ones)), 1e-12)
    k_h = kr * (1.0 + (a_sig - 1.0) * ka_ref[...])
    bonus = _split_dot(r * k_h * rk_ref[...], ones) * v

    tril = (lax.broadcasted_iota(jnp.int32, (C, C), 1) <= lax.broadcasted_iota(jnp.int32, (C, C), 0))
    tril = jnp.where(tril, 1.0, 0.0).astype(BF16)
    l1 = logw.astype(BF16)
    rem = logw - l1.astype(F32)
    l2 = rem.astype(BF16)
    l3 = (rem - l2.astype(F32)).astype(BF16)
    cum = (jnp.dot(tril, l1, preferred_element_type=F32) + jnp.dot(tril, l2, preferred_element_type=F32)
           + jnp.dot(tril, l3, preferred_element_type=F32))
    g_incl = jnp.exp(cum)
    g_excl = jnp.exp(cum - logw)
    g_inv = jnp.exp(-cum)
    g_last = g_incl[C - 1:C, :]

    a_t = _stack_heads(-kk * g_excl).astype(BF16)
    r_t = _stack_heads(r * g_incl).astype(BF16)
    b_t = _stack_heads(kk * a_sig * g_inv)
    k_t = _stack_heads(k_h * g_inv)
    v_st = _stack_heads(v).astype(BF16)

    n2 = 2 * C
    ar = jnp.concatenate([a_t, r_t], axis=0)
    bk = jnp.concatenate([b_t, k_t], axis=0)
    gmat = lax.dot_general(ar, bk.astype(BF16), (((1,), (1,)), ((), ())), preferred_element_type=F32)
    ri = lax.broadcasted_iota(jnp.int32, (n2, n2), 0)
    ci = lax.broadcasted_iota(jnp.int32, (n2, n2), 1)
    strict = ri > ci
    incl = ri >= ci
    a_ab = jnp.where(strict, gmat[:n2, :n2], 0.0)
    a_ak = jnp.where(strict, gmat[:n2, n2:], 0.0)
    a_rb = jnp.where(incl, gmat[n2:, :n2], 0.0)
    a_rk = jnp.where(incl, gmat[n2:, n2:], 0.0)

    t_inv = _unit_lower_inverse(a_ab, int(math.log2(C)))

    h0 = h_ref[...]
    h16 = h0.astype(BF16)
    x = (jnp.dot(a_t, h16, preferred_element_type=F32)
         + jnp.dot(a_ak.astype(BF16), v_st, preferred_element_type=F32))
    u = jnp.dot(t_inv.astype(BF16), x.astype(BF16), preferred_element_type=F32)
    u16 = u.astype(BF16)
    o_st = (jnp.dot(r_t, h16, preferred_element_type=F32)
            + jnp.dot(a_rb.astype(BF16), u16, preferred_element_type=F32)
            + jnp.dot(a_rk.astype(BF16), v_st, preferred_element_type=F32))
    o = o_st[:C] + o_st[C:]

    bk_t = (bk * g_last).T.astype(BF16)
    uv = jnp.concatenate([u16, v_st], axis=0)
    eye = (lax.broadcasted_iota(jnp.int32, (LANES, LANES), 0)
           == lax.broadcasted_iota(jnp.int32, (LANES, LANES), 1))
    g_col = jnp.sum(jnp.where(eye, jnp.broadcast_to(g_last, (LANES, LANES)), 0.0), axis=1, keepdims=True)
    h_new = g_col * h0 + jnp.dot(bk_t, uv, preferred_element_type=F32)
    h_ref[...] = h_new

    mu = _split_dot(o, ones) * (1.0 / RWKV_HEAD)
    d = o - mu
    var = _split_dot(d * d, ones) * (1.0 / RWKV_HEAD)
    on = d * lax.rsqrt(var + LNX_EPS) * lg_ref[...] + lb_ref[...]
    y_ref[0] = ((on + bonus) * gate_ref[0].astype(F32)).astype(BF16)

    @pl.when(c == n_chunks - 1)
    def _():
        st = h_new.T
        sout_ref[0, 0] = st[:RWKV_HEAD, :RWKV_HEAD]
        sout_ref[0, 1] = st[RWKV_HEAD:, RWKV_HEAD:]


def _rwkv(rkv, lora, gates, prev_rkv, prev_lora, s0, p, *, chunk):
    b, t, _ = rkv.shape
    n_chunks = t // chunk
    np_ = N_RWKV_PAIRS
    use_s0 = s0 is not None
    if s0 is None:
        s0 = jnp.zeros((1, 2, RWKV_HEAD, RWKV_HEAD), F32)
        s0_spec = pl.BlockSpec((1, 2, RWKV_HEAD, RWKV_HEAD), lambda i, j, c: (0, 0, 0, 0))
    else:
        s0_spec = pl.BlockSpec((1, 2, RWKV_HEAD, RWKV_HEAD), lambda i, j, c: (i, j, 0, 0))

    def tok(off):
        return pl.BlockSpec((1, chunk, LANES), lambda i, j, c: (i, c, off + j))

    def prev(off):
        return pl.BlockSpec((1, 1, LANES), lambda i, j, c: (i, 0, off + j))

    def par(off=0):
        return pl.BlockSpec((1, LANES), lambda i, j, c: (0, off + j))

    fixed = lambda shape: pl.BlockSpec(shape, lambda i, j, c: (0,) * len(shape))
    kern = functools.partial(_rwkv_kernel, chunk=chunk, use_s0=use_s0)
    y, s_out = pl.pallas_call(
        kern,
        grid=(b, np_, n_chunks),
        in_specs=[
            tok(0), tok(np_), tok(2 * np_),
            pl.BlockSpec((1, chunk, LANES), lambda i, j, c: (i, c, 0)),
            prev(0), prev(np_), prev(2 * np_),
            pl.BlockSpec((1, 1, LANES), lambda i, j, c: (i, 0, 0)),
            par(0), par(np_), par(2 * np_), fixed((1, LANES)),
            pl.BlockSpec((1, LANES, 2 * LANES), lambda i, j, c: (j, 0, 0)),
            par(), par(), par(), par(), par(), par(), par(),
            pl.BlockSpec((1, chunk, LANES), lambda i, j, c: (i, c, np_ + j)),
            s0_spec,
        ],
        out_specs=[pl.BlockSpec((1, chunk, LANES), lambda i, j, c: (i, c, j)),
                   pl.BlockSpec((1, 2, RWKV_HEAD, RWKV_HEAD), lambda i, j, c: (i, j, 0, 0))],
        out_shape=[jax.ShapeDtypeStruct((b, t, RWKV_W), BF16),
                   jax.ShapeDtypeStruct((b, N_RWKV_HEADS, RWKV_HEAD, RWKV_HEAD), F32)],
        scratch_shapes=[pltpu.VMEM((LANES, LANES), F32),
                        pltpu.VMEM((1, LANES), F32), pltpu.VMEM((1, LANES), F32),
                        pltpu.VMEM((1, LANES), F32), pltpu.VMEM((1, LANES), F32)],
        compiler_params=_cparams(("parallel", "parallel", "arbitrary")),
        name="rwkv7_chunked",
    )(rkv, rkv, rkv, lora, prev_rkv, prev_rkv, prev_rkv, prev_lora,
      p["tm_rkv"], p["tm_rkv"], p["tm_rkv"], p["tm_lora"], p["w_lora"],
      p["w0"], p["a0"], p["k_k"], p["k_a"], p["r_k"], p["lnx_g"], p["lnx_b"], gates, s0)
    return y, s_out


def _prep_layer_params(l, ln_g, w_in, q_norm_g, k_norm_g, lambda_q1, lambda_k1, lambda_q2, lambda_k2,
                       subln_g, time_mix, w0, w2, a0, a2, k_k, k_a, r_k, lnx_g, lnx_b, w_out):
    w = w_in[l]
    sh = OFF_SHIFT
    o1, o2, o3, o4 = RWKV_W, RWKV_W + LORA, 2 * RWKV_W + LORA, 3 * RWKV_W + LORA
    w_main = jnp.concatenate([w[:, :sh], w[:, sh:sh + o1], w[:, sh + o2:sh + o3], w[:, sh + o3:sh + o4],
                              w[:, OFF_GA:]], axis=1).astype(BF16)
    w_lora_in = jnp.concatenate([w[:, sh + o1:sh + o2], w[:, sh + o4:sh + SHIFT_W]], axis=1).astype(BF16)
    tmix = time_mix[l]
    tm_rkv = jnp.concatenate([tmix[:o1], tmix[o2:o3], tmix[o3:o4]]).reshape(1, 3 * RWKV_W)
    tm_lora = jnp.concatenate([tmix[o1:o2], tmix[o4:]]).reshape(1, 2 * LORA)
    w2p = w2[l].reshape(LORA, N_RWKV_PAIRS, LANES).transpose(1, 0, 2)
    a2p = a2[l].reshape(LORA, N_RWKV_PAIRS, LANES).transpose(1, 0, 2)
    z = jnp.zeros_like(w2p)
    w_lora = jnp.concatenate([jnp.concatenate([w2p, z], axis=2), jnp.concatenate([z, a2p], axis=2)],
                             axis=1).astype(BF16)
    lam_init = 0.8 - 0.6 * math.exp(-0.3 * l)
    lam = (jnp.exp(jnp.sum(lambda_q1[l] * lambda_k1[l])) - jnp.exp(jnp.sum(lambda_q2[l] * lambda_k2[l]))
           + lam_init).reshape(1, 1).astype(F32)
    row = lambda z_: z_.reshape(1, -1)
    return dict(
        layer=l, ln_g=ln_g[l], w_main=w_main, w_lora_in=w_lora_in, w_out=w_out[l].astype(BF16),
        q_gain=jnp.tile(q_norm_g[l], 2).reshape(1, DV), k_gain=jnp.tile(k_norm_g[l], 2).reshape(1, DV),
        lam=lam, out_scale=1.0 - lam_init, subln_g=subln_g[l],
        tm_rkv=tm_rkv, tm_lora=tm_lora, w_lora=w_lora,
        w0=row(w0[l]), a0=row(a0[l]), k_k=row(k_k[l]), k_a=row(k_a[l]), r_k=row(r_k[l]),
        lnx_g=row(lnx_g[l]), lnx_b=row(lnx_b[l]),
    )


def _layer(x, pos, shift_prev, wkv_prev, past, p, *, chunk):
    b, t, d = x.shape
    m = b * t
    x2 = x.reshape(m, d)
    h = _rmsnorm(x2, p["ln_g"])

    tm = min(1024, m)
    cos, s1, s2 = _rope_tables(pos)
    if t < tm:
        reps = tm // t
        cos, s1, s2 = (jnp.tile(z, (reps, 1)) for z in (cos, s1, s2))
    n_tab = cos.shape[0] // tm
    tab_spec = pl.BlockSpec((tm, LANES), lambda i, j, k: (i % n_tab, 0))
    gain_spec = pl.BlockSpec((1, LANES), lambda i, j, k: (0, 0))
    blk = lambda cols: cols // LANES

    def qk(col_off, gain, scale, dts):
        kern = functools.partial(_mm_qk_kernel, scale=scale, rows_per_chunk=min(256, tm))
        return _matmul(h, p["w_main"], col_block_off=blk(col_off), n_cols=ATT_W, kernel=kern,
                       out_dtypes=dts, extra=(gain, cos, s1, s2),
                       extra_specs=(gain_spec, tab_spec, tab_spec, tab_spec), tm=tm)

    (q16,) = qk(0, p["q_gain"], DQK ** -0.5, (BF16,))
    k32, k16 = qk(OFF_K, p["k_gain"], 1.0, ("f32_head_major", BF16))
    v32, v16 = _matmul(h, p["w_main"], col_block_off=blk(OFF_V), n_cols=ATT_W,
                       kernel=_mm_dual_kernel, out_dtypes=("f32_head_major", BF16), tm=tm)
    (rkv,) = _matmul(h, p["w_main"], col_block_off=blk(3 * ATT_W), n_cols=3 * RWKV_W,
                     kernel=_mm_plain_kernel, out_dtypes=(F32,), tm=tm)
    (gates,) = _matmul(h, p["w_main"], col_block_off=blk(3 * ATT_W + 3 * RWKV_W), n_cols=ATT_W + RWKV_W,
                       kernel=_mm_silu_kernel, out_dtypes=(BF16,), tm=tm)
    (lora,) = _matmul(h, p["w_lora_in"], col_block_off=0, n_cols=2 * LORA,
                      kernel=_mm_plain_kernel, out_dtypes=(F32,), tm=tm)

    if past is None:
        y_att = _prompt_attention(q16, k16, v16, gates, p["lam"], p["subln_g"],
                                  batch=b, seq=t, out_scale=p["out_scale"])
    else:
        cache_k, cache_v, page_table = past
        y_att = _sample_attention(q16.reshape(b, t, ATT_W), k16.reshape(b, t, ATT_W),
                                  v16.reshape(b, t, ATT_W), gates.reshape(b, t, -1),
                                  cache_k, cache_v, page_table, p["lam"], p["subln_g"],
                                  layer=p["layer"], out_scale=p["out_scale"]).reshape(m, ATT_W)

    o1, o2, o3, o4 = RWKV_W, RWKV_W + LORA, 2 * RWKV_W + LORA, 3 * RWKV_W + LORA
    prev_rkv = jnp.concatenate([shift_prev[:, :o1], shift_prev[:, o2:o3], shift_prev[:, o3:o4]],
                               axis=1).reshape(b, 1, 3 * RWKV_W)
    prev_lora = jnp.concatenate([shift_prev[:, o1:o2], shift_prev[:, o4:]], axis=1).reshape(b, 1, 2 * LORA)
    rkv3 = rkv.reshape(b, t, 3 * RWKV_W)
    lora3 = lora.reshape(b, t, 2 * LORA)
    y_rwkv, wkv_new = _rwkv(rkv3, lora3, gates.reshape(b, t, -1), prev_rkv, prev_lora, wkv_prev, p,
                            chunk=chunk)
    last_rkv = rkv3[:, -1]
    last_lora = lora3[:, -1]
    shift_new = jnp.concatenate([last_rkv[:, :RWKV_W], last_lora[:, :LORA], last_rkv[:, RWKV_W:],
                                 last_lora[:, LORA:]], axis=1)

    y = jnp.concatenate([y_att, y_rwkv.reshape(m, RWKV_W)], axis=1)
    x_spec = pl.BlockSpec((tm, 1024), lambda i, j, k: (i, j))
    (out,) = _matmul(y, p["w_out"], col_block_off=0, n_cols=d, kernel=_mm_residual_kernel,
                     out_dtypes=(F32,), extra=(x2,), extra_specs=(x_spec,), tm=tm)
    return (out.reshape(b, t, d), k32.reshape(b, t, N_ATT_HEADS, DV), v32.reshape(b, t, N_ATT_HEADS, DV),
            shift_new, wkv_new)


def kernel(x_prompt, x_sample, cache_k, cache_v, state_shift, state_wkv, page_table, ln_g, w_in, q_norm_g,
           k_norm_g, lambda_q1, lambda_k1, lambda_q2, lambda_k2, subln_g, time_mix, w0, w2, a0, a2, k_k,
           k_a, r_k, lnx_g, lnx_b, w_out):
    depth = w_in.shape[0]
    b_p, s_p = x_prompt.shape[0], x_prompt.shape[1]
    s_s = x_sample.shape[1]
    past_len = page_table.shape[1] * PAGE_SIZE
    pos_p = jnp.arange(s_p)
    pos_s = past_len + jnp.arange(s_s)
    yp, ys = x_prompt, x_sample
    outs = [[] for _ in range(8)]
    for l in range(depth):
        p = _prep_layer_params(l, ln_g, w_in, q_norm_g, k_norm_g, lambda_q1, lambda_k1, lambda_q2,
                               lambda_k2, subln_g, time_mix, w0, w2, a0, a2, k_k, k_a, r_k, lnx_g, lnx_b,
                               w_out)
        shift0 = jnp.zeros((b_p, SHIFT_W), F32)
        yp, kp, vp, shp, wkvp = _layer(yp, pos_p, shift0, None, None, p, chunk=64)
        past = (cache_k, cache_v, page_table)
        ys, ks, vs, shs, wkvs = _layer(ys, pos_s, state_shift[l], state_wkv[l], past, p, chunk=s_s)
        for lst, val in zip(outs, (kp, vp, shp, wkvp, ks, vs, shs, wkvs)):
            lst.append(val)
    stacked = [jnp.stack(lst, 0) for lst in outs]
    return (yp, ys, *stacked)
```

```python
import functools
import math

import jax
import jax.numpy as jnp
from jax import lax
from jax.experimental import pallas as pl
from jax.experimental.pallas import tpu as pltpu

F32 = jnp.float32
BF16 = jnp.bfloat16

D_MODEL = 4096
PAGE_SIZE = 128
ATT_W = 2048
RWKV_W = 2048
DV = 128
N_ATT_HEADS = ATT_W // DV
DQK = DV // 2
ROT_DIM = DQK // 4
ROPE_THETA = 500000.0
RWKV_HEAD = 64
N_RWKV_HEADS = RWKV_W // RWKV_HEAD
N_RWKV_PAIRS = N_RWKV_HEADS // 2
LORA = 64
SHIFT_W = 3 * RWKV_W + 2 * LORA
OFF_K = ATT_W
OFF_V = 2 * ATT_W
OFF_SHIFT = 3 * ATT_W
OFF_GA = OFF_SHIFT + SHIFT_W
OFF_GR = OFF_GA + ATT_W
IN_W = OFF_GR + RWKV_W
NORM_EPS = 1e-6
SUBLN_EPS = 1e-5
LNX_EPS = 64e-5

LANES = 128
VMEM_LIMIT_BYTES = 48 * 1024 * 1024
NEG_BIG = -1e30


def _cparams(sem):
    return pltpu.CompilerParams(dimension_semantics=sem, vmem_limit_bytes=VMEM_LIMIT_BYTES)


def _split_dot(x, w_bf16):
    hi = x.astype(BF16)
    lo = (x - hi.astype(F32)).astype(BF16)
    return (jnp.dot(hi, w_bf16, preferred_element_type=F32)
            + jnp.dot(lo, w_bf16, preferred_element_type=F32))


def _half_ones():
    r = lax.broadcasted_iota(jnp.int32, (LANES, LANES), 0) // RWKV_HEAD
    c = lax.broadcasted_iota(jnp.int32, (LANES, LANES), 1) // RWKV_HEAD
    return jnp.where(r == c, 1.0, 0.0).astype(BF16)


def _rmsnorm_kernel(x_ref, g_ref, o_ref):
    x = x_ref[...]
    ms = jnp.mean(x * x, axis=-1, keepdims=True)
    o_ref[...] = (x * lax.rsqrt(ms + NORM_EPS) * g_ref[...]).astype(BF16)


def _rmsnorm(x2, g, tm=256):
    m, d = x2.shape
    tm = min(tm, m)
    return pl.pallas_call(
        _rmsnorm_kernel,
        grid=(m // tm,),
        in_specs=[pl.BlockSpec((tm, d), lambda i: (i, 0)),
                  pl.BlockSpec((1, d), lambda i: (0, 0))],
        out_specs=pl.BlockSpec((tm, d), lambda i: (i, 0)),
        out_shape=jax.ShapeDtypeStruct((m, d), BF16),
        compiler_params=_cparams(("parallel",)),
        name="rmsnorm",
    )(x2, g.reshape(1, d))


def _mm_accumulate(h_ref, w_ref, acc_ref):
    k = pl.program_id(2)

    @pl.when(k == 0)
    def _():
        acc_ref[...] = jnp.zeros_like(acc_ref)

    acc_ref[...] += jnp.dot(h_ref[...], w_ref[...], preferred_element_type=F32)
    return k == pl.num_programs(2) - 1


def _mm_plain_kernel(h_ref, w_ref, o_ref, acc_ref):
    last = _mm_accumulate(h_ref, w_ref, acc_ref)

    @pl.when(last)
    def _():
        o_ref[...] = acc_ref[...].astype(o_ref.dtype)


def _mm_dual_kernel(h_ref, w_ref, o32_ref, o16_ref, acc_ref):
    last = _mm_accumulate(h_ref, w_ref, acc_ref)

    @pl.when(last)
    def _():
        o16_ref[...] = acc_ref[...].astype(BF16)
        for hh in range(acc_ref.shape[1] // LANES):
            o32_ref[:, hh, :] = acc_ref[:, hh * LANES:(hh + 1) * LANES]


def _mm_silu_kernel(h_ref, w_ref, o_ref, acc_ref):
    last = _mm_accumulate(h_ref, w_ref, acc_ref)

    @pl.when(last)
    def _():
        a = acc_ref[...]
        o_ref[...] = (a * jax.nn.sigmoid(a)).astype(o_ref.dtype)


def _mm_residual_kernel(h_ref, w_ref, x_ref, o_ref, acc_ref):
    last = _mm_accumulate(h_ref, w_ref, acc_ref)

    @pl.when(last)
    def _():
        o_ref[...] = x_ref[...] + acc_ref[...]


def _mm_qk_kernel(h_ref, w_ref, g_ref, cos_ref, s1_ref, s2_ref, *rest, scale, rows_per_chunk):
    outs, acc_ref = rest[:-1], rest[-1]
    last = _mm_accumulate(h_ref, w_ref, acc_ref)
    tm, tn = acc_ref.shape

    @pl.when(last)
    def _():
        ones = _half_ones()
        gain = g_ref[...]

        def chunk(r, carry):
            rows = pl.ds(pl.multiple_of(r * rows_per_chunk, rows_per_chunk), rows_per_chunk)
            c = cos_ref[rows, :]
            s1 = s1_ref[rows, :]
            s2 = s2_ref[rows, :]
            for hh in range(tn // LANES):
                cols = slice(hh * LANES, (hh + 1) * LANES)
                x = acc_ref[rows, cols]
                ss = _split_dot(x * x, ones)
                y = x * lax.rsqrt(ss * (1.0 / DQK) + NORM_EPS) * gain
                y = y * c + pltpu.roll(y, LANES - ROT_DIM // 2, 1) * s1 + pltpu.roll(y, ROT_DIM // 2, 1) * s2
                if scale != 1.0:
                    y = y * scale
                for o_ref in outs:
                    if len(o_ref.shape) == 3:
                        o_ref[rows, hh, :] = y.astype(o_ref.dtype)
                    else:
                        o_ref[rows, cols] = y.astype(o_ref.dtype)
            return carry

        lax.fori_loop(0, tm // rows_per_chunk, chunk, 0)


def _matmul(h, w, *, col_block_off, n_cols, kernel, out_dtypes, extra=(), extra_specs=(),
            tm=1024, tn=1024, tk=1024):
    m, kdim = h.shape
    tm = min(tm, m)
    tn = min(tn, n_cols)
    assert m % tm == 0 and n_cols % tn == 0 and kdim % tk == 0
    assert (col_block_off * LANES) % tn == 0
    joff = col_block_off * LANES // tn
    grid = (m // tm, n_cols // tn, kdim // tk)
    in_specs = [pl.BlockSpec((tm, tk), lambda i, j, k: (i, k)),
                pl.BlockSpec((tk, tn), lambda i, j, k: (k, j + joff))]
    in_specs += list(extra_specs)
    out_specs, out_shape = [], []
    for dt in out_dtypes:
        if dt == "f32_head_major":
            out_specs.append(pl.BlockSpec((tm, tn // LANES, LANES), lambda i, j, k: (i, j, 0)))
            out_shape.append(jax.ShapeDtypeStruct((m, n_cols // LANES, LANES), F32))
        else:
            out_specs.append(pl.BlockSpec((tm, tn), lambda i, j, k: (i, j)))
            out_shape.append(jax.ShapeDtypeStruct((m, n_cols), dt))
    res = pl.pallas_call(
        kernel,
        grid=grid,
        in_specs=in_specs,
        out_specs=out_specs,
        out_shape=out_shape,
        scratch_shapes=[pltpu.VMEM((tm, tn), F32)],
        compiler_params=_cparams(("parallel", "parallel", "arbitrary")),
        name="proj_matmul",
    )(h, w, *extra)
    return res


def _rope_tables(pos):
    half = ROT_DIM // 2
    inv_freq = 1.0 / (ROPE_THETA ** (jnp.arange(0, ROT_DIM, 2, dtype=F32) / ROT_DIM))
    ang = pos.astype(F32)[:, None] * inv_freq[None, :]
    cos = jnp.cos(ang)
    sin = jnp.sin(ang)
    t = pos.shape[0]
    one = jnp.ones((t, DQK - ROT_DIM), F32)
    zero = jnp.zeros((t, DQK - ROT_DIM), F32)
    zh = jnp.zeros((t, half), F32)
    c64 = jnp.concatenate([cos, cos, one], axis=1)
    s1_64 = jnp.concatenate([-sin, zh, zero], axis=1)
    s2_64 = jnp.concatenate([zh, sin, zero], axis=1)
    tile2 = lambda z: jnp.concatenate([z, z], axis=1)
    return tile2(c64), tile2(s1_64), tile2(s2_64)


def _stack_maps(q):
    lane = lax.broadcasted_iota(jnp.int32, q.shape, 1)
    z = jnp.zeros_like(q)
    return jnp.concatenate([jnp.where(lane < DQK, q, z), jnp.where(lane >= DQK, q, z)], axis=0)


def _online_softmax_steps(ss, vs, ms, ls, accs):
    def bcast(m, n):
        return jnp.tile(m, (1, n // LANES)) if n >= LANES else m[:, :n]

    m_news = [jnp.maximum(m, jnp.max(s, axis=-1, keepdims=True)) for s, m in zip(ss, ms)]
    alphas = [jnp.exp(m - mn) for m, mn in zip(ms, m_news)]
    ps = [jnp.exp(s - bcast(mn, s.shape[1])) for s, mn in zip(ss, m_news)]
    l_news = [a * l + jnp.sum(p, axis=-1, keepdims=True) for a, l, p in zip(alphas, ls, ps)]
    pvs = [jnp.dot(p.astype(BF16), v, preferred_element_type=F32) for p, v in zip(ps, vs)]
    acc_news = [a * acc + pv for a, acc, pv in zip(alphas, accs, pvs)]
    return m_news, l_news, acc_news


def _attn_finish(acc, l, lam, subln_g, gate, out_scale):
    t = acc.shape[0] // 2
    o = acc[:t] / l[:t] - lam * (acc[t:] / l[t:])
    ms = jnp.mean(o * o, axis=-1, keepdims=True)
    o = o * lax.rsqrt(ms + SUBLN_EPS) * subln_g * out_scale
    return (o * gate.astype(F32)).astype(BF16)


def _prompt_attn_kernel(lam_ref, q_ref, k_ref, v_ref, g_ref, sg_ref, o_ref,
                        q12_ref, m_ref, l_ref, acc_ref, *, tq, heads, out_scale):
    qi = pl.program_id(2)
    for hp in range(heads):
        q12_ref[hp] = _stack_maps(q_ref[:, hp * DV:(hp + 1) * DV])
    m_ref[...] = jnp.full_like(m_ref, NEG_BIG)
    l_ref[...] = jnp.zeros_like(l_ref)
    acc_ref[...] = jnp.zeros_like(acc_ref)

    def tile(kj, masked):
        rows = pl.ds(pl.multiple_of(kj * tq, tq), tq)
        hcols = [slice(hp * DV, (hp + 1) * DV) for hp in range(heads)]
        ss = [lax.dot_general(q12_ref[hp], k_ref[rows, hcols[hp]], (((1,), (1,)), ((), ())),
                              preferred_element_type=F32) for hp in range(heads)]
        if masked:
            r = lax.broadcasted_iota(jnp.int32, ss[0].shape, 0) % tq
            c = lax.broadcasted_iota(jnp.int32, ss[0].shape, 1)
            ss = [jnp.where(c <= r, s, NEG_BIG) for s in ss]
        ms, ls, accs = _online_softmax_steps(
            ss, [v_ref[rows, hcols[hp]] for hp in range(heads)],
            [m_ref[hp] for hp in range(heads)], [l_ref[hp] for hp in range(heads)],
            [acc_ref[hp] for hp in range(heads)])
        for hp in range(heads):
            m_ref[hp] = ms[hp]
            l_ref[hp] = ls[hp]
            acc_ref[hp] = accs[hp]

    def body(kj, carry):
        tile(kj, False)
        return carry

    lax.fori_loop(0, qi, body, 0)
    tile(qi, True)
    lam = lam_ref[0, 0]
    for hp in range(heads):
        cols = slice(hp * DV, (hp + 1) * DV)
        o_ref[:, cols] = _attn_finish(acc_ref[hp], l_ref[hp], lam, sg_ref[...], g_ref[:, cols], out_scale)


def _prompt_attention(q, k16, v16, gates, lam, subln_g, *, batch, seq, out_scale, tq=512, heads=2):
    tq = min(tq, seq)
    nq = seq // tq
    w = heads * DV
    kern = functools.partial(_prompt_attn_kernel, tq=tq, heads=heads, out_scale=out_scale)
    return pl.pallas_call(
        kern,
        grid=(batch, N_ATT_HEADS // heads, nq),
        in_specs=[
            pl.BlockSpec(memory_space=pltpu.SMEM),
            pl.BlockSpec((tq, w), lambda b, h, i: (b * nq + i, h)),
            pl.BlockSpec((seq, w), lambda b, h, i: (b, h)),
            pl.BlockSpec((seq, w), lambda b, h, i: (b, h)),
            pl.BlockSpec((tq, w), lambda b, h, i: (b * nq + i, h)),
            pl.BlockSpec((1, DV), lambda b, h, i: (0, 0)),
        ],
        out_specs=pl.BlockSpec((tq, w), lambda b, h, i: (b * nq + i, h)),
        out_shape=jax.ShapeDtypeStruct((batch * seq, ATT_W), BF16),
        scratch_shapes=[pltpu.VMEM((heads, 2 * tq, DV), BF16),
                        pltpu.VMEM((heads, 2 * tq, LANES), F32),
                        pltpu.VMEM((heads, 2 * tq, LANES), F32),
                        pltpu.VMEM((heads, 2 * tq, DV), F32)],
        compiler_params=_cparams(("parallel", "parallel", "arbitrary")),
        name="prompt_attention",
    )(lam, q, k16, v16, gates, subln_g.reshape(1, DV))


QUADS = 4
HEADS_PER_QUAD = N_ATT_HEADS // QUADS


def _sample_attn_kernel(pt_ref, lam_ref, q_ref, *refs, t_new, pages_per_step, out_scale):
    kc_refs = refs[:pages_per_step]
    vc_refs = refs[pages_per_step:2 * pages_per_step]
    kn_ref, vn_ref, g_ref, sg_ref, o_ref, q12_ref, m_ref, l_ref, acc_ref = refs[2 * pages_per_step:]
    j = pl.program_id(1)
    rows = 2 * t_new
    qrows = HEADS_PER_QUAD * rows
    ncols = PAGE_SIZE * HEADS_PER_QUAD

    @pl.when(j == 0)
    def _():
        for h in range(N_ATT_HEADS):
            c, i = h % QUADS, h // QUADS
            q12_ref[c, i * rows:(i + 1) * rows, :] = _stack_maps(q_ref[0, :, h * DV:(h + 1) * DV])
        m_ref[...] = jnp.full_like(m_ref, NEG_BIG)
        l_ref[...] = jnp.zeros_like(l_ref)
        acc_ref[...] = jnp.zeros_like(acc_ref)

    own = (lax.broadcasted_iota(jnp.int32, (qrows, ncols), 0) // rows
           == lax.broadcasted_iota(jnp.int32, (qrows, ncols), 1) % HEADS_PER_QUAD)
    nt = (((1,), (1,)), ((), ()))
    quads = range(QUADS)
    scores = [[jnp.where(own, lax.dot_general(q12_ref[c], kc_refs[u][pl.ds(c, ncols, stride=QUADS), :].astype(BF16),
                                              nt, preferred_element_type=F32), NEG_BIG) for c in quads]
              for u in range(pages_per_step)]
    ms = [m_ref[c] for c in quads]
    ls = [l_ref[c] for c in quads]
    accs = [acc_ref[c] for c in quads]
    for u in range(pages_per_step):
        vals = [vc_refs[u][pl.ds(c, ncols, stride=QUADS), :].astype(BF16) for c in quads]
        ms, ls, accs = _online_softmax_steps(scores[u], vals, ms, ls, accs)
    for c in quads:
        m_ref[c] = ms[c]
        l_ref[c] = ls[c]
        acc_ref[c] = accs[c]

    @pl.when(j == pl.num_programs(1) - 1)
    def _():
        r = lax.broadcasted_iota(jnp.int32, (rows, t_new), 0) % t_new
        cc = lax.broadcasted_iota(jnp.int32, (rows, t_new), 1)
        causal = cc <= r
        lam = lam_ref[0, 0]
        for h in range(N_ATT_HEADS):
            c, i = h % QUADS, h // QUADS
            sl = slice(i * rows, (i + 1) * rows)
            cols = slice(h * DV, (h + 1) * DV)
            s = lax.dot_general(q12_ref[c, sl, :], kn_ref[0, :, cols], (((1,), (1,)), ((), ())),
                                preferred_element_type=F32)
            s = jnp.where(causal, s, NEG_BIG)
            _, (l,), (acc,) = _online_softmax_steps([s], [vn_ref[0, :, cols]], [m_ref[c, sl, :]],
                                                    [l_ref[c, sl, :]], [acc_ref[c, sl, :]])
            o_ref[0, :, cols] = _attn_finish(acc, l, lam, sg_ref[...], g_ref[0, :, cols], out_scale)


def _sample_attention(q, k16, v16, gates, cache_k, cache_v, page_table, lam, subln_g, *, layer, out_scale,
                      pages_per_step=4):
    b, t_new, _ = q.shape
    n_pages = page_table.shape[1]
    n_pool = cache_k.shape[1]
    pps = math.gcd(pages_per_step, n_pages)
    page_rows = PAGE_SIZE * N_ATT_HEADS
    kc2 = cache_k.reshape(-1, DV)
    vc2 = cache_v.reshape(-1, DV)
    kern = functools.partial(_sample_attn_kernel, t_new=t_new, pages_per_step=pps, out_scale=out_scale)
    tok_spec = pl.BlockSpec((1, t_new, ATT_W), lambda i, j, pt: (i, 0, 0))

    def page_spec(u):
        return pl.BlockSpec((page_rows, DV), lambda i, j, pt: (layer * n_pool + pt[i, j * pps + u], 0))

    page_specs = [page_spec(u) for u in range(pps)]
    nq = QUADS
    qrows = HEADS_PER_QUAD * 2 * t_new
    grid_spec = pltpu.PrefetchScalarGridSpec(
        num_scalar_prefetch=1,
        grid=(b, n_pages // pps),
        in_specs=[pl.BlockSpec(memory_space=pltpu.SMEM), tok_spec] + page_specs + page_specs
                 + [tok_spec, tok_spec, tok_spec, pl.BlockSpec((1, DV), lambda i, j, pt: (0, 0))],
        out_specs=tok_spec,
        scratch_shapes=[pltpu.VMEM((nq, qrows, DV), BF16),
                        pltpu.VMEM((nq, qrows, LANES), F32),
                        pltpu.VMEM((nq, qrows, LANES), F32),
                        pltpu.VMEM((nq, qrows, DV), F32)],
    )
    return pl.pallas_call(
        kern,
        grid_spec=grid_spec,
        out_shape=jax.ShapeDtypeStruct((b, t_new, ATT_W), BF16),
        compiler_params=_cparams(("parallel", "arbitrary")),
        name="sample_attention",
    )(page_table, lam, q, *([kc2] * pps), *([vc2] * pps), k16, v16, gates, subln_g.reshape(1, DV))


def _stack_heads(x):
    lane = lax.broadcasted_iota(jnp.int32, x.shape, 1)
    z = jnp.zeros_like(x)
    return jnp.concatenate([jnp.where(lane < RWKV_HEAD, x, z), jnp.where(lane >= RWKV_HEAD, x, z)], axis=0)


def _each(f, *lists):
    return [f(*args) for args in zip(*lists)]


def _mm(a, b):
    return jnp.dot(a, b, preferred_element_type=F32)


def _rwkv_chunk(rs, krs, vs, lo, wls, w0s, a0s, k_ks, k_as, r_ks, lnx_gs, lnx_bs, gates, h0s, consts):
    ones, tril, strict, incl, eye = consts
    C = lo.shape[0]
    n2 = 2 * C
    lane = lax.broadcasted_iota(jnp.int32, (C, LANES), 1)
    lo_act = jnp.where(lane < LORA, jnp.tanh(lo), lo).astype(BF16)
    pres = [_mm(lo_act, wl) for wl in wls]
    logws = _each(lambda pre, w0: -math.exp(-0.5) * jax.nn.sigmoid(w0 + pre[:, :LANES]), pres, w0s)
    a_sigs = _each(lambda pre, a0: jax.nn.sigmoid(a0 + pre[:, LANES:]), pres, a0s)

    kks = _each(lambda kr, k_k: kr * k_k, krs, k_ks)
    kk_ss = [_split_dot(kk * kk, ones) for kk in kks]
    kks = _each(lambda kk, ss: kk / jnp.maximum(jnp.sqrt(ss), 1e-12), kks, kk_ss)
    k_hs = _each(lambda kr, a_sig, k_a: kr * (1.0 + (a_sig - 1.0) * k_a), krs, a_sigs, k_as)
    rk_ss = [_split_dot(r * k_h * r_k, ones) for r, k_h, r_k in zip(rs, k_hs, r_ks)]
    bonuses = _each(lambda ss, v: ss * v, rk_ss, vs)

    def split3(x):
        l1 = x.astype(BF16)
        rem = x - l1.astype(F32)
        l2 = rem.astype(BF16)
        return l1, l2, (rem - l2.astype(F32)).astype(BF16)

    cums = [_mm(tril, l1) + _mm(tril, l2) + _mm(tril, l3) for l1, l2, l3 in map(split3, logws)]
    g_incls = [jnp.exp(cum) for cum in cums]
    g_excls = _each(lambda cum, logw: jnp.exp(cum - logw), cums, logws)
    g_invs = [jnp.exp(-cum) for cum in cums]
    g_lasts = [g[C - 1:C, :] for g in g_incls]

    a_ts = _each(lambda kk, g: _stack_heads(-kk * g).astype(BF16), kks, g_excls)
    r_ts = _each(lambda r, g: _stack_heads(r * g).astype(BF16), rs, g_incls)
    v_sts = [_stack_heads(v).astype(BF16) for v in vs]
    bks = _each(lambda kk, a_sig, k_h, g: jnp.concatenate([_stack_heads(kk * a_sig * g), _stack_heads(k_h * g)],
                                                          axis=0), kks, a_sigs, k_hs, g_invs)
    gmats = _each(lambda a_t, r_t, bk: lax.dot_general(jnp.concatenate([a_t, r_t], axis=0), bk.astype(BF16),
                                                       (((1,), (1,)), ((), ())), preferred_element_type=F32),
                  a_ts, r_ts, bks)
    a_abs = [jnp.where(strict, g[:n2, :n2], 0.0) for g in gmats]
    a_aks = [jnp.where(strict, g[:n2, n2:], 0.0).astype(BF16) for g in gmats]
    a_rbs = [jnp.where(incl, g[n2:, :n2], 0.0).astype(BF16) for g in gmats]
    a_rks = [jnp.where(incl, g[n2:, n2:], 0.0).astype(BF16) for g in gmats]

    eye2 = jnp.where(lax.broadcasted_iota(jnp.int32, (n2, n2), 0)
                     == lax.broadcasted_iota(jnp.int32, (n2, n2), 1), 1.0, 0.0)
    t_invs = [eye2 + a for a in a_abs]
    pws = [a.astype(BF16) for a in a_abs]
    for _ in range(int(math.log2(C)) - 1):
        pws = [_mm(pw, pw).astype(BF16) for pw in pws]
        t_invs = _each(lambda t, pw: t + _mm(t.astype(BF16), pw), t_invs, pws)

    h16s = [h0.astype(BF16) for h0 in h0s]
    xs = _each(lambda a_t, h16, a_ak, v_st: _mm(a_t, h16) + _mm(a_ak, v_st), a_ts, h16s, a_aks, v_sts)
    u16s = _each(lambda t, x: _mm(t.astype(BF16), x.astype(BF16)).astype(BF16), t_invs, xs)
    o_sts = _each(lambda r_t, h16, a_rb, u16, a_rk, v_st: _mm(r_t, h16) + _mm(a_rb, u16) + _mm(a_rk, v_st),
                  r_ts, h16s, a_rbs, u16s, a_rks, v_sts)
    os_ = [o_st[:C] + o_st[C:] for o_st in o_sts]

    bk_ts = _each(lambda bk, g_last: (bk * g_last).T.astype(BF16), bks, g_lasts)
    g_cols = [jnp.sum(jnp.where(eye, jnp.broadcast_to(g_last, (LANES, LANES)), 0.0), axis=1, keepdims=True)
              for g_last in g_lasts]
    h_news = _each(lambda g_col, h0, bk_t, u16, v_st: g_col * h0 + _mm(bk_t, jnp.concatenate([u16, v_st], axis=0)),
                   g_cols, h0s, bk_ts, u16s, v_sts)

    mus = [_split_dot(o, ones) * (1.0 / RWKV_HEAD) for o in os_]
    ds = _each(lambda o, mu: o - mu, os_, mus)
    vars_ = [_split_dot(d * d, ones) * (1.0 / RWKV_HEAD) for d in ds]
    ys = _each(lambda d, var, g, b, bonus, gate:
               (((d * lax.rsqrt(var + LNX_EPS) * g + b) + bonus) * gate.astype(F32)).astype(BF16),
               ds, vars_, lnx_gs, lnx_bs, bonuses, gates)
    return ys, h_news


def _rwkv_kernel(r_ref, k_ref, v_ref, lo_ref, pr_ref, pk_ref, pv_ref, plo_ref,
                 tmr_ref, tmk_ref, tmv_ref, tmlo_ref, wl_ref, w0_ref, a0_ref, kk_ref, ka_ref, rk_ref,
                 lg_ref, lb_ref, gate_ref, s0_ref, y_ref, sout_ref,
                 h_ref, cr_ref, ck_ref, cv_ref, clo_ref, *, chunk, pairs, use_s0):
    c = pl.program_id(2)
    n_chunks = pl.num_programs(2)
    C = chunk
    n2 = 2 * C

    @pl.when(c == 0)
    def _():
        cr_ref[...] = pr_ref[0]
        ck_ref[...] = pk_ref[0]
        cv_ref[...] = pv_ref[0]
        clo_ref[...] = plo_ref[0]
        if use_s0:
            z = jnp.zeros((RWKV_HEAD, RWKV_HEAD), F32)
            for u in range(pairs):
                st = jnp.concatenate([jnp.concatenate([s0_ref[0, 2 * u], z], axis=1),
                                      jnp.concatenate([z, s0_ref[0, 2 * u + 1]], axis=1)], axis=0)
                h_ref[u] = st.T
        else:
            h_ref[...] = jnp.zeros_like(h_ref)

    ri = lax.broadcasted_iota(jnp.int32, (n2, n2), 0)
    ci = lax.broadcasted_iota(jnp.int32, (n2, n2), 1)
    tril = (lax.broadcasted_iota(jnp.int32, (C, C), 1) <= lax.broadcasted_iota(jnp.int32, (C, C), 0))
    eye = (lax.broadcasted_iota(jnp.int32, (LANES, LANES), 0)
           == lax.broadcasted_iota(jnp.int32, (LANES, LANES), 1))
    consts = (_half_ones(), jnp.where(tril, 1.0, 0.0).astype(BF16), ri > ci, ri >= ci, eye)

    def token_shift(cur_ref3, carry_ref, tm_ref):
        cur = cur_ref3[0]
        row0 = lax.broadcasted_iota(jnp.int32, cur.shape, 0) == 0
        prev = jnp.where(row0, carry_ref[...], pltpu.roll(cur, 1, 0))
        carry_ref[...] = cur[C - 1:C, :]
        return cur + tm_ref[...] * (prev - cur)

    lo = token_shift(lo_ref, clo_ref, tmlo_ref)
    r_all = token_shift(r_ref, cr_ref, tmr_ref)
    k_all = token_shift(k_ref, ck_ref, tmk_ref)
    v_all = token_shift(v_ref, cv_ref, tmv_ref)
    col = lambda u: slice(u * LANES, (u + 1) * LANES)
    per_pair = lambda x: [x[:, col(u)] for u in range(pairs)]
    ys, h_news = _rwkv_chunk(
        per_pair(r_all), per_pair(k_all), per_pair(v_all), lo, [wl_ref[u] for u in range(pairs)],
        per_pair(w0_ref[...]), per_pair(a0_ref[...]), per_pair(kk_ref[...]), per_pair(ka_ref[...]),
        per_pair(rk_ref[...]), per_pair(lg_ref[...]), per_pair(lb_ref[...]), per_pair(gate_ref[0]),
        [h_ref[u] for u in range(pairs)], consts)
    for u in range(pairs):
        y_ref[0, :, col(u)] = ys[u]
        h_ref[u] = h_news[u]

    @pl.when(c == n_chunks - 1)
    def _():
        for u in range(pairs):
            st = h_news[u].T
            sout_ref[0, 2 * u] = st[:RWKV_HEAD, :RWKV_HEAD]
            sout_ref[0, 2 * u + 1] = st[RWKV_HEAD:, RWKV_HEAD:]


def _rwkv(rkv, lora, gates, prev_rkv, prev_lora, s0, p, *, chunk, pairs):
    b, t, _ = rkv.shape
    n_chunks = t // chunk
    ng = N_RWKV_PAIRS // pairs
    w = pairs * LANES
    use_s0 = s0 is not None
    sblk = (1, 2 * pairs, RWKV_HEAD, RWKV_HEAD)
    if s0 is None:
        s0 = jnp.zeros(sblk, F32)
        s0_spec = pl.BlockSpec(sblk, lambda i, j, c: (0, 0, 0, 0))
    else:
        s0_spec = pl.BlockSpec(sblk, lambda i, j, c: (i, j, 0, 0))

    def tok(off):
        return pl.BlockSpec((1, chunk, w), lambda i, j, c: (i, c, off + j))

    def prev(off):
        return pl.BlockSpec((1, 1, w), lambda i, j, c: (i, 0, off + j))

    def par(off=0):
        return pl.BlockSpec((1, w), lambda i, j, c: (0, off + j))

    kern = functools.partial(_rwkv_kernel, chunk=chunk, pairs=pairs, use_s0=use_s0)
    y, s_out = pl.pallas_call(
        kern,
        grid=(b, ng, n_chunks),
        in_specs=[
            tok(0), tok(ng), tok(2 * ng),
            pl.BlockSpec((1, chunk, LANES), lambda i, j, c: (i, c, 0)),
            prev(0), prev(ng), prev(2 * ng),
            pl.BlockSpec((1, 1, LANES), lambda i, j, c: (i, 0, 0)),
            par(0), par(ng), par(2 * ng), pl.BlockSpec((1, LANES), lambda i, j, c: (0, 0)),
            pl.BlockSpec((pairs, LANES, 2 * LANES), lambda i, j, c: (j, 0, 0)),
            par(), par(), par(), par(), par(), par(), par(),
            pl.BlockSpec((1, chunk, w), lambda i, j, c: (i, c, ng + j)),
            s0_spec,
        ],
        out_specs=[pl.BlockSpec((1, chunk, w), lambda i, j, c: (i, c, j)),
                   pl.BlockSpec(sblk, lambda i, j, c: (i, j, 0, 0))],
        out_shape=[jax.ShapeDtypeStruct((b, t, RWKV_W), BF16),
                   jax.ShapeDtypeStruct((b, N_RWKV_HEADS, RWKV_HEAD, RWKV_HEAD), F32)],
        scratch_shapes=[pltpu.VMEM((pairs, LANES, LANES), F32),
                        pltpu.VMEM((1, w), F32), pltpu.VMEM((1, w), F32),
                        pltpu.VMEM((1, w), F32), pltpu.VMEM((1, LANES), F32)],
        compiler_params=_cparams(("parallel", "parallel", "arbitrary")),
        name="rwkv7_chunked",
    )(rkv, rkv, rkv, lora, prev_rkv, prev_rkv, prev_rkv, prev_lora,
      p["tm_rkv"], p["tm_rkv"], p["tm_rkv"], p["tm_lora"], p["w_lora"],
      p["w0"], p["a0"], p["k_k"], p["k_a"], p["r_k"], p["lnx_g"], p["lnx_b"], gates, s0)
    return y, s_out


def _prep_layer_params(l, ln_g, w_in, q_norm_g, k_norm_g, lambda_q1, lambda_k1, lambda_q2, lambda_k2,
                       subln_g, time_mix, w0, w2, a0, a2, k_k, k_a, r_k, lnx_g, lnx_b, w_out):
    w = w_in[l]
    sh = OFF_SHIFT
    o1, o2, o3, o4 = RWKV_W, RWKV_W + LORA, 2 * RWKV_W + LORA, 3 * RWKV_W + LORA
    w_main = jnp.concatenate([w[:, :sh], w[:, sh:sh + o1], w[:, sh + o2:sh + o3], w[:, sh + o3:sh + o4],
                              w[:, OFF_GA:]], axis=1).astype(BF16)
    w_lora_in = jnp.concatenate([w[:, sh + o1:sh + o2], w[:, sh + o4:sh + SHIFT_W]], axis=1).astype(BF16)
    tmix = time_mix[l]
    tm_rkv = jnp.concatenate([tmix[:o1], tmix[o2:o3], tmix[o3:o4]]).reshape(1, 3 * RWKV_W)
    tm_lora = jnp.concatenate([tmix[o1:o2], tmix[o4:]]).reshape(1, 2 * LORA)
    w2p = w2[l].reshape(LORA, N_RWKV_PAIRS, LANES).transpose(1, 0, 2)
    a2p = a2[l].reshape(LORA, N_RWKV_PAIRS, LANES).transpose(1, 0, 2)
    z = jnp.zeros_like(w2p)
    w_lora = jnp.concatenate([jnp.concatenate([w2p, z], axis=2), jnp.concatenate([z, a2p], axis=2)],
                             axis=1).astype(BF16)
    lam_init = 0.8 - 0.6 * math.exp(-0.3 * l)
    lam = (jnp.exp(jnp.sum(lambda_q1[l] * lambda_k1[l])) - jnp.exp(jnp.sum(lambda_q2[l] * lambda_k2[l]))
           + lam_init).reshape(1, 1).astype(F32)
    row = lambda z_: z_.reshape(1, -1)
    return dict(
        layer=l, ln_g=ln_g[l], w_main=w_main, w_lora_in=w_lora_in, w_out=w_out[l].astype(BF16),
        q_gain=jnp.tile(q_norm_g[l], 2).reshape(1, DV), k_gain=jnp.tile(k_norm_g[l], 2).reshape(1, DV),
        lam=lam, out_scale=1.0 - lam_init, subln_g=subln_g[l],
        tm_rkv=tm_rkv, tm_lora=tm_lora, w_lora=w_lora,
        w0=row(w0[l]), a0=row(a0[l]), k_k=row(k_k[l]), k_a=row(k_a[l]), r_k=row(r_k[l]),
        lnx_g=row(lnx_g[l]), lnx_b=row(lnx_b[l]),
    )


def _layer(x, pos, shift_prev, wkv_prev, past, p, *, chunk, pairs):
    b, t, d = x.shape
    m = b * t
    x2 = x.reshape(m, d)
    h = _rmsnorm(x2, p["ln_g"])

    tm = min(1024, m)
    cos, s1, s2 = _rope_tables(pos)
    if t < tm:
        reps = tm // t
        cos, s1, s2 = (jnp.tile(z, (reps, 1)) for z in (cos, s1, s2))
    n_tab = cos.shape[0] // tm
    tab_spec = pl.BlockSpec((tm, LANES), lambda i, j, k: (i % n_tab, 0))
    gain_spec = pl.BlockSpec((1, LANES), lambda i, j, k: (0, 0))
    blk = lambda cols: cols // LANES

    def qk(col_off, gain, scale, dts):
        kern = functools.partial(_mm_qk_kernel, scale=scale, rows_per_chunk=min(256, tm))
        return _matmul(h, p["w_main"], col_block_off=blk(col_off), n_cols=ATT_W, kernel=kern,
                       out_dtypes=dts, extra=(gain, cos, s1, s2),
                       extra_specs=(gain_spec, tab_spec, tab_spec, tab_spec), tm=tm)

    (q16,) = qk(0, p["q_gain"], DQK ** -0.5, (BF16,))
    k32, k16 = qk(OFF_K, p["k_gain"], 1.0, ("f32_head_major", BF16))
    v32, v16 = _matmul(h, p["w_main"], col_block_off=blk(OFF_V), n_cols=ATT_W,
                       kernel=_mm_dual_kernel, out_dtypes=("f32_head_major", BF16), tm=tm)
    (rkv,) = _matmul(h, p["w_main"], col_block_off=blk(3 * ATT_W), n_cols=3 * RWKV_W,
                     kernel=_mm_plain_kernel, out_dtypes=(F32,), tm=tm)
    (gates,) = _matmul(h, p["w_main"], col_block_off=blk(3 * ATT_W + 3 * RWKV_W), n_cols=ATT_W + RWKV_W,
                       kernel=_mm_silu_kernel, out_dtypes=(BF16,), tm=tm)
    (lora,) = _matmul(h, p["w_lora_in"], col_block_off=0, n_cols=2 * LORA,
                      kernel=_mm_plain_kernel, out_dtypes=(F32,), tm=tm)

    if past is None:
        y_att = _prompt_attention(q16, k16, v16, gates, p["lam"], p["subln_g"],
                                  batch=b, seq=t, out_scale=p["out_scale"])
    else:
        cache_k, cache_v, page_table = past
        y_att = _sample_attention(q16.reshape(b, t, ATT_W), k16.reshape(b, t, ATT_W),
                                  v16.reshape(b, t, ATT_W), gates.reshape(b, t, -1),
                                  cache_k, cache_v, page_table, p["lam"], p["subln_g"],
                                  layer=p["layer"], out_scale=p["out_scale"]).reshape(m, ATT_W)

    o1, o2, o3, o4 = RWKV_W, RWKV_W + LORA, 2 * RWKV_W + LORA, 3 * RWKV_W + LORA
    prev_rkv = jnp.concatenate([shift_prev[:, :o1], shift_prev[:, o2:o3], shift_prev[:, o3:o4]],
                               axis=1).reshape(b, 1, 3 * RWKV_W)
    prev_lora = jnp.concatenate([shift_prev[:, o1:o2], shift_prev[:, o4:]], axis=1).reshape(b, 1, 2 * LORA)
    rkv3 = rkv.reshape(b, t, 3 * RWKV_W)
    lora3 = lora.reshape(b, t, 2 * LORA)
    y_rwkv, wkv_new = _rwkv(rkv3, lora3, gates.reshape(b, t, -1), prev_rkv, prev_lora, wkv_prev, p,
                            chunk=chunk, pairs=pairs)
    last_rkv = rkv3[:, -1]
    last_lora = lora3[:, -1]
    shift_new = jnp.concatenate([last_rkv[:, :RWKV_W], last_lora[:, :LORA], last_rkv[:, RWKV_W:],
                                 last_lora[:, LORA:]], axis=1)

    y = jnp.concatenate([y_att, y_rwkv.reshape(m, RWKV_W)], axis=1)
    x_spec = pl.BlockSpec((tm, 1024), lambda i, j, k: (i, j))
    (out,) = _matmul(y, p["w_out"], col_block_off=0, n_cols=d, kernel=_mm_residual_kernel,
                     out_dtypes=(F32,), extra=(x2,), extra_specs=(x_spec,), tm=tm)
    return (out.reshape(b, t, d), k32.reshape(b, t, N_ATT_HEADS, DV), v32.reshape(b, t, N_ATT_HEADS, DV),
            shift_new, wkv_new)


def kernel(x_prompt, x_sample, cache_k, cache_v, state_shift, state_wkv, page_table, ln_g, w_in, q_norm_g,
           k_norm_g, lambda_q1, lambda_k1, lambda_q2, lambda_k2, subln_g, time_mix, w0, w2, a0, a2, k_k,
           k_a, r_k, lnx_g, lnx_b, w_out):
    depth = w_in.shape[0]
    b_p, s_p = x_prompt.shape[0], x_prompt.shape[1]
    s_s = x_sample.shape[1]
    past_len = page_table.shape[1] * PAGE_SIZE
    pos_p = jnp.arange(s_p)
    pos_s = past_len + jnp.arange(s_s)
    yp, ys = x_prompt, x_sample
    outs = [[] for _ in range(8)]
    for l in range(depth):
        p = _prep_layer_params(l, ln_g, w_in, q_norm_g, k_norm_g, lambda_q1, lambda_k1, lambda_q2,
                               lambda_k2, subln_g, time_mix, w0, w2, a0, a2, k_k, k_a, r_k, lnx_g, lnx_b,
                               w_out)
        shift0 = jnp.zeros((b_p, SHIFT_W), F32)
        yp, kp, vp, shp, wkvp = _layer(yp, pos_p, shift0, None, None, p, chunk=64, pairs=8)
        past = (cache_k, cache_v, page_table)
        ys, ks, vs, shs, wkvs = _layer(ys, pos_s, state_shift[l], state_wkv[l], past, p, chunk=s_s, pairs=16)
        for lst, val in zip(outs, (kp, vp, shp, wkvp, ks, vs, shs, wkvs)):
            lst.append(val)
    stacked = [jnp.stack(lst, 0) for lst in outs]
    return (yp, ys, *stacked)
```

```python
import functools
import math

import jax
import jax.numpy as jnp
from jax import lax
from jax.experimental import pallas as pl
from jax.experimental.pallas import tpu as pltpu

F32 = jnp.float32
BF16 = jnp.bfloat16

D_MODEL = 4096
PAGE_SIZE = 128
ATT_W = 2048
RWKV_W = 2048
DV = 128
N_ATT_HEADS = ATT_W // DV
DQK = DV // 2
ROT_DIM = DQK // 4
ROPE_THETA = 500000.0
RWKV_HEAD = 64
N_RWKV_HEADS = RWKV_W // RWKV_HEAD
N_RWKV_PAIRS = N_RWKV_HEADS // 2
LORA = 64
SHIFT_W = 3 * RWKV_W + 2 * LORA
OFF_K = ATT_W
OFF_V = 2 * ATT_W
OFF_SHIFT = 3 * ATT_W
OFF_GA = OFF_SHIFT + SHIFT_W
OFF_GR = OFF_GA + ATT_W
IN_W = OFF_GR + RWKV_W
NORM_EPS = 1e-6
SUBLN_EPS = 1e-5
LNX_EPS = 64e-5

LANES = 128
VMEM_LIMIT_BYTES = 48 * 1024 * 1024
NEG_BIG = -1e30


def _cparams(sem):
    return pltpu.CompilerParams(dimension_semantics=sem, vmem_limit_bytes=VMEM_LIMIT_BYTES)


def _split_dot(x, w2_bf16):
    hi = x.astype(BF16)
    lo = (x - hi.astype(F32)).astype(BF16)
    return jnp.dot(jnp.concatenate([hi, lo], axis=1), w2_bf16, preferred_element_type=F32)


def _half_ones():
    r = (lax.broadcasted_iota(jnp.int32, (2 * LANES, LANES), 0) % LANES) // RWKV_HEAD
    c = lax.broadcasted_iota(jnp.int32, (2 * LANES, LANES), 1) // RWKV_HEAD
    return jnp.where(r == c, 1.0, 0.0).astype(BF16)


def _rmsnorm_kernel(x_ref, g_ref, o_ref):
    x = x_ref[...]
    ms = jnp.mean(x * x, axis=-1, keepdims=True)
    o_ref[...] = (x * lax.rsqrt(ms + NORM_EPS) * g_ref[...]).astype(BF16)


def _rmsnorm(x2, g, tm=256):
    m, d = x2.shape
    tm = min(tm, m)
    return pl.pallas_call(
        _rmsnorm_kernel,
        grid=(m // tm,),
        in_specs=[pl.BlockSpec((tm, d), lambda i: (i, 0)),
                  pl.BlockSpec((1, d), lambda i: (0, 0))],
        out_specs=pl.BlockSpec((tm, d), lambda i: (i, 0)),
        out_shape=jax.ShapeDtypeStruct((m, d), BF16),
        compiler_params=_cparams(("parallel",)),
        name="rmsnorm",
    )(x2, g.reshape(1, d))


def _mm_full_k(lhs_refs, w_ref):
    acc, off = None, 0
    for l_ref in lhs_refs:
        k = l_ref.shape[1]
        part = jnp.dot(l_ref[...], w_ref[off:off + k, :], preferred_element_type=F32)
        acc = part if acc is None else acc + part
        off += k
    return acc


def _mm_plain_kernel(h_ref, w_ref, o_ref):
    o_ref[...] = _mm_full_k([h_ref], w_ref).astype(o_ref.dtype)


def _mm_dual_kernel(h_ref, w_ref, o32_ref, o16_ref):
    a = _mm_full_k([h_ref], w_ref)
    o16_ref[...] = a.astype(BF16)
    for hh in range(a.shape[1] // LANES):
        o32_ref[:, hh, :] = a[:, hh * LANES:(hh + 1) * LANES]


def _mm_silu_kernel(h_ref, w_ref, o_ref):
    a = _mm_full_k([h_ref], w_ref)
    o_ref[...] = (a * jax.nn.sigmoid(a)).astype(o_ref.dtype)


def _mm_residual_kernel(ya_ref, yr_ref, w_ref, x_ref, o_ref):
    o_ref[...] = x_ref[...] + _mm_full_k([ya_ref, yr_ref], w_ref)


def _mm_qk_kernel(h_ref, w_ref, g_ref, cos_ref, s1_ref, s2_ref, *rest, scale, rows_per_chunk):
    outs, acc_ref = rest[:-1], rest[-1]
    acc_ref[...] = _mm_full_k([h_ref], w_ref)
    tm, tn = acc_ref.shape
    ones = _half_ones()
    gain = g_ref[...]

    def chunk(r, carry):
        rows = pl.ds(pl.multiple_of(r * rows_per_chunk, rows_per_chunk), rows_per_chunk)
        c = cos_ref[rows, :]
        s1 = s1_ref[rows, :]
        s2 = s2_ref[rows, :]
        for hh in range(tn // LANES):
            cols = slice(hh * LANES, (hh + 1) * LANES)
            x = acc_ref[rows, cols]
            ss = _split_dot(x * x, ones)
            y = x * lax.rsqrt(ss * (1.0 / DQK) + NORM_EPS) * gain
            y = y * c + pltpu.roll(y, LANES - ROT_DIM // 2, 1) * s1 + pltpu.roll(y, ROT_DIM // 2, 1) * s2
            if scale != 1.0:
                y = y * scale
            for o_ref in outs:
                if len(o_ref.shape) == 3:
                    o_ref[rows, hh, :] = y.astype(o_ref.dtype)
                else:
                    o_ref[rows, cols] = y.astype(o_ref.dtype)
        return carry

    lax.fori_loop(0, tm // rows_per_chunk, chunk, 0)


def _matmul(lhs, w, *, col_block_off, n_cols, kernel, out_dtypes, extra=(), extra_specs=(),
            scratch=False, tm=1024, tn=512):
    m = lhs[0].shape[0]
    tm = min(tm, m)
    tn = min(tn, n_cols)
    assert m % tm == 0 and n_cols % tn == 0 and sum(l.shape[1] for l in lhs) == w.shape[0]
    assert (col_block_off * LANES) % tn == 0
    joff = col_block_off * LANES // tn
    in_specs = [pl.BlockSpec((tm, l.shape[1]), lambda i, j: (i, 0)) for l in lhs]
    in_specs.append(pl.BlockSpec((w.shape[0], tn), lambda i, j: (0, j + joff)))
    in_specs += list(extra_specs)
    out_specs, out_shape = [], []
    for dt in out_dtypes:
        if dt == "f32_head_major":
            out_specs.append(pl.BlockSpec((tm, tn // LANES, LANES), lambda i, j: (i, j, 0)))
            out_shape.append(jax.ShapeDtypeStruct((m, n_cols // LANES, LANES), F32))
        else:
            out_specs.append(pl.BlockSpec((tm, tn), lambda i, j: (i, j)))
            out_shape.append(jax.ShapeDtypeStruct((m, n_cols), dt))
    return pl.pallas_call(
        kernel,
        grid=(m // tm, n_cols // tn),
        in_specs=in_specs,
        out_specs=out_specs,
        out_shape=out_shape,
        scratch_shapes=[pltpu.VMEM((tm, tn), F32)] if scratch else [],
        compiler_params=_cparams(("parallel", "arbitrary")),
        name="proj_matmul",
    )(*lhs, w, *extra)


def _rope_tables(pos):
    half = ROT_DIM // 2
    inv_freq = 1.0 / (ROPE_THETA ** (jnp.arange(0, ROT_DIM, 2, dtype=F32) / ROT_DIM))
    ang = pos.astype(F32)[:, None] * inv_freq[None, :]
    cos = jnp.cos(ang)
    sin = jnp.sin(ang)
    t = pos.shape[0]
    one = jnp.ones((t, DQK - ROT_DIM), F32)
    zero = jnp.zeros((t, DQK - ROT_DIM), F32)
    zh = jnp.zeros((t, half), F32)
    c64 = jnp.concatenate([cos, cos, one], axis=1)
    s1_64 = jnp.concatenate([-sin, zh, zero], axis=1)
    s2_64 = jnp.concatenate([zh, sin, zero], axis=1)
    tile2 = lambda z: jnp.concatenate([z, z], axis=1)
    return tile2(c64), tile2(s1_64), tile2(s2_64)


def _stack_maps(q):
    lane = lax.broadcasted_iota(jnp.int32, q.shape, 1)
    z = jnp.zeros_like(q)
    return jnp.concatenate([jnp.where(lane < DQK, q, z), jnp.where(lane >= DQK, q, z)], axis=0)


def _online_softmax_steps(ss, vs, ms, ls, accs):
    def bcast(m, n):
        return jnp.tile(m, (1, n // LANES)) if n >= LANES else m[:, :n]

    m_news = [jnp.maximum(m, jnp.max(s, axis=-1, keepdims=True)) for s, m in zip(ss, ms)]
    alphas = [jnp.exp(m - mn) for m, mn in zip(ms, m_news)]
    ps = [jnp.exp(s - bcast(mn, s.shape[1])) for s, mn in zip(ss, m_news)]
    l_news = [a * l + jnp.sum(p, axis=-1, keepdims=True) for a, l, p in zip(alphas, ls, ps)]
    pvs = [jnp.dot(p.astype(BF16), v, preferred_element_type=F32) for p, v in zip(ps, vs)]
    acc_news = [a * acc + pv for a, acc, pv in zip(alphas, accs, pvs)]
    return m_news, l_news, acc_news


def _attn_finish(acc, l, lam, subln_g, gate, out_scale):
    t = acc.shape[0] // 2
    o = acc[:t] / l[:t] - lam * (acc[t:] / l[t:])
    ms = jnp.mean(o * o, axis=-1, keepdims=True)
    o = o * lax.rsqrt(ms + SUBLN_EPS) * subln_g * out_scale
    return (o * gate.astype(F32)).astype(BF16)


def _prompt_attn_kernel(lam_ref, q_ref, k_ref, v_ref, g_ref, sg_ref, o_ref,
                        q12_ref, m_ref, l_ref, acc_ref, *, tq, heads, out_scale):
    qi = pl.program_id(2)
    for hp in range(heads):
        q12_ref[hp] = _stack_maps(q_ref[:, hp * DV:(hp + 1) * DV])
    m_ref[...] = jnp.full_like(m_ref, NEG_BIG)
    l_ref[...] = jnp.zeros_like(l_ref)
    acc_ref[...] = jnp.zeros_like(acc_ref)

    def tile(kj, masked):
        rows = pl.ds(pl.multiple_of(kj * tq, tq), tq)
        hcols = [slice(hp * DV, (hp + 1) * DV) for hp in range(heads)]
        ss = [lax.dot_general(q12_ref[hp], k_ref[rows, hcols[hp]], (((1,), (1,)), ((), ())),
                              preferred_element_type=F32) for hp in range(heads)]
        if masked:
            r = lax.broadcasted_iota(jnp.int32, ss[0].shape, 0) % tq
            c = lax.broadcasted_iota(jnp.int32, ss[0].shape, 1)
            ss = [jnp.where(c <= r, s, NEG_BIG) for s in ss]
        ms, ls, accs = _online_softmax_steps(
            ss, [v_ref[rows, hcols[hp]] for hp in range(heads)],
            [m_ref[hp] for hp in range(heads)], [l_ref[hp] for hp in range(heads)],
            [acc_ref[hp] for hp in range(heads)])
        for hp in range(heads):
            m_ref[hp] = ms[hp]
            l_ref[hp] = ls[hp]
            acc_ref[hp] = accs[hp]

    def body(kj, carry):
        tile(kj, False)
        return carry

    lax.fori_loop(0, qi, body, 0)
    tile(qi, True)
    lam = lam_ref[0, 0]
    for hp in range(heads):
        cols = slice(hp * DV, (hp + 1) * DV)
        o_ref[:, cols] = _attn_finish(acc_ref[hp], l_ref[hp], lam, sg_ref[...], g_ref[:, cols], out_scale)


def _prompt_attention(q, k16, v16, gates, lam, subln_g, *, batch, seq, out_scale, tq=512, heads=2):
    tq = min(tq, seq)
    nq = seq // tq
    w = heads * DV
    kern = functools.partial(_prompt_attn_kernel, tq=tq, heads=heads, out_scale=out_scale)
    return pl.pallas_call(
        kern,
        grid=(batch, N_ATT_HEADS // heads, nq),
        in_specs=[
            pl.BlockSpec(memory_space=pltpu.SMEM),
            pl.BlockSpec((tq, w), lambda b, h, i: (b * nq + i, h)),
            pl.BlockSpec((seq, w), lambda b, h, i: (b, h)),
            pl.BlockSpec((seq, w), lambda b, h, i: (b, h)),
            pl.BlockSpec((tq, w), lambda b, h, i: (b * nq + i, h)),
            pl.BlockSpec((1, DV), lambda b, h, i: (0, 0)),
        ],
        out_specs=pl.BlockSpec((tq, w), lambda b, h, i: (b * nq + i, h)),
        out_shape=jax.ShapeDtypeStruct((batch * seq, ATT_W), BF16),
        scratch_shapes=[pltpu.VMEM((heads, 2 * tq, DV), BF16),
                        pltpu.VMEM((heads, 2 * tq, LANES), F32),
                        pltpu.VMEM((heads, 2 * tq, LANES), F32),
                        pltpu.VMEM((heads, 2 * tq, DV), F32)],
        compiler_params=_cparams(("parallel", "parallel", "arbitrary")),
        name="prompt_attention",
    )(lam, q, k16, v16, gates, subln_g.reshape(1, DV))


QUADS = 4
HEADS_PER_QUAD = N_ATT_HEADS // QUADS


def _sample_attn_kernel(pt_ref, lam_ref, q_ref, *refs, t_new, pages_per_step, out_scale):
    kc_refs = refs[:pages_per_step]
    vc_refs = refs[pages_per_step:2 * pages_per_step]
    kn_ref, vn_ref, g_ref, sg_ref, o_ref, q12_ref, m_ref, l_ref, acc_ref = refs[2 * pages_per_step:]
    j = pl.program_id(1)
    rows = 2 * t_new
    qrows = HEADS_PER_QUAD * rows
    ncols = PAGE_SIZE * HEADS_PER_QUAD

    @pl.when(j == 0)
    def _():
        for h in range(N_ATT_HEADS):
            c, i = h % QUADS, h // QUADS
            q12_ref[c, i * rows:(i + 1) * rows, :] = _stack_maps(q_ref[0, :, h * DV:(h + 1) * DV])
        m_ref[...] = jnp.full_like(m_ref, NEG_BIG)
        l_ref[...] = jnp.zeros_like(l_ref)
        acc_ref[...] = jnp.zeros_like(acc_ref)

    own = (lax.broadcasted_iota(jnp.int32, (qrows, ncols), 0) // rows
           == lax.broadcasted_iota(jnp.int32, (qrows, ncols), 1) % HEADS_PER_QUAD)
    nt = (((1,), (1,)), ((), ()))
    quads = range(QUADS)
    scores = [[jnp.where(own, lax.dot_general(q12_ref[c], kc_refs[u][pl.ds(c, ncols, stride=QUADS), :].astype(BF16),
                                              nt, preferred_element_type=F32), NEG_BIG) for c in quads]
              for u in range(pages_per_step)]
    ms = [m_ref[c] for c in quads]
    ls = [l_ref[c] for c in quads]
    accs = [acc_ref[c] for c in quads]
    for u in range(pages_per_step):
        vals = [vc_refs[u][pl.ds(c, ncols, stride=QUADS), :].astype(BF16) for c in quads]
        ms, ls, accs = _online_softmax_steps(scores[u], vals, ms, ls, accs)
    for c in quads:
        m_ref[c] = ms[c]
        l_ref[c] = ls[c]
        acc_ref[c] = accs[c]

    @pl.when(j == pl.num_programs(1) - 1)
    def _():
        r = lax.broadcasted_iota(jnp.int32, (rows, t_new), 0) % t_new
        cc = lax.broadcasted_iota(jnp.int32, (rows, t_new), 1)
        causal = cc <= r
        lam = lam_ref[0, 0]
        for h in range(N_ATT_HEADS):
            c, i = h % QUADS, h // QUADS
            sl = slice(i * rows, (i + 1) * rows)
            cols = slice(h * DV, (h + 1) * DV)
            s = lax.dot_general(q12_ref[c, sl, :], kn_ref[0, :, cols], (((1,), (1,)), ((), ())),
                                preferred_element_type=F32)
            s = jnp.where(causal, s, NEG_BIG)
            _, (l,), (acc,) = _online_softmax_steps([s], [vn_ref[0, :, cols]], [m_ref[c, sl, :]],
                                                    [l_ref[c, sl, :]], [acc_ref[c, sl, :]])
            o_ref[0, :, cols] = _attn_finish(acc, l, lam, sg_ref[...], g_ref[0, :, cols], out_scale)


def _sample_attention(q, k16, v16, gates, cache_k, cache_v, page_table, lam, subln_g, *, layer, out_scale,
                      pages_per_step=8):
    b, t_new, _ = q.shape
    n_pages = page_table.shape[1]
    n_pool = cache_k.shape[1]
    pps = math.gcd(pages_per_step, n_pages)
    page_rows = PAGE_SIZE * N_ATT_HEADS
    kc2 = cache_k.reshape(-1, DV)
    vc2 = cache_v.reshape(-1, DV)
    kern = functools.partial(_sample_attn_kernel, t_new=t_new, pages_per_step=pps, out_scale=out_scale)
    tok_spec = pl.BlockSpec((1, t_new, ATT_W), lambda i, j, pt: (i, 0, 0))

    def page_spec(u):
        return pl.BlockSpec((page_rows, DV), lambda i, j, pt: (layer * n_pool + pt[i, j * pps + u], 0))

    page_specs = [page_spec(u) for u in range(pps)]
    nq = QUADS
    qrows = HEADS_PER_QUAD * 2 * t_new
    grid_spec = pltpu.PrefetchScalarGridSpec(
        num_scalar_prefetch=1,
        grid=(b, n_pages // pps),
        in_specs=[pl.BlockSpec(memory_space=pltpu.SMEM), tok_spec] + page_specs + page_specs
                 + [tok_spec, tok_spec, tok_spec, pl.BlockSpec((1, DV), lambda i, j, pt: (0, 0))],
        out_specs=tok_spec,
        scratch_shapes=[pltpu.VMEM((nq, qrows, DV), BF16),
                        pltpu.VMEM((nq, qrows, LANES), F32),
                        pltpu.VMEM((nq, qrows, LANES), F32),
                        pltpu.VMEM((nq, qrows, DV), F32)],
    )
    return pl.pallas_call(
        kern,
        grid_spec=grid_spec,
        out_shape=jax.ShapeDtypeStruct((b, t_new, ATT_W), BF16),
        compiler_params=_cparams(("parallel", "arbitrary")),
        name="sample_attention",
    )(page_table, lam, q, *([kc2] * pps), *([vc2] * pps), k16, v16, gates, subln_g.reshape(1, DV))


def _stack_heads(x):
    lane = lax.broadcasted_iota(jnp.int32, x.shape, 1)
    z = jnp.zeros_like(x)
    return jnp.concatenate([jnp.where(lane < RWKV_HEAD, x, z), jnp.where(lane >= RWKV_HEAD, x, z)], axis=0)


def _each(f, *lists):
    return [f(*args) for args in zip(*lists)]


def _mm(a, b):
    return jnp.dot(a, b, preferred_element_type=F32)


def _rwkv_chunk(rs, krs, vs, lo, wls, w0s, a0s, k_ks, k_as, r_ks, lnx_gs, lnx_bs, gates, h0s, consts):
    ones, tril, strict, incl, eye = consts
    C = lo.shape[0]
    n2 = 2 * C
    lane = lax.broadcasted_iota(jnp.int32, (C, LANES), 1)
    lo_act = jnp.where(lane < LORA, jnp.tanh(lo), lo).astype(BF16)
    pres = [_mm(lo_act, wl) for wl in wls]
    logws = _each(lambda pre, w0: -math.exp(-0.5) * jax.nn.sigmoid(w0 + pre[:, :LANES]), pres, w0s)
    a_sigs = _each(lambda pre, a0: jax.nn.sigmoid(a0 + pre[:, LANES:]), pres, a0s)

    kks = _each(lambda kr, k_k: kr * k_k, krs, k_ks)
    kk_ss = [_split_dot(kk * kk, ones) for kk in kks]
    kks = _each(lambda kk, ss: kk / jnp.maximum(jnp.sqrt(ss), 1e-12), kks, kk_ss)
    k_hs = _each(lambda kr, a_sig, k_a: kr * (1.0 + (a_sig - 1.0) * k_a), krs, a_sigs, k_as)
    rk_ss = [_split_dot(r * k_h * r_k, ones) for r, k_h, r_k in zip(rs, k_hs, r_ks)]
    bonuses = _each(lambda ss, v: ss * v, rk_ss, vs)

    def split3(x):
        l1 = x.astype(BF16)
        rem = x - l1.astype(F32)
        l2 = rem.astype(BF16)
        return jnp.concatenate([l1, l2, (rem - l2.astype(F32)).astype(BF16)], axis=0)

    cums = [_mm(tril, split3(logw)) for logw in logws]
    g_incls = [jnp.exp(cum) for cum in cums]
    g_excls = _each(lambda cum, logw: jnp.exp(cum - logw), cums, logws)
    g_invs = [jnp.exp(-cum) for cum in cums]
    g_lasts = [g[C - 1:C, :] for g in g_incls]

    a_ts = _each(lambda kk, g: _stack_heads(-kk * g).astype(BF16), kks, g_excls)
    r_ts = _each(lambda r, g: _stack_heads(r * g).astype(BF16), rs, g_incls)
    v_sts = [_stack_heads(v).astype(BF16) for v in vs]
    bks = _each(lambda kk, a_sig, k_h, g: jnp.concatenate([_stack_heads(kk * a_sig * g), _stack_heads(k_h * g)],
                                                          axis=0), kks, a_sigs, k_hs, g_invs)
    gmats = _each(lambda a_t, r_t, bk: lax.dot_general(jnp.concatenate([a_t, r_t], axis=0), bk.astype(BF16),
                                                       (((1,), (1,)), ((), ())), preferred_element_type=F32),
                  a_ts, r_ts, bks)
    a_abs = [jnp.where(strict, g[:n2, :n2], 0.0) for g in gmats]
    a_aks = [jnp.where(strict, g[:n2, n2:], 0.0).astype(BF16) for g in gmats]
    a_rbs = [jnp.where(incl, g[n2:, :n2], 0.0).astype(BF16) for g in gmats]
    a_rks = [jnp.where(incl, g[n2:, n2:], 0.0).astype(BF16) for g in gmats]

    eye2 = jnp.where(lax.broadcasted_iota(jnp.int32, (n2, n2), 0)
                     == lax.broadcasted_iota(jnp.int32, (n2, n2), 1), 1.0, 0.0)
    t_invs = [eye2 + a for a in a_abs]
    pws = [a.astype(BF16) for a in a_abs]
    for _ in range(int(math.log2(C)) - 1):
        pws = [_mm(pw, pw).astype(BF16) for pw in pws]
        t_invs = _each(lambda t, pw: t + _mm(t.astype(BF16), pw), t_invs, pws)

    h16s = [h0.astype(BF16) for h0 in h0s]
    if n2 % LANES == 0:
        cat = jnp.concatenate
        xs = _each(lambda a_t, h16, a_ak, v_st: _mm(cat([a_t, a_ak], axis=1), cat([h16, v_st], axis=0)),
                   a_ts, h16s, a_aks, v_sts)
    else:
        xs = _each(lambda a_t, h16, a_ak, v_st: _mm(a_t, h16) + _mm(a_ak, v_st), a_ts, h16s, a_aks, v_sts)
    u16s = _each(lambda t, x: _mm(t.astype(BF16), x.astype(BF16)).astype(BF16), t_invs, xs)
    if n2 % LANES == 0:
        o_sts = _each(lambda r_t, h16, a_rb, u16, a_rk, v_st:
                      _mm(cat([r_t, a_rb, a_rk], axis=1), cat([h16, u16, v_st], axis=0)),
                      r_ts, h16s, a_rbs, u16s, a_rks, v_sts)
    else:
        o_sts = _each(lambda r_t, h16, a_rb, u16, a_rk, v_st: _mm(r_t, h16) + _mm(a_rb, u16) + _mm(a_rk, v_st),
                      r_ts, h16s, a_rbs, u16s, a_rks, v_sts)
    os_ = [o_st[:C] + o_st[C:] for o_st in o_sts]

    bk_ts = _each(lambda bk, g_last: (bk * g_last).T.astype(BF16), bks, g_lasts)
    g_cols = [jnp.sum(jnp.where(eye, jnp.broadcast_to(g_last, (LANES, LANES)), 0.0), axis=1, keepdims=True)
              for g_last in g_lasts]
    h_news = _each(lambda g_col, h0, bk_t, u16, v_st: g_col * h0 + _mm(bk_t, jnp.concatenate([u16, v_st], axis=0)),
                   g_cols, h0s, bk_ts, u16s, v_sts)

    mus = [_split_dot(o, ones) * (1.0 / RWKV_HEAD) for o in os_]
    ds = _each(lambda o, mu: o - mu, os_, mus)
    vars_ = [_split_dot(d * d, ones) * (1.0 / RWKV_HEAD) for d in ds]
    ys = _each(lambda d, var, g, b, bonus, gate:
               (((d * lax.rsqrt(var + LNX_EPS) * g + b) + bonus) * gate.astype(F32)).astype(BF16),
               ds, vars_, lnx_gs, lnx_bs, bonuses, gates)
    return ys, h_news


def _rwkv_kernel(r_ref, k_ref, v_ref, lo_ref, pr_ref, pk_ref, pv_ref, plo_ref,
                 tmr_ref, tmk_ref, tmv_ref, tmlo_ref, wl_ref, w0_ref, a0_ref, kk_ref, ka_ref, rk_ref,
                 lg_ref, lb_ref, gate_ref, s0_ref, y_ref, sout_ref,
                 h_ref, cr_ref, ck_ref, cv_ref, clo_ref, *, chunk, pairs, use_s0):
    c = pl.program_id(2)
    n_chunks = pl.num_programs(2)
    C = chunk
    n2 = 2 * C

    @pl.when(c == 0)
    def _():
        cr_ref[...] = pr_ref[0]
        ck_ref[...] = pk_ref[0]
        cv_ref[...] = pv_ref[0]
        clo_ref[...] = plo_ref[0]
        if use_s0:
            z = jnp.zeros((RWKV_HEAD, RWKV_HEAD), F32)
            for u in range(pairs):
                st = jnp.concatenate([jnp.concatenate([s0_ref[0, 2 * u], z], axis=1),
                                      jnp.concatenate([z, s0_ref[0, 2 * u + 1]], axis=1)], axis=0)
                h_ref[u] = st.T
        else:
            h_ref[...] = jnp.zeros_like(h_ref)

    ri = lax.broadcasted_iota(jnp.int32, (n2, n2), 0)
    ci = lax.broadcasted_iota(jnp.int32, (n2, n2), 1)
    tril = (lax.broadcasted_iota(jnp.int32, (C, 3 * C), 1) % C <= lax.broadcasted_iota(jnp.int32, (C, 3 * C), 0))
    eye = (lax.broadcasted_iota(jnp.int32, (LANES, LANES), 0)
           == lax.broadcasted_iota(jnp.int32, (LANES, LANES), 1))
    consts = (_half_ones(), jnp.where(tril, 1.0, 0.0).astype(BF16), ri > ci, ri >= ci, eye)

    def token_shift(cur_ref3, carry_ref, tm_ref):
        cur = cur_ref3[0]
        row0 = lax.broadcasted_iota(jnp.int32, cur.shape, 0) == 0
        prev = jnp.where(row0, carry_ref[...], pltpu.roll(cur, 1, 0))
        carry_ref[...] = cur[C - 1:C, :]
        return cur + tm_ref[...] * (prev - cur)

    lo = token_shift(lo_ref, clo_ref, tmlo_ref)
    r_all = token_shift(r_ref, cr_ref, tmr_ref)
    k_all = token_shift(k_ref, ck_ref, tmk_ref)
    v_all = token_shift(v_ref, cv_ref, tmv_ref)
    col = lambda u: slice(u * LANES, (u + 1) * LANES)
    per_pair = lambda x: [x[:, col(u)] for u in range(pairs)]
    ys, h_news = _rwkv_chunk(
        per_pair(r_all), per_pair(k_all), per_pair(v_all), lo, [wl_ref[u] for u in range(pairs)],
        per_pair(w0_ref[...]), per_pair(a0_ref[...]), per_pair(kk_ref[...]), per_pair(ka_ref[...]),
        per_pair(rk_ref[...]), per_pair(lg_ref[...]), per_pair(lb_ref[...]), per_pair(gate_ref[0]),
        [h_ref[u] for u in range(pairs)], consts)
    for u in range(pairs):
        y_ref[0, :, col(u)] = ys[u]
        h_ref[u] = h_news[u]

    @pl.when(c == n_chunks - 1)
    def _():
        for u in range(pairs):
            st = h_news[u].T
            sout_ref[0, 2 * u] = st[:RWKV_HEAD, :RWKV_HEAD]
            sout_ref[0, 2 * u + 1] = st[RWKV_HEAD:, RWKV_HEAD:]


def _rwkv(rkv, lora, gates, prevs, s0, p, *, chunk, pairs):
    b, t, _ = lora.shape
    n_chunks = t // chunk
    ng = N_RWKV_PAIRS // pairs
    w = pairs * LANES
    use_s0 = s0 is not None
    sblk = (1, 2 * pairs, RWKV_HEAD, RWKV_HEAD)
    if s0 is None:
        s0 = jnp.zeros(sblk, F32)
        s0_spec = pl.BlockSpec(sblk, lambda i, j, c: (0, 0, 0, 0))
    else:
        s0_spec = pl.BlockSpec(sblk, lambda i, j, c: (i, j, 0, 0))

    tok = pl.BlockSpec((1, chunk, w), lambda i, j, c: (i, c, j))
    prev = pl.BlockSpec((1, 1, w), lambda i, j, c: (i, 0, j))

    def par():
        return pl.BlockSpec((1, w), lambda i, j, c: (0, j))

    kern = functools.partial(_rwkv_kernel, chunk=chunk, pairs=pairs, use_s0=use_s0)
    y, s_out = pl.pallas_call(
        kern,
        grid=(b, ng, n_chunks),
        in_specs=[
            tok, tok, tok,
            pl.BlockSpec((1, chunk, LANES), lambda i, j, c: (i, c, 0)),
            prev, prev, prev,
            pl.BlockSpec((1, 1, LANES), lambda i, j, c: (i, 0, 0)),
            par(), par(), par(), pl.BlockSpec((1, LANES), lambda i, j, c: (0, 0)),
            pl.BlockSpec((pairs, LANES, 2 * LANES), lambda i, j, c: (j, 0, 0)),
            par(), par(), par(), par(), par(), par(), par(),
            pl.BlockSpec((1, chunk, w), lambda i, j, c: (i, c, ng + j)),
            s0_spec,
        ],
        out_specs=[pl.BlockSpec((1, chunk, w), lambda i, j, c: (i, c, j)),
                   pl.BlockSpec(sblk, lambda i, j, c: (i, j, 0, 0))],
        out_shape=[jax.ShapeDtypeStruct((b, t, RWKV_W), BF16),
                   jax.ShapeDtypeStruct((b, N_RWKV_HEADS, RWKV_HEAD, RWKV_HEAD), F32)],
        scratch_shapes=[pltpu.VMEM((pairs, LANES, LANES), F32),
                        pltpu.VMEM((1, w), F32), pltpu.VMEM((1, w), F32),
                        pltpu.VMEM((1, w), F32), pltpu.VMEM((1, LANES), F32)],
        compiler_params=_cparams(("parallel", "parallel", "arbitrary")),
        name="rwkv7_chunked",
    )(*rkv, lora, *prevs,
      p["tm_r"], p["tm_k"], p["tm_v"], p["tm_lora"], p["w_lora"],
      p["w0"], p["a0"], p["k_k"], p["k_a"], p["r_k"], p["lnx_g"], p["lnx_b"], gates, s0)
    return y, s_out


def _prep_layer_params(l, ln_g, w_in, q_norm_g, k_norm_g, lambda_q1, lambda_k1, lambda_q2, lambda_k2,
                       subln_g, time_mix, w0, w2, a0, a2, k_k, k_a, r_k, lnx_g, lnx_b, w_out):
    w = w_in[l]
    sh = OFF_SHIFT
    o1, o2, o3, o4 = RWKV_W, RWKV_W + LORA, 2 * RWKV_W + LORA, 3 * RWKV_W + LORA
    w_qkvr = w[:, :sh + o1].astype(BF16)
    w_rk = w[:, sh + o2:sh + o3].astype(BF16)
    w_rv = w[:, sh + o3:sh + o4].astype(BF16)
    w_gates = w[:, OFF_GA:].astype(BF16)
    w_lora_in = jnp.concatenate([w[:, sh + o1:sh + o2], w[:, sh + o4:sh + SHIFT_W]], axis=1).astype(BF16)
    tmix = time_mix[l]
    tm_lora = jnp.concatenate([tmix[o1:o2], tmix[o4:]]).reshape(1, 2 * LORA)
    w2p = w2[l].reshape(LORA, N_RWKV_PAIRS, LANES).transpose(1, 0, 2)
    a2p = a2[l].reshape(LORA, N_RWKV_PAIRS, LANES).transpose(1, 0, 2)
    z = jnp.zeros_like(w2p)
    w_lora = jnp.concatenate([jnp.concatenate([w2p, z], axis=2), jnp.concatenate([z, a2p], axis=2)],
                             axis=1).astype(BF16)
    lam_init = 0.8 - 0.6 * math.exp(-0.3 * l)
    lam = (jnp.exp(jnp.sum(lambda_q1[l] * lambda_k1[l])) - jnp.exp(jnp.sum(lambda_q2[l] * lambda_k2[l]))
           + lam_init).reshape(1, 1).astype(F32)
    row = lambda z_: z_.reshape(1, -1)
    return dict(
        layer=l, ln_g=ln_g[l], w_qkvr=w_qkvr, w_rk=w_rk, w_rv=w_rv, w_gates=w_gates, w_lora_in=w_lora_in,
        w_out=w_out[l].astype(BF16),
        q_gain=jnp.tile(q_norm_g[l], 2).reshape(1, DV), k_gain=jnp.tile(k_norm_g[l], 2).reshape(1, DV),
        lam=lam, out_scale=1.0 - lam_init, subln_g=subln_g[l],
        tm_r=row(tmix[:o1]), tm_k=row(tmix[o2:o3]), tm_v=row(tmix[o3:o4]), tm_lora=tm_lora, w_lora=w_lora,
        w0=row(w0[l]), a0=row(a0[l]), k_k=row(k_k[l]), k_a=row(k_a[l]), r_k=row(r_k[l]),
        lnx_g=row(lnx_g[l]), lnx_b=row(lnx_b[l]),
    )


def _layer(x, pos, shift_prev, wkv_prev, past, p, *, chunk, pairs):
    b, t, d = x.shape
    m = b * t
    x2 = x.reshape(m, d)
    h = _rmsnorm(x2, p["ln_g"])

    tm = min(1024, m)
    cos, s1, s2 = _rope_tables(pos)
    if t < tm:
        reps = tm // t
        cos, s1, s2 = (jnp.tile(z, (reps, 1)) for z in (cos, s1, s2))
    gain_spec = pl.BlockSpec((1, LANES), lambda i, j: (0, 0))
    blk = lambda cols: cols // LANES
    tm_hm, tn_hm = min(512, m), 1024

    def qk(col_off, gain, scale, dts):
        tm_, tn_ = (tm_hm, tn_hm) if "f32_head_major" in dts else (tm, 512)
        n_tab = cos.shape[0] // tm_
        tab_spec = pl.BlockSpec((tm_, LANES), lambda i, j: (i % n_tab, 0))
        kern = functools.partial(_mm_qk_kernel, scale=scale, rows_per_chunk=min(256, tm_))
        return _matmul([h], p["w_qkvr"], col_block_off=blk(col_off), n_cols=ATT_W, kernel=kern,
                       out_dtypes=dts, extra=(gain, cos, s1, s2),
                       extra_specs=(gain_spec, tab_spec, tab_spec, tab_spec), scratch=True, tm=tm_, tn=tn_)

    def plain(w, col_off, n_cols):
        return _matmul([h], w, col_block_off=blk(col_off), n_cols=n_cols, kernel=_mm_plain_kernel,
                       out_dtypes=(F32,), tm=tm)[0]

    (q16,) = qk(0, p["q_gain"], DQK ** -0.5, (BF16,))
    k32, k16 = qk(OFF_K, p["k_gain"], 1.0, ("f32_head_major", BF16))
    v32, v16 = _matmul([h], p["w_qkvr"], col_block_off=blk(OFF_V), n_cols=ATT_W,
                       kernel=_mm_dual_kernel, out_dtypes=("f32_head_major", BF16), tm=tm_hm, tn=tn_hm)
    r32 = plain(p["w_qkvr"], OFF_SHIFT, RWKV_W)
    rk32 = plain(p["w_rk"], 0, RWKV_W)
    rv32 = plain(p["w_rv"], 0, RWKV_W)
    lora = plain(p["w_lora_in"], 0, 2 * LORA)
    (gates,) = _matmul([h], p["w_gates"], col_block_off=0, n_cols=ATT_W + RWKV_W,
                       kernel=_mm_silu_kernel, out_dtypes=(BF16,), tm=tm)

    if past is None:
        y_att = _prompt_attention(q16, k16, v16, gates, p["lam"], p["subln_g"],
                                  batch=b, seq=t, out_scale=p["out_scale"])
    else:
        cache_k, cache_v, page_table = past
        y_att = _sample_attention(q16.reshape(b, t, ATT_W), k16.reshape(b, t, ATT_W),
                                  v16.reshape(b, t, ATT_W), gates.reshape(b, t, -1),
                                  cache_k, cache_v, page_table, p["lam"], p["subln_g"],
                                  layer=p["layer"], out_scale=p["out_scale"]).reshape(m, ATT_W)

    o1, o2, o3, o4 = RWKV_W, RWKV_W + LORA, 2 * RWKV_W + LORA, 3 * RWKV_W + LORA
    prev_lora = jnp.concatenate([shift_prev[:, o1:o2], shift_prev[:, o4:]], axis=1)
    prevs = [z.reshape(b, 1, -1) for z in (shift_prev[:, :o1], shift_prev[:, o2:o3], shift_prev[:, o3:o4],
                                           prev_lora)]
    rkv3 = [z.reshape(b, t, RWKV_W) for z in (r32, rk32, rv32)]
    lora3 = lora.reshape(b, t, 2 * LORA)
    y_rwkv, wkv_new = _rwkv(rkv3, lora3, gates.reshape(b, t, -1), prevs, wkv_prev, p,
                            chunk=chunk, pairs=pairs)
    last_lora = lora3[:, -1]
    shift_new = jnp.concatenate([rkv3[0][:, -1], last_lora[:, :LORA], rkv3[1][:, -1], rkv3[2][:, -1],
                                 last_lora[:, LORA:]], axis=1)

    tn_out = 512
    x_spec = pl.BlockSpec((tm, tn_out), lambda i, j: (i, j))
    (out,) = _matmul([y_att, y_rwkv.reshape(m, RWKV_W)], p["w_out"], col_block_off=0, n_cols=d,
                     kernel=_mm_residual_kernel, out_dtypes=(F32,), extra=(x2,), extra_specs=(x_spec,),
                     tm=tm, tn=tn_out)
    return (out.reshape(b, t, d), k32.reshape(b, t, N_ATT_HEADS, DV), v32.reshape(b, t, N_ATT_HEADS, DV),
            shift_new, wkv_new)


def kernel(x_prompt, x_sample, cache_k, cache_v, state_shift, state_wkv, page_table, ln_g, w_in, q_norm_g,
           k_norm_g, lambda_q1, lambda_k1, lambda_q2, lambda_k2, subln_g, time_mix, w0, w2, a0, a2, k_k,
           k_a, r_k, lnx_g, lnx_b, w_out):
    depth = w_in.shape[0]
    b_p, s_p = x_prompt.shape[0], x_prompt.shape[1]
    s_s = x_sample.shape[1]
    past_len = page_table.shape[1] * PAGE_SIZE
    pos_p = jnp.arange(s_p)
    pos_s = past_len + jnp.arange(s_s)
    yp, ys = x_prompt, x_sample
    outs = [[] for _ in range(8)]
    for l in range(depth):
        p = _prep_layer_params(l, ln_g, w_in, q_norm_g, k_norm_g, lambda_q1, lambda_k1, lambda_q2,
                               lambda_k2, subln_g, time_mix, w0, w2, a0, a2, k_k, k_a, r_k, lnx_g, lnx_b,
                               w_out)
        shift0 = jnp.zeros((b_p, SHIFT_W), F32)
        yp, kp, vp, shp, wkvp = _layer(yp, pos_p, shift0, None, None, p, chunk=64, pairs=8)
        past = (cache_k, cache_v, page_table)
        ys, ks, vs, shs, wkvs = _layer(ys, pos_s, state_shift[l], state_wkv[l], past, p, chunk=s_s, pairs=16)
        for lst, val in zip(outs, (kp, vp, shp, wkvp, ks, vs, shs, wkvs)):
            lst.append(val)
    stacked = [jnp.stack(lst, 0) for lst in outs]
    return (yp, ys, *stacked)
```

```python
import functools
import math

import jax
import jax.numpy as jnp
from jax import lax
from jax.experimental import pallas as pl
from jax.experimental.pallas import tpu as pltpu

F32 = jnp.float32
BF16 = jnp.bfloat16

D_MODEL = 4096
PAGE_SIZE = 128
ATT_W = 2048
RWKV_W = 2048
DV = 128
N_ATT_HEADS = ATT_W // DV
DQK = DV // 2
ROT_DIM = DQK // 4
ROPE_THETA = 500000.0
RWKV_HEAD = 64
N_RWKV_HEADS = RWKV_W // RWKV_HEAD
N_RWKV_PAIRS = N_RWKV_HEADS // 2
LORA = 64
SHIFT_W = 3 * RWKV_W + 2 * LORA
OFF_K = ATT_W
OFF_V = 2 * ATT_W
OFF_SHIFT = 3 * ATT_W
OFF_GA = OFF_SHIFT + SHIFT_W
OFF_GR = OFF_GA + ATT_W
IN_W = OFF_GR + RWKV_W
NORM_EPS = 1e-6
SUBLN_EPS = 1e-5
LNX_EPS = 64e-5

LANES = 128
VMEM_LIMIT_BYTES = 48 * 1024 * 1024
FUSED_VMEM_LIMIT_BYTES = 56 * 1024 * 1024
NEG_BIG = -1e30


def _cparams(sem, vmem_limit_bytes=VMEM_LIMIT_BYTES):
    return pltpu.CompilerParams(dimension_semantics=sem, vmem_limit_bytes=vmem_limit_bytes)


def _run(staged_body):
    for _ in staged_body:
        pass


def _interleave(*staged_bodies):
    active, finals = list(staged_bodies), []
    while active:
        for g in list(active):
            tag = next(g, "done")
            if tag == "done":
                active.remove(g)
            elif tag == "final":
                active.remove(g)
                finals.append(g)
    for g in finals:
        _run(g)


def _split_dot(x, w2_bf16):
    hi = x.astype(BF16)
    lo = (x - hi.astype(F32)).astype(BF16)
    return jnp.dot(jnp.concatenate([hi, lo], axis=1), w2_bf16, preferred_element_type=F32)


def _half_ones():
    r = (lax.broadcasted_iota(jnp.int32, (2 * LANES, LANES), 0) % LANES) // RWKV_HEAD
    c = lax.broadcasted_iota(jnp.int32, (2 * LANES, LANES), 1) // RWKV_HEAD
    return jnp.where(r == c, 1.0, 0.0).astype(BF16)


def _rmsnorm_kernel(x_ref, g_ref, o_ref):
    x = x_ref[...]
    ms = jnp.mean(x * x, axis=-1, keepdims=True)
    o_ref[...] = (x * lax.rsqrt(ms + NORM_EPS) * g_ref[...]).astype(BF16)


def _rmsnorm(x2, g, tm=256):
    m, d = x2.shape
    tm = min(tm, m)
    return pl.pallas_call(
        _rmsnorm_kernel,
        grid=(m // tm,),
        in_specs=[pl.BlockSpec((tm, d), lambda i: (i, 0)),
                  pl.BlockSpec((1, d), lambda i: (0, 0))],
        out_specs=pl.BlockSpec((tm, d), lambda i: (i, 0)),
        out_shape=jax.ShapeDtypeStruct((m, d), BF16),
        compiler_params=_cparams(("parallel",)),
        name="rmsnorm",
    )(x2, g.reshape(1, d))


def _mm_full_k(lhs_refs, w_ref):
    acc, off = None, 0
    for l_ref in lhs_refs:
        k = l_ref.shape[1]
        part = jnp.dot(l_ref[...], w_ref[off:off + k, :], preferred_element_type=F32)
        acc = part if acc is None else acc + part
        off += k
    return acc


def _mm_plain_kernel(h_ref, w_ref, o_ref):
    o_ref[...] = _mm_full_k([h_ref], w_ref).astype(o_ref.dtype)


def _mm_dual_kernel(h_ref, w_ref, o32_ref, o16_ref):
    a = _mm_full_k([h_ref], w_ref)
    o16_ref[...] = a.astype(BF16)
    for hh in range(a.shape[1] // LANES):
        o32_ref[:, hh, :] = a[:, hh * LANES:(hh + 1) * LANES]


def _mm_silu_kernel(h_ref, w_ref, o_ref):
    a = _mm_full_k([h_ref], w_ref)
    o_ref[...] = (a * jax.nn.sigmoid(a)).astype(o_ref.dtype)


def _mm_residual_kernel(ya_ref, yr_ref, w_ref, x_ref, o_ref):
    o_ref[...] = x_ref[...] + _mm_full_k([ya_ref, yr_ref], w_ref)


def _mm_qk_kernel(h_ref, w_ref, g_ref, cos_ref, s1_ref, s2_ref, *rest, scale, rows_per_chunk):
    outs, acc_ref = rest[:-1], rest[-1]
    acc_ref[...] = _mm_full_k([h_ref], w_ref)
    tm, tn = acc_ref.shape
    ones = _half_ones()
    gain = g_ref[...]

    def chunk(r, carry):
        rows = pl.ds(pl.multiple_of(r * rows_per_chunk, rows_per_chunk), rows_per_chunk)
        c = cos_ref[rows, :]
        s1 = s1_ref[rows, :]
        s2 = s2_ref[rows, :]
        for hh in range(tn // LANES):
            cols = slice(hh * LANES, (hh + 1) * LANES)
            x = acc_ref[rows, cols]
            ss = _split_dot(x * x, ones)
            y = x * lax.rsqrt(ss * (1.0 / DQK) + NORM_EPS) * gain
            y = y * c + pltpu.roll(y, LANES - ROT_DIM // 2, 1) * s1 + pltpu.roll(y, ROT_DIM // 2, 1) * s2
            if scale != 1.0:
                y = y * scale
            for o_ref in outs:
                if len(o_ref.shape) == 3:
                    o_ref[rows, hh, :] = y.astype(o_ref.dtype)
                else:
                    o_ref[rows, cols] = y.astype(o_ref.dtype)
        return carry

    lax.fori_loop(0, tm // rows_per_chunk, chunk, 0)


def _matmul(lhs, w, *, col_block_off, n_cols, kernel, out_dtypes, extra=(), extra_specs=(),
            scratch=False, tm=1024, tn=512):
    m = lhs[0].shape[0]
    tm = min(tm, m)
    tn = min(tn, n_cols)
    assert m % tm == 0 and n_cols % tn == 0 and sum(l.shape[1] for l in lhs) == w.shape[0]
    assert (col_block_off * LANES) % tn == 0
    joff = col_block_off * LANES // tn
    in_specs = [pl.BlockSpec((tm, l.shape[1]), lambda i, j: (i, 0)) for l in lhs]
    in_specs.append(pl.BlockSpec((w.shape[0], tn), lambda i, j: (0, j + joff)))
    in_specs += list(extra_specs)
    out_specs, out_shape = [], []
    for dt in out_dtypes:
        if dt == "f32_head_major":
            out_specs.append(pl.BlockSpec((tm, tn // LANES, LANES), lambda i, j: (i, j, 0)))
            out_shape.append(jax.ShapeDtypeStruct((m, n_cols // LANES, LANES), F32))
        else:
            out_specs.append(pl.BlockSpec((tm, tn), lambda i, j: (i, j)))
            out_shape.append(jax.ShapeDtypeStruct((m, n_cols), dt))
    return pl.pallas_call(
        kernel,
        grid=(m // tm, n_cols // tn),
        in_specs=in_specs,
        out_specs=out_specs,
        out_shape=out_shape,
        scratch_shapes=[pltpu.VMEM((tm, tn), F32)] if scratch else [],
        compiler_params=_cparams(("parallel", "arbitrary")),
        name="proj_matmul",
    )(*lhs, w, *extra)


def _rope_tables(pos):
    half = ROT_DIM // 2
    inv_freq = 1.0 / (ROPE_THETA ** (jnp.arange(0, ROT_DIM, 2, dtype=F32) / ROT_DIM))
    ang = pos.astype(F32)[:, None] * inv_freq[None, :]
    cos = jnp.cos(ang)
    sin = jnp.sin(ang)
    t = pos.shape[0]
    one = jnp.ones((t, DQK - ROT_DIM), F32)
    zero = jnp.zeros((t, DQK - ROT_DIM), F32)
    zh = jnp.zeros((t, half), F32)
    c64 = jnp.concatenate([cos, cos, one], axis=1)
    s1_64 = jnp.concatenate([-sin, zh, zero], axis=1)
    s2_64 = jnp.concatenate([zh, sin, zero], axis=1)
    tile2 = lambda z: jnp.concatenate([z, z], axis=1)
    return tile2(c64), tile2(s1_64), tile2(s2_64)


def _stack_maps(q):
    lane = lax.broadcasted_iota(jnp.int32, q.shape, 1)
    z = jnp.zeros_like(q)
    return jnp.concatenate([jnp.where(lane < DQK, q, z), jnp.where(lane >= DQK, q, z)], axis=0)


def _online_softmax_steps(ss, vs, ms, ls, accs):
    def bcast(m, n):
        return jnp.tile(m, (1, n // LANES)) if n >= LANES else m[:, :n]

    m_news = [jnp.maximum(m, jnp.max(s, axis=-1, keepdims=True)) for s, m in zip(ss, ms)]
    alphas = [jnp.exp(m - mn) for m, mn in zip(ms, m_news)]
    ps = [jnp.exp(s - bcast(mn, s.shape[1])) for s, mn in zip(ss, m_news)]
    l_news = [a * l + jnp.sum(p, axis=-1, keepdims=True) for a, l, p in zip(alphas, ls, ps)]
    pvs = [jnp.dot(p.astype(BF16), v, preferred_element_type=F32) for p, v in zip(ps, vs)]
    acc_news = [a * acc + pv for a, acc, pv in zip(alphas, accs, pvs)]
    return m_news, l_news, acc_news


def _attn_finish(acc, l, lam, subln_g, gate, out_scale):
    t = acc.shape[0] // 2
    o = acc[:t] / l[:t] - lam * (acc[t:] / l[t:])
    ms = jnp.mean(o * o, axis=-1, keepdims=True)
    o = o * lax.rsqrt(ms + SUBLN_EPS) * subln_g * out_scale
    return (o * gate.astype(F32)).astype(BF16)


def _prompt_attn_kernel(lam_ref, q_ref, k_ref, v_ref, g_ref, sg_ref, o_ref,
                        q12_ref, m_ref, l_ref, acc_ref, *, tq, heads, out_scale):
    qi = pl.program_id(2)
    for hp in range(heads):
        q12_ref[hp] = _stack_maps(q_ref[:, hp * DV:(hp + 1) * DV])
    m_ref[...] = jnp.full_like(m_ref, NEG_BIG)
    l_ref[...] = jnp.zeros_like(l_ref)
    acc_ref[...] = jnp.zeros_like(acc_ref)

    def tile(kj, masked):
        rows = pl.ds(pl.multiple_of(kj * tq, tq), tq)
        hcols = [slice(hp * DV, (hp + 1) * DV) for hp in range(heads)]
        ss = [lax.dot_general(q12_ref[hp], k_ref[rows, hcols[hp]], (((1,), (1,)), ((), ())),
                              preferred_element_type=F32) for hp in range(heads)]
        if masked:
            r = lax.broadcasted_iota(jnp.int32, ss[0].shape, 0) % tq
            c = lax.broadcasted_iota(jnp.int32, ss[0].shape, 1)
            ss = [jnp.where(c <= r, s, NEG_BIG) for s in ss]
        ms, ls, accs = _online_softmax_steps(
            ss, [v_ref[rows, hcols[hp]] for hp in range(heads)],
            [m_ref[hp] for hp in range(heads)], [l_ref[hp] for hp in range(heads)],
            [acc_ref[hp] for hp in range(heads)])
        for hp in range(heads):
            m_ref[hp] = ms[hp]
            l_ref[hp] = ls[hp]
            acc_ref[hp] = accs[hp]

    def body(kj, carry):
        tile(kj, False)
        return carry

    lax.fori_loop(0, qi, body, 0)
    tile(qi, True)
    lam = lam_ref[0, 0]
    for hp in range(heads):
        cols = slice(hp * DV, (hp + 1) * DV)
        o_ref[:, cols] = _attn_finish(acc_ref[hp], l_ref[hp], lam, sg_ref[...], g_ref[:, cols], out_scale)


def _prompt_attention(q, k16, v16, gates, lam, subln_g, *, batch, seq, out_scale, tq=512, heads=2):
    tq = min(tq, seq)
    nq = seq // tq
    w = heads * DV
    kern = functools.partial(_prompt_attn_kernel, tq=tq, heads=heads, out_scale=out_scale)
    return pl.pallas_call(
        kern,
        grid=(batch, N_ATT_HEADS // heads, nq),
        in_specs=[
            pl.BlockSpec(memory_space=pltpu.SMEM),
            pl.BlockSpec((tq, w), lambda b, h, i: (b * nq + i, h)),
            pl.BlockSpec((seq, w), lambda b, h, i: (b, h)),
            pl.BlockSpec((seq, w), lambda b, h, i: (b, h)),
            pl.BlockSpec((tq, w), lambda b, h, i: (b * nq + i, h)),
            pl.BlockSpec((1, DV), lambda b, h, i: (0, 0)),
        ],
        out_specs=pl.BlockSpec((tq, w), lambda b, h, i: (b * nq + i, h)),
        out_shape=jax.ShapeDtypeStruct((batch * seq, ATT_W), BF16),
        scratch_shapes=[pltpu.VMEM((heads, 2 * tq, DV), BF16),
                        pltpu.VMEM((heads, 2 * tq, LANES), F32),
                        pltpu.VMEM((heads, 2 * tq, LANES), F32),
                        pltpu.VMEM((heads, 2 * tq, DV), F32)],
        compiler_params=_cparams(("parallel", "parallel", "arbitrary")),
        name="prompt_attention",
    )(lam, q, k16, v16, gates, subln_g.reshape(1, DV))


QUADS = 4
HEADS_PER_QUAD = N_ATT_HEADS // QUADS


def _sample_attn_body(j, n_steps, lam_ref, q_ref, refs, *, t_new, pages_per_step, out_scale):
    kc_refs = refs[:pages_per_step]
    vc_refs = refs[pages_per_step:2 * pages_per_step]
    kn_ref, vn_ref, g_ref, sg_ref, o_ref, q12_ref, m_ref, l_ref, acc_ref = refs[2 * pages_per_step:]
    rows = 2 * t_new
    qrows = HEADS_PER_QUAD * rows
    ncols = PAGE_SIZE * HEADS_PER_QUAD

    @pl.when(j == 0)
    def _():
        for h in range(N_ATT_HEADS):
            c, i = h % QUADS, h // QUADS
            q12_ref[c, i * rows:(i + 1) * rows, :] = _stack_maps(q_ref[0, :, h * DV:(h + 1) * DV])
        m_ref[...] = jnp.full_like(m_ref, NEG_BIG)
        l_ref[...] = jnp.zeros_like(l_ref)
        acc_ref[...] = jnp.zeros_like(acc_ref)

    yield "init"
    own = (lax.broadcasted_iota(jnp.int32, (qrows, ncols), 0) // rows
           == lax.broadcasted_iota(jnp.int32, (qrows, ncols), 1) % HEADS_PER_QUAD)
    nt = (((1,), (1,)), ((), ()))
    quads = range(QUADS)
    ms = [m_ref[c] for c in quads]
    ls = [l_ref[c] for c in quads]
    accs = [acc_ref[c] for c in quads]
    for u in range(pages_per_step):
        scores = [jnp.where(own, lax.dot_general(q12_ref[c], kc_refs[u][pl.ds(c, ncols, stride=QUADS), :].astype(BF16),
                                                 nt, preferred_element_type=F32), NEG_BIG) for c in quads]
        yield "scores"
        vals = [vc_refs[u][pl.ds(c, ncols, stride=QUADS), :].astype(BF16) for c in quads]
        ms, ls, accs = _online_softmax_steps(scores, vals, ms, ls, accs)
        yield "softmax"
    for c in quads:
        m_ref[c] = ms[c]
        l_ref[c] = ls[c]
        acc_ref[c] = accs[c]

    yield "final"

    @pl.when(j == n_steps - 1)
    def _():
        r = lax.broadcasted_iota(jnp.int32, (rows, t_new), 0) % t_new
        cc = lax.broadcasted_iota(jnp.int32, (rows, t_new), 1)
        causal = cc <= r
        lam = lam_ref[0, 0]
        for h in range(N_ATT_HEADS):
            c, i = h % QUADS, h // QUADS
            sl = slice(i * rows, (i + 1) * rows)
            cols = slice(h * DV, (h + 1) * DV)
            s = lax.dot_general(q12_ref[c, sl, :], kn_ref[0, :, cols], (((1,), (1,)), ((), ())),
                                preferred_element_type=F32)
            s = jnp.where(causal, s, NEG_BIG)
            _, (l,), (acc,) = _online_softmax_steps([s], [vn_ref[0, :, cols]], [m_ref[c, sl, :]],
                                                    [l_ref[c, sl, :]], [acc_ref[c, sl, :]])
            o_ref[0, :, cols] = _attn_finish(acc, l, lam, sg_ref[...], g_ref[0, :, cols], out_scale)


def _sample_attn_kernel(pt_ref, lam_ref, q_ref, *refs, **kw):
    _run(_sample_attn_body(pl.program_id(1), pl.num_programs(1), lam_ref, q_ref, refs, **kw))


def _sample_attn_call(q, k16, v16, gates, cache_k, cache_v, page_table, lam, subln_g, *, layer, out_scale,
                      pages_per_step=8):
    b, t_new, _ = q.shape
    n_pages = page_table.shape[1]
    n_pool = cache_k.shape[1]
    pps = math.gcd(pages_per_step, n_pages)
    page_rows = PAGE_SIZE * N_ATT_HEADS
    kc2 = cache_k.reshape(-1, DV)
    vc2 = cache_v.reshape(-1, DV)
    tok_spec = pl.BlockSpec((1, t_new, ATT_W), lambda i, j, pt: (i, 0, 0))

    def page_spec(u):
        return pl.BlockSpec((page_rows, DV), lambda i, j, pt: (layer * n_pool + pt[i, j * pps + u], 0))

    page_specs = [page_spec(u) for u in range(pps)]
    qrows = HEADS_PER_QUAD * 2 * t_new
    return dict(
        grid=(b, n_pages // pps),
        kw=dict(t_new=t_new, pages_per_step=pps, out_scale=out_scale),
        in_specs=[pl.BlockSpec(memory_space=pltpu.SMEM), tok_spec] + page_specs + page_specs
                 + [tok_spec, tok_spec, tok_spec, pl.BlockSpec((1, DV), lambda i, j, pt: (0, 0))],
        args=[lam, q] + [kc2] * pps + [vc2] * pps + [k16, v16, gates, subln_g.reshape(1, DV)],
        out_specs=[tok_spec],
        out_shape=[jax.ShapeDtypeStruct((b, t_new, ATT_W), BF16)],
        scratch=[pltpu.VMEM((QUADS, qrows, DV), BF16),
                 pltpu.VMEM((QUADS, qrows, LANES), F32),
                 pltpu.VMEM((QUADS, qrows, LANES), F32),
                 pltpu.VMEM((QUADS, qrows, DV), F32)],
    )


def _sample_attention(page_table, call):
    grid_spec = pltpu.PrefetchScalarGridSpec(
        num_scalar_prefetch=1, grid=call["grid"], in_specs=call["in_specs"], out_specs=call["out_specs"],
        scratch_shapes=call["scratch"])
    return pl.pallas_call(
        functools.partial(_sample_attn_kernel, **call["kw"]),
        grid_spec=grid_spec,
        out_shape=call["out_shape"],
        compiler_params=_cparams(("parallel", "arbitrary")),
        name="sample_attention",
    )(page_table, *call["args"])[0]


def _stack_heads(x):
    lane = lax.broadcasted_iota(jnp.int32, x.shape, 1)
    z = jnp.zeros_like(x)
    return jnp.concatenate([jnp.where(lane < RWKV_HEAD, x, z), jnp.where(lane >= RWKV_HEAD, x, z)], axis=0)


def _each(f, *lists):
    return [f(*args) for args in zip(*lists)]


def _mm(a, b):
    return jnp.dot(a, b, preferred_element_type=F32)


def _rwkv_chunk(rs, krs, vs, lo, wls, w0s, a0s, k_ks, k_as, r_ks, lnx_gs, lnx_bs, gates, h0s, consts):
    ones, tril, strict, incl, eye = consts
    C = lo.shape[0]
    n2 = 2 * C
    lane = lax.broadcasted_iota(jnp.int32, (C, LANES), 1)
    lo_act = jnp.where(lane < LORA, jnp.tanh(lo), lo).astype(BF16)
    pres = [_mm(lo_act, wl) for wl in wls]
    logws = _each(lambda pre, w0: -math.exp(-0.5) * jax.nn.sigmoid(w0 + pre[:, :LANES]), pres, w0s)
    a_sigs = _each(lambda pre, a0: jax.nn.sigmoid(a0 + pre[:, LANES:]), pres, a0s)
    yield "stage"

    kks = _each(lambda kr, k_k: kr * k_k, krs, k_ks)
    kk_ss = [_split_dot(kk * kk, ones) for kk in kks]
    kks = _each(lambda kk, ss: kk / jnp.maximum(jnp.sqrt(ss), 1e-12), kks, kk_ss)
    k_hs = _each(lambda kr, a_sig, k_a: kr * (1.0 + (a_sig - 1.0) * k_a), krs, a_sigs, k_as)
    rk_ss = [_split_dot(r * k_h * r_k, ones) for r, k_h, r_k in zip(rs, k_hs, r_ks)]
    bonuses = _each(lambda ss, v: ss * v, rk_ss, vs)
    yield "stage"

    def split3(x):
        l1 = x.astype(BF16)
        rem = x - l1.astype(F32)
        l2 = rem.astype(BF16)
        return jnp.concatenate([l1, l2, (rem - l2.astype(F32)).astype(BF16)], axis=0)

    cums = [_mm(tril, split3(logw)) for logw in logws]
    yield "stage"
    g_incls = [jnp.exp(cum) for cum in cums]
    g_excls = _each(lambda cum, logw: jnp.exp(cum - logw), cums, logws)
    g_invs = [jnp.exp(-cum) for cum in cums]
    g_lasts = [g[C - 1:C, :] for g in g_incls]

    a_ts = _each(lambda kk, g: _stack_heads(-kk * g).astype(BF16), kks, g_excls)
    r_ts = _each(lambda r, g: _stack_heads(r * g).astype(BF16), rs, g_incls)
    v_sts = [_stack_heads(v).astype(BF16) for v in vs]
    bks = _each(lambda kk, a_sig, k_h, g: jnp.concatenate([_stack_heads(kk * a_sig * g), _stack_heads(k_h * g)],
                                                          axis=0), kks, a_sigs, k_hs, g_invs)
    gmats = _each(lambda a_t, r_t, bk: lax.dot_general(jnp.concatenate([a_t, r_t], axis=0), bk.astype(BF16),
                                                       (((1,), (1,)), ((), ())), preferred_element_type=F32),
                  a_ts, r_ts, bks)
    yield "stage"
    a_abs = [jnp.where(strict, g[:n2, :n2], 0.0) for g in gmats]
    a_aks = [jnp.where(strict, g[:n2, n2:], 0.0).astype(BF16) for g in gmats]
    a_rbs = [jnp.where(incl, g[n2:, :n2], 0.0).astype(BF16) for g in gmats]
    a_rks = [jnp.where(incl, g[n2:, n2:], 0.0).astype(BF16) for g in gmats]
    yield "stage"

    eye2 = jnp.where(lax.broadcasted_iota(jnp.int32, (n2, n2), 0)
                     == lax.broadcasted_iota(jnp.int32, (n2, n2), 1), 1.0, 0.0)
    t_invs = [eye2 + a for a in a_abs]
    pws = [a.astype(BF16) for a in a_abs]
    for _ in range(int(math.log2(C)) - 1):
        pws = [_mm(pw, pw).astype(BF16) for pw in pws]
        t_invs = _each(lambda t, pw: t + _mm(t.astype(BF16), pw), t_invs, pws)
        yield "stage"

    h16s = [h0.astype(BF16) for h0 in h0s]
    if n2 % LANES == 0:
        cat = jnp.concatenate
        xs = _each(lambda a_t, h16, a_ak, v_st: _mm(cat([a_t, a_ak], axis=1), cat([h16, v_st], axis=0)),
                   a_ts, h16s, a_aks, v_sts)
    else:
        xs = _each(lambda a_t, h16, a_ak, v_st: _mm(a_t, h16) + _mm(a_ak, v_st), a_ts, h16s, a_aks, v_sts)
    u16s = _each(lambda t, x: _mm(t.astype(BF16), x.astype(BF16)).astype(BF16), t_invs, xs)
    yield "stage"
    if n2 % LANES == 0:
        o_sts = _each(lambda r_t, h16, a_rb, u16, a_rk, v_st:
                      _mm(cat([r_t, a_rb, a_rk], axis=1), cat([h16, u16, v_st], axis=0)),
                      r_ts, h16s, a_rbs, u16s, a_rks, v_sts)
    else:
        o_sts = _each(lambda r_t, h16, a_rb, u16, a_rk, v_st: _mm(r_t, h16) + _mm(a_rb, u16) + _mm(a_rk, v_st),
                      r_ts, h16s, a_rbs, u16s, a_rks, v_sts)
    os_ = [o_st[:C] + o_st[C:] for o_st in o_sts]
    yield "stage"

    bk_ts = _each(lambda bk, g_last: (bk * g_last).T.astype(BF16), bks, g_lasts)
    g_cols = [jnp.sum(jnp.where(eye, jnp.broadcast_to(g_last, (LANES, LANES)), 0.0), axis=1, keepdims=True)
              for g_last in g_lasts]
    h_news = _each(lambda g_col, h0, bk_t, u16, v_st: g_col * h0 + _mm(bk_t, jnp.concatenate([u16, v_st], axis=0)),
                   g_cols, h0s, bk_ts, u16s, v_sts)
    yield "stage"

    mus = [_split_dot(o, ones) * (1.0 / RWKV_HEAD) for o in os_]
    ds = _each(lambda o, mu: o - mu, os_, mus)
    vars_ = [_split_dot(d * d, ones) * (1.0 / RWKV_HEAD) for d in ds]
    ys = _each(lambda d, var, g, b, bonus, gate:
               (((d * lax.rsqrt(var + LNX_EPS) * g + b) + bonus) * gate.astype(F32)).astype(BF16),
               ds, vars_, lnx_gs, lnx_bs, bonuses, gates)
    return ys, h_news


def _rwkv_body(c, n_chunks, refs, *, chunk, pairs, use_s0):
    (r_ref, k_ref, v_ref, lo_ref, pr_ref, pk_ref, pv_ref, plo_ref,
     tmr_ref, tmk_ref, tmv_ref, tmlo_ref, wl_ref, w0_ref, a0_ref, kk_ref, ka_ref, rk_ref,
     lg_ref, lb_ref, gate_ref, s0_ref, y_ref, sout_ref,
     h_ref, cr_ref, ck_ref, cv_ref, clo_ref) = refs
    C = chunk
    n2 = 2 * C

    @pl.when(c == 0)
    def _():
        cr_ref[...] = pr_ref[0]
        ck_ref[...] = pk_ref[0]
        cv_ref[...] = pv_ref[0]
        clo_ref[...] = plo_ref[0]
        if use_s0:
            z = jnp.zeros((RWKV_HEAD, RWKV_HEAD), F32)
            for u in range(pairs):
                st = jnp.concatenate([jnp.concatenate([s0_ref[0, 2 * u], z], axis=1),
                                      jnp.concatenate([z, s0_ref[0, 2 * u + 1]], axis=1)], axis=0)
                h_ref[u] = st.T
        else:
            h_ref[...] = jnp.zeros_like(h_ref)

    yield "init"
    ri = lax.broadcasted_iota(jnp.int32, (n2, n2), 0)
    ci = lax.broadcasted_iota(jnp.int32, (n2, n2), 1)
    tril = (lax.broadcasted_iota(jnp.int32, (C, 3 * C), 1) % C <= lax.broadcasted_iota(jnp.int32, (C, 3 * C), 0))
    eye = (lax.broadcasted_iota(jnp.int32, (LANES, LANES), 0)
           == lax.broadcasted_iota(jnp.int32, (LANES, LANES), 1))
    consts = (_half_ones(), jnp.where(tril, 1.0, 0.0).astype(BF16), ri > ci, ri >= ci, eye)

    def token_shift(cur_ref3, carry_ref, tm_ref):
        cur = cur_ref3[0]
        row0 = lax.broadcasted_iota(jnp.int32, cur.shape, 0) == 0
        prev = jnp.where(row0, carry_ref[...], pltpu.roll(cur, 1, 0))
        carry_ref[...] = cur[C - 1:C, :]
        return cur + tm_ref[...] * (prev - cur)

    lo = token_shift(lo_ref, clo_ref, tmlo_ref)
    r_all = token_shift(r_ref, cr_ref, tmr_ref)
    k_all = token_shift(k_ref, ck_ref, tmk_ref)
    v_all = token_shift(v_ref, cv_ref, tmv_ref)
    col = lambda u: slice(u * LANES, (u + 1) * LANES)
    per_pair = lambda x: [x[:, col(u)] for u in range(pairs)]
    ys, h_news = yield from _rwkv_chunk(
        per_pair(r_all), per_pair(k_all), per_pair(v_all), lo, [wl_ref[u] for u in range(pairs)],
        per_pair(w0_ref[...]), per_pair(a0_ref[...]), per_pair(kk_ref[...]), per_pair(ka_ref[...]),
        per_pair(rk_ref[...]), per_pair(lg_ref[...]), per_pair(lb_ref[...]), per_pair(gate_ref[0]),
        [h_ref[u] for u in range(pairs)], consts)
    for u in range(pairs):
        y_ref[0, :, col(u)] = ys[u]
        h_ref[u] = h_news[u]

    yield "final"

    @pl.when(c == n_chunks - 1)
    def _():
        for u in range(pairs):
            st = h_news[u].T
            sout_ref[0, 2 * u] = st[:RWKV_HEAD, :RWKV_HEAD]
            sout_ref[0, 2 * u + 1] = st[RWKV_HEAD:, RWKV_HEAD:]


def _rwkv_kernel(*refs, **kw):
    _run(_rwkv_body(pl.program_id(2), pl.num_programs(2), refs, **kw))


def _rwkv_call(rkv, lora, gates, prevs, s0, p, *, chunk, pairs):
    b, t, _ = lora.shape
    n_chunks = t // chunk
    ng = N_RWKV_PAIRS // pairs
    w = pairs * LANES
    use_s0 = s0 is not None
    sblk = (1, 2 * pairs, RWKV_HEAD, RWKV_HEAD)
    if s0 is None:
        s0 = jnp.zeros(sblk, F32)
        s0_spec = pl.BlockSpec(sblk, lambda i, j, c: (0, 0, 0, 0))
    else:
        s0_spec = pl.BlockSpec(sblk, lambda i, j, c: (i, j, 0, 0))

    tok = pl.BlockSpec((1, chunk, w), lambda i, j, c: (i, c, j))
    prev = pl.BlockSpec((1, 1, w), lambda i, j, c: (i, 0, j))

    def par():
        return pl.BlockSpec((1, w), lambda i, j, c: (0, j))

    return dict(
        grid=(b, ng, n_chunks),
        kw=dict(chunk=chunk, pairs=pairs, use_s0=use_s0),
        in_specs=[
            tok, tok, tok,
            pl.BlockSpec((1, chunk, LANES), lambda i, j, c: (i, c, 0)),
            prev, prev, prev,
            pl.BlockSpec((1, 1, LANES), lambda i, j, c: (i, 0, 0)),
            par(), par(), par(), pl.BlockSpec((1, LANES), lambda i, j, c: (0, 0)),
            pl.BlockSpec((pairs, LANES, 2 * LANES), lambda i, j, c: (j, 0, 0)),
            par(), par(), par(), par(), par(), par(), par(),
            pl.BlockSpec((1, chunk, w), lambda i, j, c: (i, c, ng + j)),
            s0_spec,
        ],
        args=[*rkv, lora, *prevs, p["tm_r"], p["tm_k"], p["tm_v"], p["tm_lora"], p["w_lora"],
              p["w0"], p["a0"], p["k_k"], p["k_a"], p["r_k"], p["lnx_g"], p["lnx_b"], gates, s0],
        out_specs=[pl.BlockSpec((1, chunk, w), lambda i, j, c: (i, c, j)),
                   pl.BlockSpec(sblk, lambda i, j, c: (i, j, 0, 0))],
        out_shape=[jax.ShapeDtypeStruct((b, t, RWKV_W), BF16),
                   jax.ShapeDtypeStruct((b, N_RWKV_HEADS, RWKV_HEAD, RWKV_HEAD), F32)],
        scratch=[pltpu.VMEM((pairs, LANES, LANES), F32),
                 pltpu.VMEM((1, w), F32), pltpu.VMEM((1, w), F32),
                 pltpu.VMEM((1, w), F32), pltpu.VMEM((1, LANES), F32)],
    )


def _rwkv(call):
    return pl.pallas_call(
        functools.partial(_rwkv_kernel, **call["kw"]),
        grid=call["grid"],
        in_specs=call["in_specs"],
        out_specs=call["out_specs"],
        out_shape=call["out_shape"],
        scratch_shapes=call["scratch"],
        compiler_params=_cparams(("parallel", "parallel", "arbitrary")),
        name="rwkv7_chunked",
    )(*call["args"])


def _remap(spec, decode):
    if spec.block_shape is None:
        return spec
    return pl.BlockSpec(spec.block_shape, lambda *a: spec.index_map(*decode(*a)))


def _fused_kernel(pt_ref, *refs, n_in, n_out, n_scr, rwkv_kw, attn_kw, n_chunks, attn_steps):
    (ri, ai), (ro, ao), (rs, as_) = n_in, n_out, n_scr
    r_in, a_in = refs[:ri], refs[ri:ri + ai]
    r_out, a_out = refs[ri + ai:ri + ai + ro], refs[ri + ai + ro:ri + ai + ro + ao]
    r_scr, a_scr = refs[ri + ai + ro + ao:ri + ai + ro + ao + rs], refs[ri + ai + ro + ao + rs:]
    s = pl.program_id(0)
    _interleave(
        _rwkv_body(s % n_chunks, n_chunks, (*r_in, *r_out, *r_scr), **rwkv_kw),
        _sample_attn_body(s % attn_steps, attn_steps, a_in[0], a_in[1], (*a_in[2:], *a_out, *a_scr), **attn_kw))


def _rwkv_with_sample_attention(rwkv_call, attn_call, page_table):
    b_r, ng, nc = rwkv_call["grid"]
    b_a, nj = attn_call["grid"]
    n_steps = b_r * ng * nc
    assert n_steps == b_a * nj
    dec_r = lambda s, pt: (s // (ng * nc), (s // nc) % ng, s % nc)
    dec_a = lambda s, pt: (s // nj, s % nj, pt)
    in_specs = [_remap(sp, dec_r) for sp in rwkv_call["in_specs"]] + [_remap(sp, dec_a) for sp in attn_call["in_specs"]]
    out_specs = [_remap(sp, dec_r) for sp in rwkv_call["out_specs"]] + [_remap(sp, dec_a) for sp in attn_call["out_specs"]]
    kern = functools.partial(
        _fused_kernel,
        n_in=(len(rwkv_call["in_specs"]), len(attn_call["in_specs"])),
        n_out=(len(rwkv_call["out_specs"]), len(attn_call["out_specs"])),
        n_scr=(len(rwkv_call["scratch"]), len(attn_call["scratch"])),
        rwkv_kw=rwkv_call["kw"], attn_kw=attn_call["kw"], n_chunks=nc, attn_steps=nj)
    grid_spec = pltpu.PrefetchScalarGridSpec(
        num_scalar_prefetch=1, grid=(n_steps,), in_specs=in_specs, out_specs=out_specs,
        scratch_shapes=rwkv_call["scratch"] + attn_call["scratch"])
    y_rwkv, s_out, y_att = pl.pallas_call(
        kern,
        grid_spec=grid_spec,
        out_shape=rwkv_call["out_shape"] + attn_call["out_shape"],
        compiler_params=_cparams(("arbitrary",), FUSED_VMEM_LIMIT_BYTES),
        name="rwkv7_and_paged_attention",
    )(page_table, *rwkv_call["args"], *attn_call["args"])
    return y_rwkv, s_out, y_att


def _prep_layer_params(l, ln_g, w_in, q_norm_g, k_norm_g, lambda_q1, lambda_k1, lambda_q2, lambda_k2,
                       subln_g, time_mix, w0, w2, a0, a2, k_k, k_a, r_k, lnx_g, lnx_b, w_out):
    w = w_in[l]
    sh = OFF_SHIFT
    o1, o2, o3, o4 = RWKV_W, RWKV_W + LORA, 2 * RWKV_W + LORA, 3 * RWKV_W + LORA
    w_qkvr = w[:, :sh + o1].astype(BF16)
    w_rk = w[:, sh + o2:sh + o3].astype(BF16)
    w_rv = w[:, sh + o3:sh + o4].astype(BF16)
    w_gates = w[:, OFF_GA:].astype(BF16)
    w_lora_in = jnp.concatenate([w[:, sh + o1:sh + o2], w[:, sh + o4:sh + SHIFT_W]], axis=1).astype(BF16)
    tmix = time_mix[l]
    tm_lora = jnp.concatenate([tmix[o1:o2], tmix[o4:]]).reshape(1, 2 * LORA)
    w2p = w2[l].reshape(LORA, N_RWKV_PAIRS, LANES).transpose(1, 0, 2)
    a2p = a2[l].reshape(LORA, N_RWKV_PAIRS, LANES).transpose(1, 0, 2)
    z = jnp.zeros_like(w2p)
    w_lora = jnp.concatenate([jnp.concatenate([w2p, z], axis=2), jnp.concatenate([z, a2p], axis=2)],
                             axis=1).astype(BF16)
    lam_init = 0.8 - 0.6 * math.exp(-0.3 * l)
    lam = (jnp.exp(jnp.sum(lambda_q1[l] * lambda_k1[l])) - jnp.exp(jnp.sum(lambda_q2[l] * lambda_k2[l]))
           + lam_init).reshape(1, 1).astype(F32)
    row = lambda z_: z_.reshape(1, -1)
    return dict(
        layer=l, ln_g=ln_g[l], w_qkvr=w_qkvr, w_rk=w_rk, w_rv=w_rv, w_gates=w_gates, w_lora_in=w_lora_in,
        w_out=w_out[l].astype(BF16),
        q_gain=jnp.tile(q_norm_g[l], 2).reshape(1, DV), k_gain=jnp.tile(k_norm_g[l], 2).reshape(1, DV),
        lam=lam, out_scale=1.0 - lam_init, subln_g=subln_g[l],
        tm_r=row(tmix[:o1]), tm_k=row(tmix[o2:o3]), tm_v=row(tmix[o3:o4]), tm_lora=tm_lora, w_lora=w_lora,
        w0=row(w0[l]), a0=row(a0[l]), k_k=row(k_k[l]), k_a=row(k_a[l]), r_k=row(r_k[l]),
        lnx_g=row(lnx_g[l]), lnx_b=row(lnx_b[l]),
    )


def _project(x, pos, p):
    b, t, d = x.shape
    m = b * t
    x2 = x.reshape(m, d)
    h = _rmsnorm(x2, p["ln_g"])

    tm = min(1024, m)
    cos, s1, s2 = _rope_tables(pos)
    if t < tm:
        reps = tm // t
        cos, s1, s2 = (jnp.tile(z, (reps, 1)) for z in (cos, s1, s2))
    gain_spec = pl.BlockSpec((1, LANES), lambda i, j: (0, 0))
    blk = lambda cols: cols // LANES
    tm_hm, tn_hm = min(512, m), 1024

    def qk(col_off, gain, scale, dts):
        tm_, tn_ = (tm_hm, tn_hm) if "f32_head_major" in dts else (tm, 512)
        n_tab = cos.shape[0] // tm_
        tab_spec = pl.BlockSpec((tm_, LANES), lambda i, j: (i % n_tab, 0))
        kern = functools.partial(_mm_qk_kernel, scale=scale, rows_per_chunk=min(256, tm_))
        return _matmul([h], p["w_qkvr"], col_block_off=blk(col_off), n_cols=ATT_W, kernel=kern,
                       out_dtypes=dts, extra=(gain, cos, s1, s2),
                       extra_specs=(gain_spec, tab_spec, tab_spec, tab_spec), scratch=True, tm=tm_, tn=tn_)

    def plain(w, col_off, n_cols):
        return _matmul([h], w, col_block_off=blk(col_off), n_cols=n_cols, kernel=_mm_plain_kernel,
                       out_dtypes=(F32,), tm=tm)[0]

    (q16,) = qk(0, p["q_gain"], DQK ** -0.5, (BF16,))
    k32, k16 = qk(OFF_K, p["k_gain"], 1.0, ("f32_head_major", BF16))
    v32, v16 = _matmul([h], p["w_qkvr"], col_block_off=blk(OFF_V), n_cols=ATT_W,
                       kernel=_mm_dual_kernel, out_dtypes=("f32_head_major", BF16), tm=tm_hm, tn=tn_hm)
    rkv = [plain(p["w_qkvr"], OFF_SHIFT, RWKV_W).reshape(b, t, RWKV_W),
           plain(p["w_rk"], 0, RWKV_W).reshape(b, t, RWKV_W),
           plain(p["w_rv"], 0, RWKV_W).reshape(b, t, RWKV_W)]
    lora = plain(p["w_lora_in"], 0, 2 * LORA).reshape(b, t, 2 * LORA)
    (gates,) = _matmul([h], p["w_gates"], col_block_off=0, n_cols=ATT_W + RWKV_W,
                       kernel=_mm_silu_kernel, out_dtypes=(BF16,), tm=tm)
    return dict(b=b, t=t, m=m, tm=tm, x2=x2, q16=q16, k32=k32, k16=k16, v32=v32, v16=v16, rkv=rkv, lora=lora,
                gates=gates)


def _rwkv_call_for(pr, shift_prev, wkv_prev, p, *, chunk, pairs):
    b = pr["b"]
    o1, o2, o3, o4 = RWKV_W, RWKV_W + LORA, 2 * RWKV_W + LORA, 3 * RWKV_W + LORA
    prev_lora = jnp.concatenate([shift_prev[:, o1:o2], shift_prev[:, o4:]], axis=1)
    prevs = [z.reshape(b, 1, -1) for z in (shift_prev[:, :o1], shift_prev[:, o2:o3], shift_prev[:, o3:o4],
                                           prev_lora)]
    return _rwkv_call(pr["rkv"], pr["lora"], pr["gates"].reshape(b, pr["t"], -1), prevs, wkv_prev, p,
                      chunk=chunk, pairs=pairs)


def _finish_layer(pr, y_att, y_rwkv, p):
    b, t, m, tm = pr["b"], pr["t"], pr["m"], pr["tm"]
    d = pr["x2"].shape[1]
    last_lora = pr["lora"][:, -1]
    shift_new = jnp.concatenate([pr["rkv"][0][:, -1], last_lora[:, :LORA], pr["rkv"][1][:, -1],
                                 pr["rkv"][2][:, -1], last_lora[:, LORA:]], axis=1)
    tn_out = 512
    x_spec = pl.BlockSpec((tm, tn_out), lambda i, j: (i, j))
    (out,) = _matmul([y_att.reshape(m, ATT_W), y_rwkv.reshape(m, RWKV_W)], p["w_out"], col_block_off=0,
                     n_cols=d, kernel=_mm_residual_kernel, out_dtypes=(F32,), extra=(pr["x2"],),
                     extra_specs=(x_spec,), tm=tm, tn=tn_out)
    return (out.reshape(b, t, d), pr["k32"].reshape(b, t, N_ATT_HEADS, DV),
            pr["v32"].reshape(b, t, N_ATT_HEADS, DV), shift_new)


def _layer_pair(xp, xs, pos_p, pos_s, shift_s, wkv_s, cache_k, cache_v, page_table, p, *,
                chunk_p=64, pairs_p=8, pairs_s=16, pages_per_step=8):
    pp = _project(xp, pos_p, p)
    ps = _project(xs, pos_s, p)
    b_s, t_s = ps["b"], ps["t"]
    rw_p = _rwkv_call_for(pp, jnp.zeros((pp["b"], SHIFT_W), F32), None, p, chunk=chunk_p, pairs=pairs_p)
    rw_s = _rwkv_call_for(ps, shift_s, wkv_s, p, chunk=t_s, pairs=pairs_s)
    at_s = _sample_attn_call(ps["q16"].reshape(b_s, t_s, ATT_W), ps["k16"].reshape(b_s, t_s, ATT_W),
                             ps["v16"].reshape(b_s, t_s, ATT_W), ps["gates"].reshape(b_s, t_s, -1),
                             cache_k, cache_v, page_table, p["lam"], p["subln_g"], layer=p["layer"],
                             out_scale=p["out_scale"], pages_per_step=pages_per_step)
    if math.prod(rw_p["grid"]) == math.prod(at_s["grid"]):
        y_rwkv_p, wkv_p, y_att_s = _rwkv_with_sample_attention(rw_p, at_s, page_table)
    else:
        y_rwkv_p, wkv_p = _rwkv(rw_p)
        y_att_s = _sample_attention(page_table, at_s)
    y_att_p = _prompt_attention(pp["q16"], pp["k16"], pp["v16"], pp["gates"], p["lam"], p["subln_g"],
                                batch=pp["b"], seq=pp["t"], out_scale=p["out_scale"])
    y_rwkv_s, wkv_s_new = _rwkv(rw_s)
    return (*_finish_layer(pp, y_att_p, y_rwkv_p, p), wkv_p), (*_finish_layer(ps, y_att_s, y_rwkv_s, p), wkv_s_new)


def kernel(x_prompt, x_sample, cache_k, cache_v, state_shift, state_wkv, page_table, ln_g, w_in, q_norm_g,
           k_norm_g, lambda_q1, lambda_k1, lambda_q2, lambda_k2, subln_g, time_mix, w0, w2, a0, a2, k_k,
           k_a, r_k, lnx_g, lnx_b, w_out):
    depth = w_in.shape[0]
    s_p, s_s = x_prompt.shape[1], x_sample.shape[1]
    past_len = page_table.shape[1] * PAGE_SIZE
    pos_p = jnp.arange(s_p)
    pos_s = past_len + jnp.arange(s_s)
    yp, ys = x_prompt, x_sample
    outs = [[] for _ in range(8)]
    for l in range(depth):
        p = _prep_layer_params(l, ln_g, w_in, q_norm_g, k_norm_g, lambda_q1, lambda_k1, lambda_q2,
                               lambda_k2, subln_g, time_mix, w0, w2, a0, a2, k_k, k_a, r_k, lnx_g, lnx_b,
                               w_out)
        (yp, kp, vp, shp, wkvp), (ys, ks, vs, shs, wkvs) = _layer_pair(
            yp, ys, pos_p, pos_s, state_shift[l], state_wkv[l], cache_k, cache_v, page_table, p)
        for lst, val in zip(outs, (kp, vp, shp, wkvp, ks, vs, shs, wkvs)):
            lst.append(val)
    stacked = [jnp.stack(lst, 0) for lst in outs]
    return (yp, ys, *stacked)
```

```python
import functools
import math

import jax
import jax.numpy as jnp
from jax import lax
from jax.experimental import pallas as pl
from jax.experimental.pallas import tpu as pltpu

F32 = jnp.float32
BF16 = jnp.bfloat16

D_MODEL = 4096
PAGE_SIZE = 128
ATT_W = 2048
RWKV_W = 2048
DV = 128
N_ATT_HEADS = ATT_W // DV
DQK = DV // 2
ROT_DIM = DQK // 4
ROPE_THETA = 500000.0
RWKV_HEAD = 64
N_RWKV_HEADS = RWKV_W // RWKV_HEAD
N_RWKV_PAIRS = N_RWKV_HEADS // 2
LORA = 64
SHIFT_W = 3 * RWKV_W + 2 * LORA
OFF_K = ATT_W
OFF_V = 2 * ATT_W
OFF_SHIFT = 3 * ATT_W
OFF_GA = OFF_SHIFT + SHIFT_W
OFF_GR = OFF_GA + ATT_W
IN_W = OFF_GR + RWKV_W
NORM_EPS = 1e-6
SUBLN_EPS = 1e-5
LNX_EPS = 64e-5

LANES = 128
VMEM_LIMIT_BYTES = 48 * 1024 * 1024
FUSED_VMEM_LIMIT_BYTES = 56 * 1024 * 1024
NEG_BIG = -1e30


def _cparams(sem, vmem_limit_bytes=VMEM_LIMIT_BYTES):
    return pltpu.CompilerParams(dimension_semantics=sem, vmem_limit_bytes=vmem_limit_bytes)


def _run(staged_body):
    for _ in staged_body:
        pass


def _interleave(*staged_bodies):
    active, finals = list(staged_bodies), []
    while active:
        for g in list(active):
            tag = next(g, "done")
            if tag == "done":
                active.remove(g)
            elif tag == "final":
                active.remove(g)
                finals.append(g)
    for g in finals:
        _run(g)


def _split_dot(x, w2_bf16):
    hi = x.astype(BF16)
    lo = (x - hi.astype(F32)).astype(BF16)
    return jnp.dot(jnp.concatenate([hi, lo], axis=1), w2_bf16, preferred_element_type=F32)


def _segment_sums(xs, w2_bf16):
    rows = xs[0].shape[0]
    out = _split_dot(jnp.concatenate(xs, axis=0), w2_bf16)
    return [out[i * rows:(i + 1) * rows] for i in range(len(xs))]


def _half_ones():
    r = (lax.broadcasted_iota(jnp.int32, (2 * LANES, LANES), 0) % LANES) // RWKV_HEAD
    c = lax.broadcasted_iota(jnp.int32, (2 * LANES, LANES), 1) // RWKV_HEAD
    return jnp.where(r == c, 1.0, 0.0).astype(BF16)


def _rmsnorm_kernel(x_ref, g_ref, o_ref):
    x = x_ref[...]
    ms = jnp.mean(x * x, axis=-1, keepdims=True)
    o_ref[...] = (x * lax.rsqrt(ms + NORM_EPS) * g_ref[...]).astype(BF16)


def _rmsnorm(x2, g, tm=512):
    m, d = x2.shape
    tm = min(tm, m)
    return pl.pallas_call(
        _rmsnorm_kernel,
        grid=(m // tm,),
        in_specs=[pl.BlockSpec((tm, d), lambda i: (i, 0)),
                  pl.BlockSpec((1, d), lambda i: (0, 0))],
        out_specs=pl.BlockSpec((tm, d), lambda i: (i, 0)),
        out_shape=jax.ShapeDtypeStruct((m, d), BF16),
        compiler_params=_cparams(("parallel",)),
        name="rmsnorm",
    )(x2, g.reshape(1, d))


def _mm_full_k(lhs_refs, w_ref):
    acc, off = None, 0
    for l_ref in lhs_refs:
        k = l_ref.shape[1]
        part = jnp.dot(l_ref[...], w_ref[off:off + k, :], preferred_element_type=F32)
        acc = part if acc is None else acc + part
        off += k
    return acc


def _mm_plain_kernel(h_ref, w_ref, o_ref):
    o_ref[...] = _mm_full_k([h_ref], w_ref).astype(o_ref.dtype)


def _mm_dual_kernel(h_ref, w_ref, o32_ref, o16_ref):
    a = _mm_full_k([h_ref], w_ref)
    o16_ref[...] = a.astype(BF16)
    for hh in range(a.shape[1] // LANES):
        o32_ref[:, hh, :] = a[:, hh * LANES:(hh + 1) * LANES]


def _mm_silu_kernel(h_ref, w_ref, o_ref):
    a = _mm_full_k([h_ref], w_ref)
    o_ref[...] = (a * jax.nn.sigmoid(a)).astype(o_ref.dtype)


def _mm_residual_kernel(ya_ref, yr_ref, w_ref, x_ref, o_ref):
    o_ref[...] = x_ref[...] + _mm_full_k([ya_ref, yr_ref], w_ref)


def _mm_qk_kernel(h_ref, w_ref, g_ref, cos_ref, s1_ref, s2_ref, *rest, scale, rows_per_chunk):
    outs, acc_ref = rest[:-1], rest[-1]
    acc_ref[...] = _mm_full_k([h_ref], w_ref)
    tm, tn = acc_ref.shape
    ones = _half_ones()
    gain = g_ref[...]

    def chunk(r, carry):
        rows = pl.ds(pl.multiple_of(r * rows_per_chunk, rows_per_chunk), rows_per_chunk)
        c = cos_ref[rows, :]
        s1 = s1_ref[rows, :]
        s2 = s2_ref[rows, :]
        for hh in range(tn // LANES):
            cols = slice(hh * LANES, (hh + 1) * LANES)
            x = acc_ref[rows, cols]
            ss = _split_dot(x * x, ones)
            y = x * lax.rsqrt(ss * (1.0 / DQK) + NORM_EPS) * gain
            y = y * c + pltpu.roll(y, LANES - ROT_DIM // 2, 1) * s1 + pltpu.roll(y, ROT_DIM // 2, 1) * s2
            if scale != 1.0:
                y = y * scale
            for o_ref in outs:
                if len(o_ref.shape) == 3:
                    o_ref[rows, hh, :] = y.astype(o_ref.dtype)
                else:
                    o_ref[rows, cols] = y.astype(o_ref.dtype)
        return carry

    lax.fori_loop(0, tm // rows_per_chunk, chunk, 0)


def _matmul(lhs, w, *, col_block_off, n_cols, kernel, out_dtypes, extra=(), extra_specs=(),
            scratch=False, tm=1024, tn=512):
    m = lhs[0].shape[0]
    tm = min(tm, m)
    tn = min(tn, n_cols)
    assert m % tm == 0 and n_cols % tn == 0 and sum(l.shape[1] for l in lhs) == w.shape[0]
    assert (col_block_off * LANES) % tn == 0
    joff = col_block_off * LANES // tn
    in_specs = [pl.BlockSpec((tm, l.shape[1]), lambda i, j: (i, 0)) for l in lhs]
    in_specs.append(pl.BlockSpec((w.shape[0], tn), lambda i, j: (0, j + joff)))
    in_specs += list(extra_specs)
    out_specs, out_shape = [], []
    for dt in out_dtypes:
        if dt == "f32_head_major":
            out_specs.append(pl.BlockSpec((tm, tn // LANES, LANES), lambda i, j: (i, j, 0)))
            out_shape.append(jax.ShapeDtypeStruct((m, n_cols // LANES, LANES), F32))
        else:
            out_specs.append(pl.BlockSpec((tm, tn), lambda i, j: (i, j)))
            out_shape.append(jax.ShapeDtypeStruct((m, n_cols), dt))
    return pl.pallas_call(
        kernel,
        grid=(m // tm, n_cols // tn),
        in_specs=in_specs,
        out_specs=out_specs,
        out_shape=out_shape,
        scratch_shapes=[pltpu.VMEM((tm, tn), F32)] if scratch else [],
        compiler_params=_cparams(("parallel", "arbitrary")),
        name="proj_matmul",
    )(*lhs, w, *extra)


def _rope_tables(pos):
    half = ROT_DIM // 2
    inv_freq = 1.0 / (ROPE_THETA ** (jnp.arange(0, ROT_DIM, 2, dtype=F32) / ROT_DIM))
    ang = pos.astype(F32)[:, None] * inv_freq[None, :]
    cos = jnp.cos(ang)
    sin = jnp.sin(ang)
    t = pos.shape[0]
    one = jnp.ones((t, DQK - ROT_DIM), F32)
    zero = jnp.zeros((t, DQK - ROT_DIM), F32)
    zh = jnp.zeros((t, half), F32)
    c64 = jnp.concatenate([cos, cos, one], axis=1)
    s1_64 = jnp.concatenate([-sin, zh, zero], axis=1)
    s2_64 = jnp.concatenate([zh, sin, zero], axis=1)
    tile2 = lambda z: jnp.concatenate([z, z], axis=1)
    return tile2(c64), tile2(s1_64), tile2(s2_64)


def _stack_maps(q):
    lane = lax.broadcasted_iota(jnp.int32, q.shape, 1)
    z = jnp.zeros_like(q)
    return jnp.concatenate([jnp.where(lane < DQK, q, z), jnp.where(lane >= DQK, q, z)], axis=0)


def _online_softmax_steps(ss, vs, ms, ls, accs):
    def bcast(m, n):
        return jnp.tile(m, (1, n // LANES)) if n >= LANES else m[:, :n]

    m_news = [jnp.maximum(m, jnp.max(s, axis=-1, keepdims=True)) for s, m in zip(ss, ms)]
    alphas = [jnp.exp2(m - mn) for m, mn in zip(ms, m_news)]
    ps = [jnp.exp2(s - bcast(mn, s.shape[1])) for s, mn in zip(ss, m_news)]
    l_news = [a * l + jnp.sum(p, axis=-1, keepdims=True) for a, l, p in zip(alphas, ls, ps)]
    pvs = [jnp.dot(p.astype(BF16), v, preferred_element_type=F32) for p, v in zip(ps, vs)]
    acc_news = [a * acc + pv for a, acc, pv in zip(alphas, accs, pvs)]
    return m_news, l_news, acc_news


def _attn_finish(acc, l, lam, subln_g, gate, out_scale):
    t = acc.shape[0] // 2
    o = acc[:t] / l[:t] - lam * (acc[t:] / l[t:])
    ms = jnp.mean(o * o, axis=-1, keepdims=True)
    o = o * lax.rsqrt(ms + SUBLN_EPS) * subln_g * out_scale
    return (o * gate.astype(F32)).astype(BF16)


def _prompt_attn_kernel(lam_ref, q_ref, k_ref, v_ref, g_ref, sg_ref, o_ref,
                        q12_ref, m_ref, l_ref, acc_ref, *, tq, heads, out_scale):
    qi = pl.program_id(2)
    for hp in range(heads):
        q12_ref[hp] = _stack_maps(q_ref[:, hp * DV:(hp + 1) * DV])
    m_ref[...] = jnp.full_like(m_ref, NEG_BIG)
    l_ref[...] = jnp.zeros_like(l_ref)
    acc_ref[...] = jnp.zeros_like(acc_ref)

    def tile(kj, masked):
        rows = pl.ds(pl.multiple_of(kj * tq, tq), tq)
        hcols = [slice(hp * DV, (hp + 1) * DV) for hp in range(heads)]
        ss = [lax.dot_general(q12_ref[hp], k_ref[rows, hcols[hp]], (((1,), (1,)), ((), ())),
                              preferred_element_type=F32) for hp in range(heads)]
        if masked:
            r = lax.broadcasted_iota(jnp.int32, ss[0].shape, 0) % tq
            c = lax.broadcasted_iota(jnp.int32, ss[0].shape, 1)
            ss = [jnp.where(c <= r, s, NEG_BIG) for s in ss]
        ms, ls, accs = _online_softmax_steps(
            ss, [v_ref[rows, hcols[hp]] for hp in range(heads)],
            [m_ref[hp] for hp in range(heads)], [l_ref[hp] for hp in range(heads)],
            [acc_ref[hp] for hp in range(heads)])
        for hp in range(heads):
            m_ref[hp] = ms[hp]
            l_ref[hp] = ls[hp]
            acc_ref[hp] = accs[hp]

    def body(kj, carry):
        tile(kj, False)
        return carry

    lax.fori_loop(0, qi, body, 0)
    tile(qi, True)
    lam = lam_ref[0, 0]
    for hp in range(heads):
        cols = slice(hp * DV, (hp + 1) * DV)
        o_ref[:, cols] = _attn_finish(acc_ref[hp], l_ref[hp], lam, sg_ref[...], g_ref[:, cols], out_scale)


def _prompt_attention(q, k16, v16, gates, lam, subln_g, *, batch, seq, out_scale, tq=512, heads=2):
    tq = min(tq, seq)
    nq = seq // tq
    w = heads * DV
    kern = functools.partial(_prompt_attn_kernel, tq=tq, heads=heads, out_scale=out_scale)
    return pl.pallas_call(
        kern,
        grid=(batch, N_ATT_HEADS // heads, nq),
        in_specs=[
            pl.BlockSpec(memory_space=pltpu.SMEM),
            pl.BlockSpec((tq, w), lambda b, h, i: (b * nq + i, h)),
            pl.BlockSpec((seq, w), lambda b, h, i: (b, h)),
            pl.BlockSpec((seq, w), lambda b, h, i: (b, h)),
            pl.BlockSpec((tq, w), lambda b, h, i: (b * nq + i, h)),
            pl.BlockSpec((1, DV), lambda b, h, i: (0, 0)),
        ],
        out_specs=pl.BlockSpec((tq, w), lambda b, h, i: (b * nq + i, h)),
        out_shape=jax.ShapeDtypeStruct((batch * seq, ATT_W), BF16),
        scratch_shapes=[pltpu.VMEM((heads, 2 * tq, DV), BF16),
                        pltpu.VMEM((heads, 2 * tq, LANES), F32),
                        pltpu.VMEM((heads, 2 * tq, LANES), F32),
                        pltpu.VMEM((heads, 2 * tq, DV), F32)],
        compiler_params=_cparams(("parallel", "parallel", "arbitrary")),
        name="prompt_attention",
    )(lam, q, k16, v16, gates, subln_g.reshape(1, DV))


QUADS = 4
HEADS_PER_QUAD = N_ATT_HEADS // QUADS


def _sample_attn_body(j, n_steps, lam_ref, q_ref, refs, *, t_new, pages_per_step, out_scale):
    kc_refs = refs[:pages_per_step]
    vc_refs = refs[pages_per_step:2 * pages_per_step]
    kn_ref, vn_ref, g_ref, sg_ref, o_ref, q12_ref, m_ref, l_ref, acc_ref = refs[2 * pages_per_step:]
    rows = 2 * t_new
    qrows = HEADS_PER_QUAD * rows
    ncols = PAGE_SIZE * HEADS_PER_QUAD

    @pl.when(j == 0)
    def _():
        for h in range(N_ATT_HEADS):
            c, i = h % QUADS, h // QUADS
            q12_ref[c, i * rows:(i + 1) * rows, :] = _stack_maps(q_ref[0, :, h * DV:(h + 1) * DV])
        m_ref[...] = jnp.full_like(m_ref, NEG_BIG)
        l_ref[...] = jnp.zeros_like(l_ref)
        acc_ref[...] = jnp.zeros_like(acc_ref)

    yield "init"
    own = (lax.broadcasted_iota(jnp.int32, (qrows, ncols), 0) // rows
           == lax.broadcasted_iota(jnp.int32, (qrows, ncols), 1) % HEADS_PER_QUAD)
    nt = (((1,), (1,)), ((), ()))
    quads = range(QUADS)
    ms = [m_ref[c] for c in quads]
    ls = [l_ref[c] for c in quads]
    accs = [acc_ref[c] for c in quads]
    for u in range(pages_per_step):
        scores = [jnp.where(own, lax.dot_general(q12_ref[c], kc_refs[u][pl.ds(c, ncols, stride=QUADS), :].astype(BF16),
                                                 nt, preferred_element_type=F32), NEG_BIG) for c in quads]
        yield "scores"
        vals = [vc_refs[u][pl.ds(c, ncols, stride=QUADS), :].astype(BF16) for c in quads]
        ms, ls, accs = _online_softmax_steps(scores, vals, ms, ls, accs)
        yield "softmax"
    for c in quads:
        m_ref[c] = ms[c]
        l_ref[c] = ls[c]
        acc_ref[c] = accs[c]

    yield "final"

    @pl.when(j == n_steps - 1)
    def _():
        r = lax.broadcasted_iota(jnp.int32, (rows, t_new), 0) % t_new
        cc = lax.broadcasted_iota(jnp.int32, (rows, t_new), 1)
        causal = cc <= r
        lam = lam_ref[0, 0]
        for h in range(N_ATT_HEADS):
            c, i = h % QUADS, h // QUADS
            sl = slice(i * rows, (i + 1) * rows)
            cols = slice(h * DV, (h + 1) * DV)
            s = lax.dot_general(q12_ref[c, sl, :], kn_ref[0, :, cols], (((1,), (1,)), ((), ())),
                                preferred_element_type=F32)
            s = jnp.where(causal, s, NEG_BIG)
            _, (l,), (acc,) = _online_softmax_steps([s], [vn_ref[0, :, cols]], [m_ref[c, sl, :]],
                                                    [l_ref[c, sl, :]], [acc_ref[c, sl, :]])
            o_ref[0, :, cols] = _attn_finish(acc, l, lam, sg_ref[...], g_ref[0, :, cols], out_scale)


def _sample_attn_kernel(pt_ref, lam_ref, q_ref, *refs, **kw):
    _run(_sample_attn_body(pl.program_id(1), pl.num_programs(1), lam_ref, q_ref, refs, **kw))


def _sample_attn_call(q, k16, v16, gates, cache_k, cache_v, page_table, lam, subln_g, *, layer, out_scale,
                      pages_per_step=8):
    b, t_new, _ = q.shape
    n_pages = page_table.shape[1]
    n_pool = cache_k.shape[1]
    pps = math.gcd(pages_per_step, n_pages)
    page_rows = PAGE_SIZE * N_ATT_HEADS
    kc2 = cache_k.reshape(-1, DV)
    vc2 = cache_v.reshape(-1, DV)
    tok_spec = pl.BlockSpec((1, t_new, ATT_W), lambda i, j, pt: (i, 0, 0))

    def page_spec(u):
        return pl.BlockSpec((page_rows, DV), lambda i, j, pt: (layer * n_pool + pt[i, j * pps + u], 0))

    page_specs = [page_spec(u) for u in range(pps)]
    qrows = HEADS_PER_QUAD * 2 * t_new
    return dict(
        grid=(b, n_pages // pps),
        kw=dict(t_new=t_new, pages_per_step=pps, out_scale=out_scale),
        in_specs=[pl.BlockSpec(memory_space=pltpu.SMEM), tok_spec] + page_specs + page_specs
                 + [tok_spec, tok_spec, tok_spec, pl.BlockSpec((1, DV), lambda i, j, pt: (0, 0))],
        args=[lam, q] + [kc2] * pps + [vc2] * pps + [k16, v16, gates, subln_g.reshape(1, DV)],
        out_specs=[tok_spec],
        out_shape=[jax.ShapeDtypeStruct((b, t_new, ATT_W), BF16)],
        scratch=[pltpu.VMEM((QUADS, qrows, DV), BF16),
                 pltpu.VMEM((QUADS, qrows, LANES), F32),
                 pltpu.VMEM((QUADS, qrows, LANES), F32),
                 pltpu.VMEM((QUADS, qrows, DV), F32)],
    )


def _sample_attention(page_table, call):
    grid_spec = pltpu.PrefetchScalarGridSpec(
        num_scalar_prefetch=1, grid=call["grid"], in_specs=call["in_specs"], out_specs=call["out_specs"],
        scratch_shapes=call["scratch"])
    return pl.pallas_call(
        functools.partial(_sample_attn_kernel, **call["kw"]),
        grid_spec=grid_spec,
        out_shape=call["out_shape"],
        compiler_params=_cparams(("parallel", "arbitrary")),
        name="sample_attention",
    )(page_table, *call["args"])[0]


def _stack_heads(x):
    lane = lax.broadcasted_iota(jnp.int32, x.shape, 1)
    z = jnp.zeros_like(x)
    return jnp.concatenate([jnp.where(lane < RWKV_HEAD, x, z), jnp.where(lane >= RWKV_HEAD, x, z)], axis=0)


def _each(f, *lists):
    return [f(*args) for args in zip(*lists)]


def _mm(a, b):
    return jnp.dot(a, b, preferred_element_type=F32)


def _rwkv_chunk(rs, krs, vs, lo, wls, w0s, a0s, k_ks, k_as, r_ks, lnx_gs, lnx_bs, gates, h0s, consts):
    ones, tril, strict, incl, eye = consts
    C = lo.shape[0]
    n2 = 2 * C
    lane = lax.broadcasted_iota(jnp.int32, (C, LANES), 1)
    lo_act = jnp.where(lane < LORA, jnp.tanh(lo), lo).astype(BF16)
    pres = [_mm(lo_act, wl) for wl in wls]
    logws = _each(lambda pre, w0: -math.exp(-0.5) * jax.nn.sigmoid(w0 + pre[:, :LANES]), pres, w0s)
    a_sigs = _each(lambda pre, a0: jax.nn.sigmoid(a0 + pre[:, LANES:]), pres, a0s)
    yield "stage"

    kks = _each(lambda kr, k_k: kr * k_k, krs, k_ks)
    kk_ss = _segment_sums([kk * kk for kk in kks], ones)
    kks = _each(lambda kk, ss: kk / jnp.maximum(jnp.sqrt(ss), 1e-12), kks, kk_ss)
    k_hs = _each(lambda kr, a_sig, k_a: kr * (1.0 + (a_sig - 1.0) * k_a), krs, a_sigs, k_as)
    rk_ss = _segment_sums([r * k_h * r_k for r, k_h, r_k in zip(rs, k_hs, r_ks)], ones)
    bonuses = _each(lambda ss, v: ss * v, rk_ss, vs)
    yield "stage"

    def split3(x):
        l1 = x.astype(BF16)
        rem = x - l1.astype(F32)
        l2 = rem.astype(BF16)
        return jnp.concatenate([l1, l2, (rem - l2.astype(F32)).astype(BF16)], axis=0)

    cums = [_mm(tril, split3(logw)) for logw in logws]
    yield "stage"
    g_incls = [jnp.exp(cum) for cum in cums]
    g_excls = _each(lambda cum, logw: jnp.exp(cum - logw), cums, logws)
    g_invs = [jnp.exp(-cum) for cum in cums]
    g_lasts = [g[C - 1:C, :] for g in g_incls]

    a_ts = _each(lambda kk, g: _stack_heads(-kk * g).astype(BF16), kks, g_excls)
    r_ts = _each(lambda r, g: _stack_heads(r * g).astype(BF16), rs, g_incls)
    v_sts = [_stack_heads(v).astype(BF16) for v in vs]
    bks = _each(lambda kk, a_sig, k_h, g: jnp.concatenate([_stack_heads(kk * a_sig * g), _stack_heads(k_h * g)],
                                                          axis=0), kks, a_sigs, k_hs, g_invs)
    gmats = _each(lambda a_t, r_t, bk: lax.dot_general(jnp.concatenate([a_t, r_t], axis=0), bk.astype(BF16),
                                                       (((1,), (1,)), ((), ())), preferred_element_type=F32),
                  a_ts, r_ts, bks)
    yield "stage"
    a_abs = [jnp.where(strict, g[:n2, :n2], 0.0) for g in gmats]
    a_aks = [jnp.where(strict, g[:n2, n2:], 0.0).astype(BF16) for g in gmats]
    a_rbs = [jnp.where(incl, g[n2:, :n2], 0.0).astype(BF16) for g in gmats]
    a_rks = [jnp.where(incl, g[n2:, n2:], 0.0).astype(BF16) for g in gmats]
    yield "stage"

    eye2 = jnp.where(lax.broadcasted_iota(jnp.int32, (n2, n2), 0)
                     == lax.broadcasted_iota(jnp.int32, (n2, n2), 1), 1.0, 0.0)
    t_invs = [eye2 + a for a in a_abs]
    pws = [a.astype(BF16) for a in a_abs]
    for _ in range(int(math.log2(C)) - 1):
        pws = [_mm(pw, pw).astype(BF16) for pw in pws]
        t_invs = _each(lambda t, pw: t + _mm(t.astype(BF16), pw), t_invs, pws)
        yield "stage"

    h16s = [h0.astype(BF16) for h0 in h0s]
    if n2 % LANES == 0:
        cat = jnp.concatenate
        xs = _each(lambda a_t, h16, a_ak, v_st: _mm(cat([a_t, a_ak], axis=1), cat([h16, v_st], axis=0)),
                   a_ts, h16s, a_aks, v_sts)
    else:
        xs = _each(lambda a_t, h16, a_ak, v_st: _mm(a_t, h16) + _mm(a_ak, v_st), a_ts, h16s, a_aks, v_sts)
    u16s = _each(lambda t, x: _mm(t.astype(BF16), x.astype(BF16)).astype(BF16), t_invs, xs)
    yield "stage"
    if n2 % LANES == 0:
        o_sts = _each(lambda r_t, h16, a_rb, u16, a_rk, v_st:
                      _mm(cat([r_t, a_rb, a_rk], axis=1), cat([h16, u16, v_st], axis=0)),
                      r_ts, h16s, a_rbs, u16s, a_rks, v_sts)
    else:
        o_sts = _each(lambda r_t, h16, a_rb, u16, a_rk, v_st: _mm(r_t, h16) + _mm(a_rb, u16) + _mm(a_rk, v_st),
                      r_ts, h16s, a_rbs, u16s, a_rks, v_sts)
    os_ = [o_st[:C] + o_st[C:] for o_st in o_sts]
    yield "stage"

    bk_ts = _each(lambda bk, g_last: (bk * g_last).T.astype(BF16), bks, g_lasts)
    g_cols = [jnp.sum(jnp.where(eye, jnp.broadcast_to(g_last, (LANES, LANES)), 0.0), axis=1, keepdims=True)
              for g_last in g_lasts]
    h_news = _each(lambda g_col, h0, bk_t, u16, v_st: g_col * h0 + _mm(bk_t, jnp.concatenate([u16, v_st], axis=0)),
                   g_cols, h0s, bk_ts, u16s, v_sts)
    yield "stage"

    mus = [z * (1.0 / RWKV_HEAD) for z in _segment_sums(os_, ones)]
    ds = _each(lambda o, mu: o - mu, os_, mus)
    vars_ = [z * (1.0 / RWKV_HEAD) for z in _segment_sums([d * d for d in ds], ones)]
    ys = _each(lambda d, var, g, b, bonus, gate:
               (((d * lax.rsqrt(var + LNX_EPS) * g + b) + bonus) * gate.astype(F32)).astype(BF16),
               ds, vars_, lnx_gs, lnx_bs, bonuses, gates)
    return ys, h_news


def _rwkv_body(c, n_chunks, refs, *, chunk, pairs, use_s0):
    (r_ref, k_ref, v_ref, lo_ref, pr_ref, pk_ref, pv_ref, plo_ref,
     tmr_ref, tmk_ref, tmv_ref, tmlo_ref, wl_ref, w0_ref, a0_ref, kk_ref, ka_ref, rk_ref,
     lg_ref, lb_ref, gate_ref, s0_ref, y_ref, sout_ref,
     h_ref, cr_ref, ck_ref, cv_ref, clo_ref) = refs
    C = chunk
    n2 = 2 * C

    @pl.when(c == 0)
    def _():
        cr_ref[...] = pr_ref[0]
        ck_ref[...] = pk_ref[0]
        cv_ref[...] = pv_ref[0]
        clo_ref[...] = plo_ref[0]
        if use_s0:
            z = jnp.zeros((RWKV_HEAD, RWKV_HEAD), F32)
            for u in range(pairs):
                st = jnp.concatenate([jnp.concatenate([s0_ref[0, 2 * u], z], axis=1),
                                      jnp.concatenate([z, s0_ref[0, 2 * u + 1]], axis=1)], axis=0)
                h_ref[u] = st.T
        else:
            h_ref[...] = jnp.zeros_like(h_ref)

    yield "init"
    ri = lax.broadcasted_iota(jnp.int32, (n2, n2), 0)
    ci = lax.broadcasted_iota(jnp.int32, (n2, n2), 1)
    tril = (lax.broadcasted_iota(jnp.int32, (C, 3 * C), 1) % C <= lax.broadcasted_iota(jnp.int32, (C, 3 * C), 0))
    eye = (lax.broadcasted_iota(jnp.int32, (LANES, LANES), 0)
           == lax.broadcasted_iota(jnp.int32, (LANES, LANES), 1))
    consts = (_half_ones(), jnp.where(tril, 1.0, 0.0).astype(BF16), ri > ci, ri >= ci, eye)

    def token_shift(cur_ref3, carry_ref, tm_ref):
        cur = cur_ref3[0]
        row0 = lax.broadcasted_iota(jnp.int32, cur.shape, 0) == 0
        prev = jnp.where(row0, carry_ref[...], pltpu.roll(cur, 1, 0))
        carry_ref[...] = cur[C - 1:C, :]
        return cur + tm_ref[...] * (prev - cur)

    lo = token_shift(lo_ref, clo_ref, tmlo_ref)
    r_all = token_shift(r_ref, cr_ref, tmr_ref)
    k_all = token_shift(k_ref, ck_ref, tmk_ref)
    v_all = token_shift(v_ref, cv_ref, tmv_ref)
    col = lambda u: slice(u * LANES, (u + 1) * LANES)
    per_pair = lambda x: [x[:, col(u)] for u in range(pairs)]
    ys, h_news = yield from _rwkv_chunk(
        per_pair(r_all), per_pair(k_all), per_pair(v_all), lo, [wl_ref[u] for u in range(pairs)],
        per_pair(w0_ref[...]), per_pair(a0_ref[...]), per_pair(kk_ref[...]), per_pair(ka_ref[...]),
        per_pair(rk_ref[...]), per_pair(lg_ref[...]), per_pair(lb_ref[...]), per_pair(gate_ref[0]),
        [h_ref[u] for u in range(pairs)], consts)
    for u in range(pairs):
        y_ref[0, :, col(u)] = ys[u]
        h_ref[u] = h_news[u]

    yield "final"

    @pl.when(c == n_chunks - 1)
    def _():
        for u in range(pairs):
            st = h_news[u].T
            sout_ref[0, 2 * u] = st[:RWKV_HEAD, :RWKV_HEAD]
            sout_ref[0, 2 * u + 1] = st[RWKV_HEAD:, RWKV_HEAD:]


def _rwkv_kernel(*refs, **kw):
    _run(_rwkv_body(pl.program_id(2), pl.num_programs(2), refs, **kw))


def _rwkv_call(rkv, lora, gates, prevs, s0, p, *, chunk, pairs):
    b, t, _ = lora.shape
    n_chunks = t // chunk
    ng = N_RWKV_PAIRS // pairs
    w = pairs * LANES
    use_s0 = s0 is not None
    sblk = (1, 2 * pairs, RWKV_HEAD, RWKV_HEAD)
    if s0 is None:
        s0 = jnp.zeros(sblk, F32)
        s0_spec = pl.BlockSpec(sblk, lambda i, j, c: (0, 0, 0, 0))
    else:
        s0_spec = pl.BlockSpec(sblk, lambda i, j, c: (i, j, 0, 0))

    tok = pl.BlockSpec((1, chunk, w), lambda i, j, c: (i, c, j))
    prev = pl.BlockSpec((1, 1, w), lambda i, j, c: (i, 0, j))

    def par():
        return pl.BlockSpec((1, w), lambda i, j, c: (0, j))

    return dict(
        grid=(b, ng, n_chunks),
        kw=dict(chunk=chunk, pairs=pairs, use_s0=use_s0),
        in_specs=[
            tok, tok, tok,
            pl.BlockSpec((1, chunk, LANES), lambda i, j, c: (i, c, 0)),
            prev, prev, prev,
            pl.BlockSpec((1, 1, LANES), lambda i, j, c: (i, 0, 0)),
            par(), par(), par(), pl.BlockSpec((1, LANES), lambda i, j, c: (0, 0)),
            pl.BlockSpec((pairs, LANES, 2 * LANES), lambda i, j, c: (j, 0, 0)),
            par(), par(), par(), par(), par(), par(), par(),
            pl.BlockSpec((1, chunk, w), lambda i, j, c: (i, c, ng + j)),
            s0_spec,
        ],
        args=[*rkv, lora, *prevs, p["tm_r"], p["tm_k"], p["tm_v"], p["tm_lora"], p["w_lora"],
              p["w0"], p["a0"], p["k_k"], p["k_a"], p["r_k"], p["lnx_g"], p["lnx_b"], gates, s0],
        out_specs=[pl.BlockSpec((1, chunk, w), lambda i, j, c: (i, c, j)),
                   pl.BlockSpec(sblk, lambda i, j, c: (i, j, 0, 0))],
        out_shape=[jax.ShapeDtypeStruct((b, t, RWKV_W), BF16),
                   jax.ShapeDtypeStruct((b, N_RWKV_HEADS, RWKV_HEAD, RWKV_HEAD), F32)],
        scratch=[pltpu.VMEM((pairs, LANES, LANES), F32),
                 pltpu.VMEM((1, w), F32), pltpu.VMEM((1, w), F32),
                 pltpu.VMEM((1, w), F32), pltpu.VMEM((1, LANES), F32)],
    )


def _rwkv(call):
    return pl.pallas_call(
        functools.partial(_rwkv_kernel, **call["kw"]),
        grid=call["grid"],
        in_specs=call["in_specs"],
        out_specs=call["out_specs"],
        out_shape=call["out_shape"],
        scratch_shapes=call["scratch"],
        compiler_params=_cparams(("parallel", "parallel", "arbitrary")),
        name="rwkv7_chunked",
    )(*call["args"])


def _remap(spec, decode):
    if spec.block_shape is None:
        return spec
    return pl.BlockSpec(spec.block_shape, lambda *a: spec.index_map(*decode(*a)))


def _fused_kernel(pt_ref, *refs, n_in, n_out, n_scr, rwkv_kw, attn_kw, n_chunks, attn_steps):
    (ri, ai), (ro, ao), (rs, as_) = n_in, n_out, n_scr
    r_in, a_in = refs[:ri], refs[ri:ri + ai]
    r_out, a_out = refs[ri + ai:ri + ai + ro], refs[ri + ai + ro:ri + ai + ro + ao]
    r_scr, a_scr = refs[ri + ai + ro + ao:ri + ai + ro + ao + rs], refs[ri + ai + ro + ao + rs:]
    s = pl.program_id(0)
    _interleave(
        _rwkv_body(s % n_chunks, n_chunks, (*r_in, *r_out, *r_scr), **rwkv_kw),
        _sample_attn_body(s % attn_steps, attn_steps, a_in[0], a_in[1], (*a_in[2:], *a_out, *a_scr), **attn_kw))


def _rwkv_with_sample_attention(rwkv_call, attn_call, page_table):
    b_r, ng, nc = rwkv_call["grid"]
    b_a, nj = attn_call["grid"]
    n_steps = b_r * ng * nc
    assert n_steps == b_a * nj
    dec_r = lambda s, pt: (s // (ng * nc), (s // nc) % ng, s % nc)
    dec_a = lambda s, pt: (s // nj, s % nj, pt)
    in_specs = [_remap(sp, dec_r) for sp in rwkv_call["in_specs"]] + [_remap(sp, dec_a) for sp in attn_call["in_specs"]]
    out_specs = [_remap(sp, dec_r) for sp in rwkv_call["out_specs"]] + [_remap(sp, dec_a) for sp in attn_call["out_specs"]]
    kern = functools.partial(
        _fused_kernel,
        n_in=(len(rwkv_call["in_specs"]), len(attn_call["in_specs"])),
        n_out=(len(rwkv_call["out_specs"]), len(attn_call["out_specs"])),
        n_scr=(len(rwkv_call["scratch"]), len(attn_call["scratch"])),
        rwkv_kw=rwkv_call["kw"], attn_kw=attn_call["kw"], n_chunks=nc, attn_steps=nj)
    grid_spec = pltpu.PrefetchScalarGridSpec(
        num_scalar_prefetch=1, grid=(n_steps,), in_specs=in_specs, out_specs=out_specs,
        scratch_shapes=rwkv_call["scratch"] + attn_call["scratch"])
    y_rwkv, s_out, y_att = pl.pallas_call(
        kern,
        grid_spec=grid_spec,
        out_shape=rwkv_call["out_shape"] + attn_call["out_shape"],
        compiler_params=_cparams(("arbitrary",), FUSED_VMEM_LIMIT_BYTES),
        name="rwkv7_and_paged_attention",
    )(page_table, *rwkv_call["args"], *attn_call["args"])
    return y_rwkv, s_out, y_att


def _prep_layer_params(l, ln_g, w_in, q_norm_g, k_norm_g, lambda_q1, lambda_k1, lambda_q2, lambda_k2,
                       subln_g, time_mix, w0, w2, a0, a2, k_k, k_a, r_k, lnx_g, lnx_b, w_out):
    w = w_in[l]
    sh = OFF_SHIFT
    o1, o2, o3, o4 = RWKV_W, RWKV_W + LORA, 2 * RWKV_W + LORA, 3 * RWKV_W + LORA
    w_qkvr = w[:, :sh + o1].astype(BF16)
    w_rk = w[:, sh + o2:sh + o3].astype(BF16)
    w_rv = w[:, sh + o3:sh + o4].astype(BF16)
    w_gates = w[:, OFF_GA:].astype(BF16)
    w_lora_in = jnp.concatenate([w[:, sh + o1:sh + o2], w[:, sh + o4:sh + SHIFT_W]], axis=1).astype(BF16)
    tmix = time_mix[l]
    tm_lora = jnp.concatenate([tmix[o1:o2], tmix[o4:]]).reshape(1, 2 * LORA)
    w2p = w2[l].reshape(LORA, N_RWKV_PAIRS, LANES).transpose(1, 0, 2)
    a2p = a2[l].reshape(LORA, N_RWKV_PAIRS, LANES).transpose(1, 0, 2)
    z = jnp.zeros_like(w2p)
    w_lora = jnp.concatenate([jnp.concatenate([w2p, z], axis=2), jnp.concatenate([z, a2p], axis=2)],
                             axis=1).astype(BF16)
    lam_init = 0.8 - 0.6 * math.exp(-0.3 * l)
    lam = (jnp.exp(jnp.sum(lambda_q1[l] * lambda_k1[l])) - jnp.exp(jnp.sum(lambda_q2[l] * lambda_k2[l]))
           + lam_init).reshape(1, 1).astype(F32)
    row = lambda z_: z_.reshape(1, -1)
    return dict(
        layer=l, ln_g=ln_g[l], w_qkvr=w_qkvr, w_rk=w_rk, w_rv=w_rv, w_gates=w_gates, w_lora_in=w_lora_in,
        w_out=w_out[l].astype(BF16),
        q_gain=jnp.tile(q_norm_g[l], 2).reshape(1, DV), k_gain=jnp.tile(k_norm_g[l], 2).reshape(1, DV),
        lam=lam, out_scale=1.0 - lam_init, subln_g=subln_g[l],
        tm_r=row(tmix[:o1]), tm_k=row(tmix[o2:o3]), tm_v=row(tmix[o3:o4]), tm_lora=tm_lora, w_lora=w_lora,
        w0=row(w0[l]), a0=row(a0[l]), k_k=row(k_k[l]), k_a=row(k_a[l]), r_k=row(r_k[l]),
        lnx_g=row(lnx_g[l]), lnx_b=row(lnx_b[l]),
    )


def _project(x, pos, p):
    b, t, d = x.shape
    m = b * t
    x2 = x.reshape(m, d)
    h = _rmsnorm(x2, p["ln_g"])

    tm = min(1024, m)
    cos, s1, s2 = _rope_tables(pos)
    if t < tm:
        reps = tm // t
        cos, s1, s2 = (jnp.tile(z, (reps, 1)) for z in (cos, s1, s2))
    gain_spec = pl.BlockSpec((1, LANES), lambda i, j: (0, 0))
    blk = lambda cols: cols // LANES
    tm_hm, tn_hm = min(512, m), 1024

    def qk(col_off, gain, scale, dts):
        tm_, tn_ = (tm_hm, tn_hm) if "f32_head_major" in dts else (tm, 512)
        n_tab = cos.shape[0] // tm_
        tab_spec = pl.BlockSpec((tm_, LANES), lambda i, j: (i % n_tab, 0))
        kern = functools.partial(_mm_qk_kernel, scale=scale, rows_per_chunk=min(256, tm_))
        return _matmul([h], p["w_qkvr"], col_block_off=blk(col_off), n_cols=ATT_W, kernel=kern,
                       out_dtypes=dts, extra=(gain, cos, s1, s2),
                       extra_specs=(gain_spec, tab_spec, tab_spec, tab_spec), scratch=True, tm=tm_, tn=tn_)

    def plain(w, col_off, n_cols):
        return _matmul([h], w, col_block_off=blk(col_off), n_cols=n_cols, kernel=_mm_plain_kernel,
                       out_dtypes=(F32,), tm=tm)[0]

    (q16,) = qk(0, p["q_gain"], DQK ** -0.5 * math.log2(math.e), (BF16,))
    k32, k16 = qk(OFF_K, p["k_gain"], 1.0, ("f32_head_major", BF16))
    v32, v16 = _matmul([h], p["w_qkvr"], col_block_off=blk(OFF_V), n_cols=ATT_W,
                       kernel=_mm_dual_kernel, out_dtypes=("f32_head_major", BF16), tm=tm_hm, tn=tn_hm)
    rkv = [plain(p["w_qkvr"], OFF_SHIFT, RWKV_W).reshape(b, t, RWKV_W),
           plain(p["w_rk"], 0, RWKV_W).reshape(b, t, RWKV_W),
           plain(p["w_rv"], 0, RWKV_W).reshape(b, t, RWKV_W)]
    lora = plain(p["w_lora_in"], 0, 2 * LORA).reshape(b, t, 2 * LORA)
    (gates,) = _matmul([h], p["w_gates"], col_block_off=0, n_cols=ATT_W + RWKV_W,
                       kernel=_mm_silu_kernel, out_dtypes=(BF16,), tm=tm)
    return dict(b=b, t=t, m=m, tm=tm, x2=x2, q16=q16, k32=k32, k16=k16, v32=v32, v16=v16, rkv=rkv, lora=lora,
                gates=gates)


def _rwkv_call_for(pr, shift_prev, wkv_prev, p, *, chunk, pairs):
    b = pr["b"]
    o1, o2, o3, o4 = RWKV_W, RWKV_W + LORA, 2 * RWKV_W + LORA, 3 * RWKV_W + LORA
    prev_lora = jnp.concatenate([shift_prev[:, o1:o2], shift_prev[:, o4:]], axis=1)
    prevs = [z.reshape(b, 1, -1) for z in (shift_prev[:, :o1], shift_prev[:, o2:o3], shift_prev[:, o3:o4],
                                           prev_lora)]
    return _rwkv_call(pr["rkv"], pr["lora"], pr["gates"].reshape(b, pr["t"], -1), prevs, wkv_prev, p,
                      chunk=chunk, pairs=pairs)


def _finish_layer(pr, y_att, y_rwkv, p):
    b, t, m, tm = pr["b"], pr["t"], pr["m"], pr["tm"]
    d = pr["x2"].shape[1]
    last_lora = pr["lora"][:, -1]
    shift_new = jnp.concatenate([pr["rkv"][0][:, -1], last_lora[:, :LORA], pr["rkv"][1][:, -1],
                                 pr["rkv"][2][:, -1], last_lora[:, LORA:]], axis=1)
    tn_out = 512
    x_spec = pl.BlockSpec((tm, tn_out), lambda i, j: (i, j))
    (out,) = _matmul([y_att.reshape(m, ATT_W), y_rwkv.reshape(m, RWKV_W)], p["w_out"], col_block_off=0,
                     n_cols=d, kernel=_mm_residual_kernel, out_dtypes=(F32,), extra=(pr["x2"],),
                     extra_specs=(x_spec,), tm=tm, tn=tn_out)
    return (out.reshape(b, t, d), pr["k32"].reshape(b, t, N_ATT_HEADS, DV),
            pr["v32"].reshape(b, t, N_ATT_HEADS, DV), shift_new)


def _layer_pair(xp, xs, pos_p, pos_s, shift_s, wkv_s, cache_k, cache_v, page_table, p, *,
                chunk_p=64, pairs_p=8, pairs_s=16, pages_per_step=8):
    pp = _project(xp, pos_p, p)
    ps = _project(xs, pos_s, p)
    b_s, t_s = ps["b"], ps["t"]
    rw_p = _rwkv_call_for(pp, jnp.zeros((pp["b"], SHIFT_W), F32), None, p, chunk=chunk_p, pairs=pairs_p)
    rw_s = _rwkv_call_for(ps, shift_s, wkv_s, p, chunk=t_s, pairs=pairs_s)
    at_s = _sample_attn_call(ps["q16"].reshape(b_s, t_s, ATT_W), ps["k16"].reshape(b_s, t_s, ATT_W),
                             ps["v16"].reshape(b_s, t_s, ATT_W), ps["gates"].reshape(b_s, t_s, -1),
                             cache_k, cache_v, page_table, p["lam"], p["subln_g"], layer=p["layer"],
                             out_scale=p["out_scale"], pages_per_step=pages_per_step)
    if math.prod(rw_p["grid"]) == math.prod(at_s["grid"]):
        y_rwkv_p, wkv_p, y_att_s = _rwkv_with_sample_attention(rw_p, at_s, page_table)
    else:
        y_rwkv_p, wkv_p = _rwkv(rw_p)
        y_att_s = _sample_attention(page_table, at_s)
    y_att_p = _prompt_attention(pp["q16"], pp["k16"], pp["v16"], pp["gates"], p["lam"], p["subln_g"],
                                batch=pp["b"], seq=pp["t"], out_scale=p["out_scale"])
    y_rwkv_s, wkv_s_new = _rwkv(rw_s)
    return (*_finish_layer(pp, y_att_p, y_rwkv_p, p), wkv_p), (*_finish_layer(ps, y_att_s, y_rwkv_s, p), wkv_s_new)


def kernel(x_prompt, x_sample, cache_k, cache_v, state_shift, state_wkv, page_table, ln_g, w_in, q_norm_g,
           k_norm_g, lambda_q1, lambda_k1, lambda_q2, lambda_k2, subln_g, time_mix, w0, w2, a0, a2, k_k,
           k_a, r_k, lnx_g, lnx_b, w_out):
    depth = w_in.shape[0]
    s_p, s_s = x_prompt.shape[1], x_sample.shape[1]
    past_len = page_table.shape[1] * PAGE_SIZE
    pos_p = jnp.arange(s_p)
    pos_s = past_len + jnp.arange(s_s)
    yp, ys = x_prompt, x_sample
    outs = [[] for _ in range(8)]
    for l in range(depth):
        p = _prep_layer_params(l, ln_g, w_in, q_norm_g, k_norm_g, lambda_q1, lambda_k1, lambda_q2,
                               lambda_k2, subln_g, time_mix, w0, w2, a0, a2, k_k, k_a, r_k, lnx_g, lnx_b,
                               w_out)
        (yp, kp, vp, shp, wkvp), (ys, ks, vs, shs, wkvs) = _layer_pair(
            yp, ys, pos_p, pos_s, state_shift[l], state_wkv[l], cache_k, cache_v, page_table, p)
        for lst, val in zip(outs, (kp, vp, shp, wkvp, ks, vs, shs, wkvs)):
            lst.append(val)
    stacked = [jnp.stack(lst, 0) for lst in outs]
    return (yp, ys, *stacked)
```

```python
import functools
import math

import jax
import jax.numpy as jnp
from jax import lax
from jax.experimental import pallas as pl
from jax.experimental.pallas import tpu as pltpu

F32 = jnp.float32
BF16 = jnp.bfloat16

D_MODEL = 4096
PAGE_SIZE = 128
ATT_W = 2048
RWKV_W = 2048
DV = 128
N_ATT_HEADS = ATT_W // DV
DQK = DV // 2
ROT_DIM = DQK // 4
ROPE_THETA = 500000.0
RWKV_HEAD = 64
N_RWKV_HEADS = RWKV_W // RWKV_HEAD
N_RWKV_PAIRS = N_RWKV_HEADS // 2
LORA = 64
SHIFT_W = 3 * RWKV_W + 2 * LORA
OFF_K = ATT_W
OFF_V = 2 * ATT_W
OFF_SHIFT = 3 * ATT_W
OFF_GA = OFF_SHIFT + SHIFT_W
OFF_GR = OFF_GA + ATT_W
IN_W = OFF_GR + RWKV_W
NORM_EPS = 1e-6
SUBLN_EPS = 1e-5
LNX_EPS = 64e-5

LANES = 128
VMEM_LIMIT_BYTES = 48 * 1024 * 1024
FUSED_VMEM_LIMIT_BYTES = 56 * 1024 * 1024
NEG_BIG = -1e30


def _cparams(sem, vmem_limit_bytes=VMEM_LIMIT_BYTES):
    return pltpu.CompilerParams(dimension_semantics=sem, vmem_limit_bytes=vmem_limit_bytes)


def _run(staged_body):
    for _ in staged_body:
        pass


def _interleave(*staged_bodies):
    active, finals = list(staged_bodies), []
    while active:
        for g in list(active):
            tag = next(g, "done")
            if tag == "done":
                active.remove(g)
            elif tag == "final":
                active.remove(g)
                finals.append(g)
    for g in finals:
        _run(g)


def _split_dot(x, w2_bf16):
    hi = x.astype(BF16)
    lo = (x - hi.astype(F32)).astype(BF16)
    return jnp.dot(jnp.concatenate([hi, lo], axis=1), w2_bf16, preferred_element_type=F32)


def _segment_sums(xs, w2_bf16):
    rows = xs[0].shape[0]
    out = _split_dot(jnp.concatenate(xs, axis=0), w2_bf16)
    return [out[i * rows:(i + 1) * rows] for i in range(len(xs))]


def _half_ones():
    r = (lax.broadcasted_iota(jnp.int32, (2 * LANES, LANES), 0) % LANES) // RWKV_HEAD
    c = lax.broadcasted_iota(jnp.int32, (2 * LANES, LANES), 1) // RWKV_HEAD
    return jnp.where(r == c, 1.0, 0.0).astype(BF16)


def _rmsnorm_kernel(x_ref, g_ref, o_ref):
    x = x_ref[...]
    ms = jnp.mean(x * x, axis=-1, keepdims=True)
    o_ref[...] = (x * lax.rsqrt(ms + NORM_EPS) * g_ref[...]).astype(BF16)


def _rmsnorm(x2, g, tm=512):
    m, d = x2.shape
    tm = min(tm, m)
    return pl.pallas_call(
        _rmsnorm_kernel,
        grid=(m // tm,),
        in_specs=[pl.BlockSpec((tm, d), lambda i: (i, 0)),
                  pl.BlockSpec((1, d), lambda i: (0, 0))],
        out_specs=pl.BlockSpec((tm, d), lambda i: (i, 0)),
        out_shape=jax.ShapeDtypeStruct((m, d), BF16),
        compiler_params=_cparams(("parallel",)),
        name="rmsnorm",
    )(x2, g.reshape(1, d))


def _mm_full_k(lhs_refs, w_ref):
    acc, off = None, 0
    for l_ref in lhs_refs:
        k = l_ref.shape[1]
        part = jnp.dot(l_ref[...], w_ref[off:off + k, :], preferred_element_type=F32)
        acc = part if acc is None else acc + part
        off += k
    return acc


def _mm_plain_kernel(h_ref, w_ref, o_ref):
    o_ref[...] = _mm_full_k([h_ref], w_ref).astype(o_ref.dtype)


def _mm_dual_kernel(h_ref, w_ref, o32_ref, o16_ref):
    a = _mm_full_k([h_ref], w_ref)
    o16_ref[...] = a.astype(BF16)
    for hh in range(a.shape[1] // LANES):
        o32_ref[:, hh, :] = a[:, hh * LANES:(hh + 1) * LANES]


def _mm_silu_kernel(h_ref, w_ref, o_ref):
    a = _mm_full_k([h_ref], w_ref)
    o_ref[...] = (a * jax.nn.sigmoid(a)).astype(o_ref.dtype)


def _mm_residual_kernel(ya_ref, yr_ref, w_ref, x_ref, o_ref):
    o_ref[...] = x_ref[...] + _mm_full_k([ya_ref, yr_ref], w_ref)


def _mm_qk_kernel(h_ref, w_ref, g_ref, cos_ref, s1_ref, s2_ref, *rest, scale, rows_per_chunk):
    outs, acc_ref = rest[:-1], rest[-1]
    acc_ref[...] = _mm_full_k([h_ref], w_ref)
    tm, tn = acc_ref.shape
    ones = _half_ones()
    gain = g_ref[...]

    def chunk(r, carry):
        rows = pl.ds(pl.multiple_of(r * rows_per_chunk, rows_per_chunk), rows_per_chunk)
        c = cos_ref[rows, :]
        s1 = s1_ref[rows, :]
        s2 = s2_ref[rows, :]
        for hh in range(tn // LANES):
            cols = slice(hh * LANES, (hh + 1) * LANES)
            x = acc_ref[rows, cols]
            ss = _split_dot(x * x, ones)
            y = x * lax.rsqrt(ss * (1.0 / DQK) + NORM_EPS) * gain
            y = y * c + pltpu.roll(y, LANES - ROT_DIM // 2, 1) * s1 + pltpu.roll(y, ROT_DIM // 2, 1) * s2
            if scale != 1.0:
                y = y * scale
            for o_ref in outs:
                if len(o_ref.shape) == 3:
                    o_ref[rows, hh, :] = y.astype(o_ref.dtype)
                else:
                    o_ref[rows, cols] = y.astype(o_ref.dtype)
        return carry

    lax.fori_loop(0, tm // rows_per_chunk, chunk, 0)


def _matmul(lhs, w, *, col_block_off, n_cols, kernel, out_dtypes, extra=(), extra_specs=(),
            scratch=False, tm=1024, tn=512):
    m = lhs[0].shape[0]
    tm = min(tm, m)
    tn = min(tn, n_cols)
    assert m % tm == 0 and n_cols % tn == 0 and sum(l.shape[1] for l in lhs) == w.shape[0]
    assert (col_block_off * LANES) % tn == 0
    joff = col_block_off * LANES // tn
    in_specs = [pl.BlockSpec((tm, l.shape[1]), lambda i, j: (i, 0)) for l in lhs]
    in_specs.append(pl.BlockSpec((w.shape[0], tn), lambda i, j: (0, j + joff)))
    in_specs += list(extra_specs)
    out_specs, out_shape = [], []
    for dt in out_dtypes:
        if dt == "f32_head_major":
            out_specs.append(pl.BlockSpec((tm, tn // LANES, LANES), lambda i, j: (i, j, 0)))
            out_shape.append(jax.ShapeDtypeStruct((m, n_cols // LANES, LANES), F32))
        else:
            out_specs.append(pl.BlockSpec((tm, tn), lambda i, j: (i, j)))
            out_shape.append(jax.ShapeDtypeStruct((m, n_cols), dt))
    return pl.pallas_call(
        kernel,
        grid=(m // tm, n_cols // tn),
        in_specs=in_specs,
        out_specs=out_specs,
        out_shape=out_shape,
        scratch_shapes=[pltpu.VMEM((tm, tn), F32)] if scratch else [],
        compiler_params=_cparams(("parallel", "arbitrary")),
        name="proj_matmul",
    )(*lhs, w, *extra)


def _rope_tables(pos):
    half = ROT_DIM // 2
    inv_freq = 1.0 / (ROPE_THETA ** (jnp.arange(0, ROT_DIM, 2, dtype=F32) / ROT_DIM))
    ang = pos.astype(F32)[:, None] * inv_freq[None, :]
    cos = jnp.cos(ang)
    sin = jnp.sin(ang)
    t = pos.shape[0]
    one = jnp.ones((t, DQK - ROT_DIM), F32)
    zero = jnp.zeros((t, DQK - ROT_DIM), F32)
    zh = jnp.zeros((t, half), F32)
    c64 = jnp.concatenate([cos, cos, one], axis=1)
    s1_64 = jnp.concatenate([-sin, zh, zero], axis=1)
    s2_64 = jnp.concatenate([zh, sin, zero], axis=1)
    tile2 = lambda z: jnp.concatenate([z, z], axis=1)
    return tile2(c64), tile2(s1_64), tile2(s2_64)


def _stack_maps(q):
    lane = lax.broadcasted_iota(jnp.int32, q.shape, 1)
    z = jnp.zeros_like(q)
    return jnp.concatenate([jnp.where(lane < DQK, q, z), jnp.where(lane >= DQK, q, z)], axis=0)


def _online_softmax_steps(ss, vs, ms, ls, accs):
    def bcast(m, n):
        return jnp.tile(m, (1, n // LANES)) if n >= LANES else m[:, :n]

    m_news = [jnp.maximum(m, jnp.max(s, axis=-1, keepdims=True)) for s, m in zip(ss, ms)]
    alphas = [jnp.exp2(m - mn) for m, mn in zip(ms, m_news)]
    ps = [jnp.exp2(s - bcast(mn, s.shape[1])) for s, mn in zip(ss, m_news)]
    l_news = [a * l + jnp.sum(p, axis=-1, keepdims=True) for a, l, p in zip(alphas, ls, ps)]
    pvs = [jnp.dot(p.astype(BF16), v, preferred_element_type=F32) for p, v in zip(ps, vs)]
    acc_news = [a * acc + pv for a, acc, pv in zip(alphas, accs, pvs)]
    return m_news, l_news, acc_news


def _attn_finish(acc, l, lam, subln_g, gate, out_scale):
    t = acc.shape[0] // 2
    o = acc[:t] / l[:t] - lam * (acc[t:] / l[t:])
    ms = jnp.mean(o * o, axis=-1, keepdims=True)
    o = o * lax.rsqrt(ms + SUBLN_EPS) * subln_g * out_scale
    return (o * gate.astype(F32)).astype(BF16)


def _prompt_attn_kernel(lam_ref, q_ref, k_ref, v_ref, g_ref, sg_ref, o_ref,
                        q12_ref, m_ref, l_ref, acc_ref, *, tq, heads, out_scale):
    qi = pl.program_id(2)
    for hp in range(heads):
        q12_ref[hp] = _stack_maps(q_ref[:, hp * DV:(hp + 1) * DV])
    m_ref[...] = jnp.full_like(m_ref, NEG_BIG)
    l_ref[...] = jnp.zeros_like(l_ref)
    acc_ref[...] = jnp.zeros_like(acc_ref)

    def tile(kj, masked):
        rows = pl.ds(pl.multiple_of(kj * tq, tq), tq)
        hcols = [slice(hp * DV, (hp + 1) * DV) for hp in range(heads)]
        ss = [lax.dot_general(q12_ref[hp], k_ref[rows, hcols[hp]], (((1,), (1,)), ((), ())),
                              preferred_element_type=F32) for hp in range(heads)]
        if masked:
            r = lax.broadcasted_iota(jnp.int32, ss[0].shape, 0) % tq
            c = lax.broadcasted_iota(jnp.int32, ss[0].shape, 1)
            ss = [jnp.where(c <= r, s, NEG_BIG) for s in ss]
        ms, ls, accs = _online_softmax_steps(
            ss, [v_ref[rows, hcols[hp]] for hp in range(heads)],
            [m_ref[hp] for hp in range(heads)], [l_ref[hp] for hp in range(heads)],
            [acc_ref[hp] for hp in range(heads)])
        for hp in range(heads):
            m_ref[hp] = ms[hp]
            l_ref[hp] = ls[hp]
            acc_ref[hp] = accs[hp]

    def body(kj, carry):
        tile(kj, False)
        return carry

    lax.fori_loop(0, qi, body, 0)
    tile(qi, True)
    lam = lam_ref[0, 0]
    for hp in range(heads):
        cols = slice(hp * DV, (hp + 1) * DV)
        o_ref[:, cols] = _attn_finish(acc_ref[hp], l_ref[hp], lam, sg_ref[...], g_ref[:, cols], out_scale)


def _prompt_attention(q, k16, v16, gates, lam, subln_g, *, batch, seq, out_scale, tq=512, heads=2):
    tq = min(tq, seq)
    nq = seq // tq
    w = heads * DV
    kern = functools.partial(_prompt_attn_kernel, tq=tq, heads=heads, out_scale=out_scale)
    return pl.pallas_call(
        kern,
        grid=(batch, N_ATT_HEADS // heads, nq),
        in_specs=[
            pl.BlockSpec(memory_space=pltpu.SMEM),
            pl.BlockSpec((tq, w), lambda b, h, i: (b * nq + i, h)),
            pl.BlockSpec((seq, w), lambda b, h, i: (b, h)),
            pl.BlockSpec((seq, w), lambda b, h, i: (b, h)),
            pl.BlockSpec((tq, w), lambda b, h, i: (b * nq + i, h)),
            pl.BlockSpec((1, DV), lambda b, h, i: (0, 0)),
        ],
        out_specs=pl.BlockSpec((tq, w), lambda b, h, i: (b * nq + i, h)),
        out_shape=jax.ShapeDtypeStruct((batch * seq, ATT_W), BF16),
        scratch_shapes=[pltpu.VMEM((heads, 2 * tq, DV), BF16),
                        pltpu.VMEM((heads, 2 * tq, LANES), F32),
                        pltpu.VMEM((heads, 2 * tq, LANES), F32),
                        pltpu.VMEM((heads, 2 * tq, DV), F32)],
        compiler_params=_cparams(("parallel", "parallel", "arbitrary")),
        name="prompt_attention",
    )(lam, q, k16, v16, gates, subln_g.reshape(1, DV))


QUADS = 4
HEADS_PER_QUAD = N_ATT_HEADS // QUADS


def _sample_attn_body(j, n_steps, lam_ref, q_ref, refs, *, t_new, pages_per_step, out_scale):
    kc_refs = refs[:pages_per_step]
    vc_refs = refs[pages_per_step:2 * pages_per_step]
    kn_ref, vn_ref, g_ref, sg_ref, o_ref, q12_ref, m_ref, l_ref, acc_ref = refs[2 * pages_per_step:]
    rows = 2 * t_new
    qrows = HEADS_PER_QUAD * rows
    ncols = PAGE_SIZE * HEADS_PER_QUAD

    @pl.when(j == 0)
    def _():
        for h in range(N_ATT_HEADS):
            c, i = h % QUADS, h // QUADS
            q12_ref[c, i * rows:(i + 1) * rows, :] = _stack_maps(q_ref[0, :, h * DV:(h + 1) * DV])
        m_ref[...] = jnp.full_like(m_ref, NEG_BIG)
        l_ref[...] = jnp.zeros_like(l_ref)
        acc_ref[...] = jnp.zeros_like(acc_ref)

    yield "init"
    own = (lax.broadcasted_iota(jnp.int32, (qrows, ncols), 0) // rows
           == lax.broadcasted_iota(jnp.int32, (qrows, ncols), 1) % HEADS_PER_QUAD)
    nt = (((1,), (1,)), ((), ()))
    quads = range(QUADS)
    ms = [m_ref[c] for c in quads]
    ls = [l_ref[c] for c in quads]
    accs = [acc_ref[c] for c in quads]
    for u in range(pages_per_step):
        scores = [jnp.where(own, lax.dot_general(q12_ref[c], kc_refs[u][pl.ds(c, ncols, stride=QUADS), :].astype(BF16),
                                                 nt, preferred_element_type=F32), NEG_BIG) for c in quads]
        yield "scores"
        vals = [vc_refs[u][pl.ds(c, ncols, stride=QUADS), :].astype(BF16) for c in quads]
        ms, ls, accs = _online_softmax_steps(scores, vals, ms, ls, accs)
        yield "softmax"
    for c in quads:
        m_ref[c] = ms[c]
        l_ref[c] = ls[c]
        acc_ref[c] = accs[c]

    yield "final"

    @pl.when(j == n_steps - 1)
    def _():
        r = lax.broadcasted_iota(jnp.int32, (rows, t_new), 0) % t_new
        cc = lax.broadcasted_iota(jnp.int32, (rows, t_new), 1)
        causal = cc <= r
        lam = lam_ref[0, 0]
        for h in range(N_ATT_HEADS):
            c, i = h % QUADS, h // QUADS
            sl = slice(i * rows, (i + 1) * rows)
            cols = slice(h * DV, (h + 1) * DV)
            s = lax.dot_general(q12_ref[c, sl, :], kn_ref[0, :, cols], (((1,), (1,)), ((), ())),
                                preferred_element_type=F32)
            s = jnp.where(causal, s, NEG_BIG)
            _, (l,), (acc,) = _online_softmax_steps([s], [vn_ref[0, :, cols]], [m_ref[c, sl, :]],
                                                    [l_ref[c, sl, :]], [acc_ref[c, sl, :]])
            o_ref[0, :, cols] = _attn_finish(acc, l, lam, sg_ref[...], g_ref[0, :, cols], out_scale)


def _sample_attn_kernel(pt_ref, lam_ref, q_ref, *refs, **kw):
    _run(_sample_attn_body(pl.program_id(1), pl.num_programs(1), lam_ref, q_ref, refs, **kw))


def _sample_attn_call(q, k16, v16, gates, cache_k, cache_v, page_table, lam, subln_g, *, layer, out_scale,
                      pages_per_step=8):
    b, t_new, _ = q.shape
    n_pages = page_table.shape[1]
    n_pool = cache_k.shape[1]
    pps = math.gcd(pages_per_step, n_pages)
    page_rows = PAGE_SIZE * N_ATT_HEADS
    kc2 = cache_k.reshape(-1, DV)
    vc2 = cache_v.reshape(-1, DV)
    tok_spec = pl.BlockSpec((1, t_new, ATT_W), lambda i, j, pt: (i, 0, 0))

    def page_spec(u):
        return pl.BlockSpec((page_rows, DV), lambda i, j, pt: (layer * n_pool + pt[i, j * pps + u], 0))

    page_specs = [page_spec(u) for u in range(pps)]
    qrows = HEADS_PER_QUAD * 2 * t_new
    return dict(
        grid=(b, n_pages // pps),
        kw=dict(t_new=t_new, pages_per_step=pps, out_scale=out_scale),
        in_specs=[pl.BlockSpec(memory_space=pltpu.SMEM), tok_spec] + page_specs + page_specs
                 + [tok_spec, tok_spec, tok_spec, pl.BlockSpec((1, DV), lambda i, j, pt: (0, 0))],
        args=[lam, q] + [kc2] * pps + [vc2] * pps + [k16, v16, gates, subln_g.reshape(1, DV)],
        out_specs=[tok_spec],
        out_shape=[jax.ShapeDtypeStruct((b, t_new, ATT_W), BF16)],
        scratch=[pltpu.VMEM((QUADS, qrows, DV), BF16),
                 pltpu.VMEM((QUADS, qrows, LANES), F32),
                 pltpu.VMEM((QUADS, qrows, LANES), F32),
                 pltpu.VMEM((QUADS, qrows, DV), F32)],
    )


def _sample_attention(page_table, call):
    grid_spec = pltpu.PrefetchScalarGridSpec(
        num_scalar_prefetch=1, grid=call["grid"], in_specs=call["in_specs"], out_specs=call["out_specs"],
        scratch_shapes=call["scratch"])
    return pl.pallas_call(
        functools.partial(_sample_attn_kernel, **call["kw"]),
        grid_spec=grid_spec,
        out_shape=call["out_shape"],
        compiler_params=_cparams(("parallel", "arbitrary")),
        name="sample_attention",
    )(page_table, *call["args"])[0]


def _stack_heads(x):
    lane = lax.broadcasted_iota(jnp.int32, x.shape, 1)
    z = jnp.zeros_like(x)
    return jnp.concatenate([jnp.where(lane < RWKV_HEAD, x, z), jnp.where(lane >= RWKV_HEAD, x, z)], axis=0)


def _each(f, *lists):
    return [f(*args) for args in zip(*lists)]


def _mm(a, b):
    return jnp.dot(a, b, preferred_element_type=F32)


def _rwkv_chunk(rs, krs, vs, lo, wls, w0s, a0s, k_ks, k_as, r_ks, lnx_gs, lnx_bs, gates, h0s, consts):
    ones, tril, strict, incl, eye = consts
    C = lo.shape[0]
    n2 = 2 * C
    lane = lax.broadcasted_iota(jnp.int32, (C, LANES), 1)
    lo_act = jnp.where(lane < LORA, jnp.tanh(lo), lo).astype(BF16)
    pres = [_mm(lo_act, wl) for wl in wls]
    logws = _each(lambda pre, w0: -math.exp(-0.5) * jax.nn.sigmoid(w0 + pre[:, :LANES]), pres, w0s)
    a_sigs = _each(lambda pre, a0: jax.nn.sigmoid(a0 + pre[:, LANES:]), pres, a0s)
    yield "stage"

    kks = _each(lambda kr, k_k: kr * k_k, krs, k_ks)
    kk_ss = _segment_sums([kk * kk for kk in kks], ones)
    kks = _each(lambda kk, ss: kk / jnp.maximum(jnp.sqrt(ss), 1e-12), kks, kk_ss)
    k_hs = _each(lambda kr, a_sig, k_a: kr * (1.0 + (a_sig - 1.0) * k_a), krs, a_sigs, k_as)
    rk_ss = _segment_sums([r * k_h * r_k for r, k_h, r_k in zip(rs, k_hs, r_ks)], ones)
    bonuses = _each(lambda ss, v: ss * v, rk_ss, vs)
    yield "stage"

    def split3(x):
        l1 = x.astype(BF16)
        rem = x - l1.astype(F32)
        l2 = rem.astype(BF16)
        return jnp.concatenate([l1, l2, (rem - l2.astype(F32)).astype(BF16)], axis=0)

    cums = [_mm(tril, split3(logw)) for logw in logws]
    yield "stage"
    g_incls = [jnp.exp(cum) for cum in cums]
    g_excls = _each(lambda cum, logw: jnp.exp(cum - logw), cums, logws)
    g_invs = [jnp.exp(-cum) for cum in cums]
    g_lasts = [g[C - 1:C, :] for g in g_incls]

    a_ts = _each(lambda kk, g: _stack_heads(-kk * g).astype(BF16), kks, g_excls)
    r_ts = _each(lambda r, g: _stack_heads(r * g).astype(BF16), rs, g_incls)
    v_sts = [_stack_heads(v).astype(BF16) for v in vs]
    bks = _each(lambda kk, a_sig, k_h, g: jnp.concatenate([_stack_heads(kk * a_sig * g), _stack_heads(k_h * g)],
                                                          axis=0), kks, a_sigs, k_hs, g_invs)
    gmats = _each(lambda a_t, r_t, bk: lax.dot_general(jnp.concatenate([a_t, r_t], axis=0), bk.astype(BF16),
                                                       (((1,), (1,)), ((), ())), preferred_element_type=F32),
                  a_ts, r_ts, bks)
    yield "stage"
    a_abs = [jnp.where(strict, g[:n2, :n2], 0.0) for g in gmats]
    a_aks = [jnp.where(strict, g[:n2, n2:], 0.0).astype(BF16) for g in gmats]
    a_rbs = [jnp.where(incl, g[n2:, :n2], 0.0).astype(BF16) for g in gmats]
    a_rks = [jnp.where(incl, g[n2:, n2:], 0.0).astype(BF16) for g in gmats]
    yield "stage"

    eye2 = jnp.where(lax.broadcasted_iota(jnp.int32, (n2, n2), 0)
                     == lax.broadcasted_iota(jnp.int32, (n2, n2), 1), 1.0, 0.0)
    n_factors = int(math.log2(C))
    pows = [a_abs]
    pows16 = [[a.astype(BF16) for a in a_abs]]
    for _ in range(n_factors - 1):
        pows.append([_mm(pw, pw) for pw in pows16[-1]])
        pows16.append([x.astype(BF16) for x in pows[-1]])
        yield "stage"
    prods = [_each(lambda x, y, x16, y16: eye2 + x + y + _mm(x16, y16), pows[i], pows[i + 1], pows16[i], pows16[i + 1])
             for i in range(0, n_factors - 1, 2)]
    if n_factors % 2:
        prods.append([eye2 + x for x in pows[-1]])
    yield "stage"
    while len(prods) > 1:
        nxt = [_each(lambda a, b: _mm(a.astype(BF16), b.astype(BF16)), prods[i], prods[i + 1])
               for i in range(0, len(prods) - 1, 2)]
        prods = nxt + ([prods[-1]] if len(prods) % 2 else [])
        yield "stage"
    t_invs = prods[0]

    h16s = [h0.astype(BF16) for h0 in h0s]
    if n2 % LANES == 0:
        cat = jnp.concatenate
        xs = _each(lambda a_t, h16, a_ak, v_st: _mm(cat([a_t, a_ak], axis=1), cat([h16, v_st], axis=0)),
                   a_ts, h16s, a_aks, v_sts)
    else:
        xs = _each(lambda a_t, h16, a_ak, v_st: _mm(a_t, h16) + _mm(a_ak, v_st), a_ts, h16s, a_aks, v_sts)
    u16s = _each(lambda t, x: _mm(t.astype(BF16), x.astype(BF16)).astype(BF16), t_invs, xs)
    yield "stage"
    if n2 % LANES == 0:
        o_sts = _each(lambda r_t, h16, a_rb, u16, a_rk, v_st:
                      _mm(cat([r_t, a_rb, a_rk], axis=1), cat([h16, u16, v_st], axis=0)),
                      r_ts, h16s, a_rbs, u16s, a_rks, v_sts)
    else:
        o_sts = _each(lambda r_t, h16, a_rb, u16, a_rk, v_st: _mm(r_t, h16) + _mm(a_rb, u16) + _mm(a_rk, v_st),
                      r_ts, h16s, a_rbs, u16s, a_rks, v_sts)
    os_ = [o_st[:C] + o_st[C:] for o_st in o_sts]
    yield "stage"

    bk_ts = _each(lambda bk, g_last: (bk * g_last).T.astype(BF16), bks, g_lasts)
    g_cols = [jnp.sum(jnp.where(eye, jnp.broadcast_to(g_last, (LANES, LANES)), 0.0), axis=1, keepdims=True)
              for g_last in g_lasts]
    h_news = _each(lambda g_col, h0, bk_t, u16, v_st: g_col * h0 + _mm(bk_t, jnp.concatenate([u16, v_st], axis=0)),
                   g_cols, h0s, bk_ts, u16s, v_sts)
    yield "stage"

    mus = [z * (1.0 / RWKV_HEAD) for z in _segment_sums(os_, ones)]
    ds = _each(lambda o, mu: o - mu, os_, mus)
    vars_ = [z * (1.0 / RWKV_HEAD) for z in _segment_sums([d * d for d in ds], ones)]
    ys = _each(lambda d, var, g, b, bonus, gate:
               (((d * lax.rsqrt(var + LNX_EPS) * g + b) + bonus) * gate.astype(F32)).astype(BF16),
               ds, vars_, lnx_gs, lnx_bs, bonuses, gates)
    return ys, h_news


def _rwkv_body(c, n_chunks, refs, *, chunk, pairs, use_s0):
    (r_ref, k_ref, v_ref, lo_ref, pr_ref, pk_ref, pv_ref, plo_ref,
     tmr_ref, tmk_ref, tmv_ref, tmlo_ref, wl_ref, w0_ref, a0_ref, kk_ref, ka_ref, rk_ref,
     lg_ref, lb_ref, gate_ref, s0_ref, y_ref, sout_ref,
     h_ref, cr_ref, ck_ref, cv_ref, clo_ref) = refs
    C = chunk
    n2 = 2 * C

    @pl.when(c == 0)
    def _():
        cr_ref[...] = pr_ref[0]
        ck_ref[...] = pk_ref[0]
        cv_ref[...] = pv_ref[0]
        clo_ref[...] = plo_ref[0]
        if use_s0:
            z = jnp.zeros((RWKV_HEAD, RWKV_HEAD), F32)
            for u in range(pairs):
                st = jnp.concatenate([jnp.concatenate([s0_ref[0, 2 * u], z], axis=1),
                                      jnp.concatenate([z, s0_ref[0, 2 * u + 1]], axis=1)], axis=0)
                h_ref[u] = st.T
        else:
            h_ref[...] = jnp.zeros_like(h_ref)

    yield "init"
    ri = lax.broadcasted_iota(jnp.int32, (n2, n2), 0)
    ci = lax.broadcasted_iota(jnp.int32, (n2, n2), 1)
    tril = (lax.broadcasted_iota(jnp.int32, (C, 3 * C), 1) % C <= lax.broadcasted_iota(jnp.int32, (C, 3 * C), 0))
    eye = (lax.broadcasted_iota(jnp.int32, (LANES, LANES), 0)
           == lax.broadcasted_iota(jnp.int32, (LANES, LANES), 1))
    consts = (_half_ones(), jnp.where(tril, 1.0, 0.0).astype(BF16), ri > ci, ri >= ci, eye)

    def token_shift(cur_ref3, carry_ref, tm_ref):
        cur = cur_ref3[0]
        row0 = lax.broadcasted_iota(jnp.int32, cur.shape, 0) == 0
        prev = jnp.where(row0, carry_ref[...], pltpu.roll(cur, 1, 0))
        carry_ref[...] = cur[C - 1:C, :]
        return cur + tm_ref[...] * (prev - cur)

    lo = token_shift(lo_ref, clo_ref, tmlo_ref)
    r_all = token_shift(r_ref, cr_ref, tmr_ref)
    k_all = token_shift(k_ref, ck_ref, tmk_ref)
    v_all = token_shift(v_ref, cv_ref, tmv_ref)
    col = lambda u: slice(u * LANES, (u + 1) * LANES)
    per_pair = lambda x: [x[:, col(u)] for u in range(pairs)]
    ys, h_news = yield from _rwkv_chunk(
        per_pair(r_all), per_pair(k_all), per_pair(v_all), lo, [wl_ref[u] for u in range(pairs)],
        per_pair(w0_ref[...]), per_pair(a0_ref[...]), per_pair(kk_ref[...]), per_pair(ka_ref[...]),
        per_pair(rk_ref[...]), per_pair(lg_ref[...]), per_pair(lb_ref[...]), per_pair(gate_ref[0]),
        [h_ref[u] for u in range(pairs)], consts)
    for u in range(pairs):
        y_ref[0, :, col(u)] = ys[u]
        h_ref[u] = h_news[u]

    yield "final"

    @pl.when(c == n_chunks - 1)
    def _():
        for u in range(pairs):
            st = h_news[u].T
            sout_ref[0, 2 * u] = st[:RWKV_HEAD, :RWKV_HEAD]
            sout_ref[0, 2 * u + 1] = st[RWKV_HEAD:, RWKV_HEAD:]


def _rwkv_kernel(*refs, **kw):
    _run(_rwkv_body(pl.program_id(2), pl.num_programs(2), refs, **kw))


def _rwkv_call(rkv, lora, gates, prevs, s0, p, *, chunk, pairs):
    b, t, _ = lora.shape
    n_chunks = t // chunk
    ng = N_RWKV_PAIRS // pairs
    w = pairs * LANES
    use_s0 = s0 is not None
    sblk = (1, 2 * pairs, RWKV_HEAD, RWKV_HEAD)
    if s0 is None:
        s0 = jnp.zeros(sblk, F32)
        s0_spec = pl.BlockSpec(sblk, lambda i, j, c: (0, 0, 0, 0))
    else:
        s0_spec = pl.BlockSpec(sblk, lambda i, j, c: (i, j, 0, 0))

    tok = pl.BlockSpec((1, chunk, w), lambda i, j, c: (i, c, j))
    prev = pl.BlockSpec((1, 1, w), lambda i, j, c: (i, 0, j))

    def par():
        return pl.BlockSpec((1, w), lambda i, j, c: (0, j))

    return dict(
        grid=(b, ng, n_chunks),
        kw=dict(chunk=chunk, pairs=pairs, use_s0=use_s0),
        in_specs=[
            tok, tok, tok,
            pl.BlockSpec((1, chunk, LANES), lambda i, j, c: (i, c, 0)),
            prev, prev, prev,
            pl.BlockSpec((1, 1, LANES), lambda i, j, c: (i, 0, 0)),
            par(), par(), par(), pl.BlockSpec((1, LANES), lambda i, j, c: (0, 0)),
            pl.BlockSpec((pairs, LANES, 2 * LANES), lambda i, j, c: (j, 0, 0)),
            par(), par(), par(), par(), par(), par(), par(),
            pl.BlockSpec((1, chunk, w), lambda i, j, c: (i, c, ng + j)),
            s0_spec,
        ],
        args=[*rkv, lora, *prevs, p["tm_r"], p["tm_k"], p["tm_v"], p["tm_lora"], p["w_lora"],
              p["w0"], p["a0"], p["k_k"], p["k_a"], p["r_k"], p["lnx_g"], p["lnx_b"], gates, s0],
        out_specs=[pl.BlockSpec((1, chunk, w), lambda i, j, c: (i, c, j)),
                   pl.BlockSpec(sblk, lambda i, j, c: (i, j, 0, 0))],
        out_shape=[jax.ShapeDtypeStruct((b, t, RWKV_W), BF16),
                   jax.ShapeDtypeStruct((b, N_RWKV_HEADS, RWKV_HEAD, RWKV_HEAD), F32)],
        scratch=[pltpu.VMEM((pairs, LANES, LANES), F32),
                 pltpu.VMEM((1, w), F32), pltpu.VMEM((1, w), F32),
                 pltpu.VMEM((1, w), F32), pltpu.VMEM((1, LANES), F32)],
    )


def _rwkv(call):
    return pl.pallas_call(
        functools.partial(_rwkv_kernel, **call["kw"]),
        grid=call["grid"],
        in_specs=call["in_specs"],
        out_specs=call["out_specs"],
        out_shape=call["out_shape"],
        scratch_shapes=call["scratch"],
        compiler_params=_cparams(("parallel", "parallel", "arbitrary")),
        name="rwkv7_chunked",
    )(*call["args"])


def _remap(spec, decode):
    if spec.block_shape is None:
        return spec
    return pl.BlockSpec(spec.block_shape, lambda *a: spec.index_map(*decode(*a)))


def _fused_kernel(pt_ref, *refs, n_in, n_out, n_scr, rwkv_kw, attn_kw, n_chunks, attn_steps):
    (ri, ai), (ro, ao), (rs, as_) = n_in, n_out, n_scr
    r_in, a_in = refs[:ri], refs[ri:ri + ai]
    r_out, a_out = refs[ri + ai:ri + ai + ro], refs[ri + ai + ro:ri + ai + ro + ao]
    r_scr, a_scr = refs[ri + ai + ro + ao:ri + ai + ro + ao + rs], refs[ri + ai + ro + ao + rs:]
    s = pl.program_id(0)
    _interleave(
        _rwkv_body(s % n_chunks, n_chunks, (*r_in, *r_out, *r_scr), **rwkv_kw),
        _sample_attn_body(s % attn_steps, attn_steps, a_in[0], a_in[1], (*a_in[2:], *a_out, *a_scr), **attn_kw))


def _rwkv_with_sample_attention(rwkv_call, attn_call, page_table):
    b_r, ng, nc = rwkv_call["grid"]
    b_a, nj = attn_call["grid"]
    n_steps = b_r * ng * nc
    assert n_steps == b_a * nj
    dec_r = lambda s, pt: (s // (ng * nc), (s // nc) % ng, s % nc)
    dec_a = lambda s, pt: (s // nj, s % nj, pt)
    in_specs = [_remap(sp, dec_r) for sp in rwkv_call["in_specs"]] + [_remap(sp, dec_a) for sp in attn_call["in_specs"]]
    out_specs = [_remap(sp, dec_r) for sp in rwkv_call["out_specs"]] + [_remap(sp, dec_a) for sp in attn_call["out_specs"]]
    kern = functools.partial(
        _fused_kernel,
        n_in=(len(rwkv_call["in_specs"]), len(attn_call["in_specs"])),
        n_out=(len(rwkv_call["out_specs"]), len(attn_call["out_specs"])),
        n_scr=(len(rwkv_call["scratch"]), len(attn_call["scratch"])),
        rwkv_kw=rwkv_call["kw"], attn_kw=attn_call["kw"], n_chunks=nc, attn_steps=nj)
    grid_spec = pltpu.PrefetchScalarGridSpec(
        num_scalar_prefetch=1, grid=(n_steps,), in_specs=in_specs, out_specs=out_specs,
        scratch_shapes=rwkv_call["scratch"] + attn_call["scratch"])
    y_rwkv, s_out, y_att = pl.pallas_call(
        kern,
        grid_spec=grid_spec,
        out_shape=rwkv_call["out_shape"] + attn_call["out_shape"],
        compiler_params=_cparams(("arbitrary",), FUSED_VMEM_LIMIT_BYTES),
        name="rwkv7_and_paged_attention",
    )(page_table, *rwkv_call["args"], *attn_call["args"])
    return y_rwkv, s_out, y_att


def _prep_layer_params(l, ln_g, w_in, q_norm_g, k_norm_g, lambda_q1, lambda_k1, lambda_q2, lambda_k2,
                       subln_g, time_mix, w0, w2, a0, a2, k_k, k_a, r_k, lnx_g, lnx_b, w_out):
    w = w_in[l]
    sh = OFF_SHIFT
    o1, o2, o3, o4 = RWKV_W, RWKV_W + LORA, 2 * RWKV_W + LORA, 3 * RWKV_W + LORA
    w_qkvr = w[:, :sh + o1].astype(BF16)
    w_rk = w[:, sh + o2:sh + o3].astype(BF16)
    w_rv = w[:, sh + o3:sh + o4].astype(BF16)
    w_gates = w[:, OFF_GA:].astype(BF16)
    w_lora_in = jnp.concatenate([w[:, sh + o1:sh + o2], w[:, sh + o4:sh + SHIFT_W]], axis=1).astype(BF16)
    tmix = time_mix[l]
    tm_lora = jnp.concatenate([tmix[o1:o2], tmix[o4:]]).reshape(1, 2 * LORA)
    w2p = w2[l].reshape(LORA, N_RWKV_PAIRS, LANES).transpose(1, 0, 2)
    a2p = a2[l].reshape(LORA, N_RWKV_PAIRS, LANES).transpose(1, 0, 2)
    z = jnp.zeros_like(w2p)
    w_lora = jnp.concatenate([jnp.concatenate([w2p, z], axis=2), jnp.concatenate([z, a2p], axis=2)],
                             axis=1).astype(BF16)
    lam_init = 0.8 - 0.6 * math.exp(-0.3 * l)
    lam = (jnp.exp(jnp.sum(lambda_q1[l] * lambda_k1[l])) - jnp.exp(jnp.sum(lambda_q2[l] * lambda_k2[l]))
           + lam_init).reshape(1, 1).astype(F32)
    row = lambda z_: z_.reshape(1, -1)
    return dict(
        layer=l, ln_g=ln_g[l], w_qkvr=w_qkvr, w_rk=w_rk, w_rv=w_rv, w_gates=w_gates, w_lora_in=w_lora_in,
        w_out=w_out[l].astype(BF16),
        q_gain=jnp.tile(q_norm_g[l], 2).reshape(1, DV), k_gain=jnp.tile(k_norm_g[l], 2).reshape(1, DV),
        lam=lam, out_scale=1.0 - lam_init, subln_g=subln_g[l],
        tm_r=row(tmix[:o1]), tm_k=row(tmix[o2:o3]), tm_v=row(tmix[o3:o4]), tm_lora=tm_lora, w_lora=w_lora,
        w0=row(w0[l]), a0=row(a0[l]), k_k=row(k_k[l]), k_a=row(k_a[l]), r_k=row(r_k[l]),
        lnx_g=row(lnx_g[l]), lnx_b=row(lnx_b[l]),
    )


def _project(x, pos, p):
    b, t, d = x.shape
    m = b * t
    x2 = x.reshape(m, d)
    h = _rmsnorm(x2, p["ln_g"])

    tm = min(1024, m)
    cos, s1, s2 = _rope_tables(pos)
    if t < tm:
        reps = tm // t
        cos, s1, s2 = (jnp.tile(z, (reps, 1)) for z in (cos, s1, s2))
    gain_spec = pl.BlockSpec((1, LANES), lambda i, j: (0, 0))
    blk = lambda cols: cols // LANES
    tm_hm, tn_hm = min(512, m), 1024

    def qk(col_off, gain, scale, dts):
        tm_, tn_ = (tm_hm, tn_hm) if "f32_head_major" in dts else (tm, 512)
        n_tab = cos.shape[0] // tm_
        tab_spec = pl.BlockSpec((tm_, LANES), lambda i, j: (i % n_tab, 0))
        kern = functools.partial(_mm_qk_kernel, scale=scale, rows_per_chunk=min(256, tm_))
        return _matmul([h], p["w_qkvr"], col_block_off=blk(col_off), n_cols=ATT_W, kernel=kern,
                       out_dtypes=dts, extra=(gain, cos, s1, s2),
                       extra_specs=(gain_spec, tab_spec, tab_spec, tab_spec), scratch=True, tm=tm_, tn=tn_)

    def plain(w, col_off, n_cols):
        return _matmul([h], w, col_block_off=blk(col_off), n_cols=n_cols, kernel=_mm_plain_kernel,
                       out_dtypes=(F32,), tm=tm)[0]

    (q16,) = qk(0, p["q_gain"], DQK ** -0.5 * math.log2(math.e), (BF16,))
    k32, k16 = qk(OFF_K, p["k_gain"], 1.0, ("f32_head_major", BF16))
    v32, v16 = _matmul([h], p["w_qkvr"], col_block_off=blk(OFF_V), n_cols=ATT_W,
                       kernel=_mm_dual_kernel, out_dtypes=("f32_head_major", BF16), tm=tm_hm, tn=tn_hm)
    rkv = [plain(p["w_qkvr"], OFF_SHIFT, RWKV_W).reshape(b, t, RWKV_W),
           plain(p["w_rk"], 0, RWKV_W).reshape(b, t, RWKV_W),
           plain(p["w_rv"], 0, RWKV_W).reshape(b, t, RWKV_W)]
    lora = plain(p["w_lora_in"], 0, 2 * LORA).reshape(b, t, 2 * LORA)
    (gates,) = _matmul([h], p["w_gates"], col_block_off=0, n_cols=ATT_W + RWKV_W,
                       kernel=_mm_silu_kernel, out_dtypes=(BF16,), tm=tm)
    return dict(b=b, t=t, m=m, tm=tm, x2=x2, q16=q16, k32=k32, k16=k16, v32=v32, v16=v16, rkv=rkv, lora=lora,
                gates=gates)


def _rwkv_call_for(pr, shift_prev, wkv_prev, p, *, chunk, pairs):
    b = pr["b"]
    o1, o2, o3, o4 = RWKV_W, RWKV_W + LORA, 2 * RWKV_W + LORA, 3 * RWKV_W + LORA
    prev_lora = jnp.concatenate([shift_prev[:, o1:o2], shift_prev[:, o4:]], axis=1)
    prevs = [z.reshape(b, 1, -1) for z in (shift_prev[:, :o1], shift_prev[:, o2:o3], shift_prev[:, o3:o4],
                                           prev_lora)]
    return _rwkv_call(pr["rkv"], pr["lora"], pr["gates"].reshape(b, pr["t"], -1), prevs, wkv_prev, p,
                      chunk=chunk, pairs=pairs)


def _finish_layer(pr, y_att, y_rwkv, p):
    b, t, m, tm = pr["b"], pr["t"], pr["m"], pr["tm"]
    d = pr["x2"].shape[1]
    last_lora = pr["lora"][:, -1]
    shift_new = jnp.concatenate([pr["rkv"][0][:, -1], last_lora[:, :LORA], pr["rkv"][1][:, -1],
                                 pr["rkv"][2][:, -1], last_lora[:, LORA:]], axis=1)
    tn_out = 512
    x_spec = pl.BlockSpec((tm, tn_out), lambda i, j: (i, j))
    (out,) = _matmul([y_att.reshape(m, ATT_W), y_rwkv.reshape(m, RWKV_W)], p["w_out"], col_block_off=0,
                     n_cols=d, kernel=_mm_residual_kernel, out_dtypes=(F32,), extra=(pr["x2"],),
                     extra_specs=(x_spec,), tm=tm, tn=tn_out)
    return (out.reshape(b, t, d), pr["k32"].reshape(b, t, N_ATT_HEADS, DV),
            pr["v32"].reshape(b, t, N_ATT_HEADS, DV), shift_new)


def _layer_pair(xp, xs, pos_p, pos_s, shift_s, wkv_s, cache_k, cache_v, page_table, p, *,
                chunk_p=64, pairs_p=8, pairs_s=16, pages_per_step=8):
    pp = _project(xp, pos_p, p)
    ps = _project(xs, pos_s, p)
    b_s, t_s = ps["b"], ps["t"]
    rw_p = _rwkv_call_for(pp, jnp.zeros((pp["b"], SHIFT_W), F32), None, p, chunk=chunk_p, pairs=pairs_p)
    rw_s = _rwkv_call_for(ps, shift_s, wkv_s, p, chunk=t_s, pairs=pairs_s)
    at_s = _sample_attn_call(ps["q16"].reshape(b_s, t_s, ATT_W), ps["k16"].reshape(b_s, t_s, ATT_W),
                             ps["v16"].reshape(b_s, t_s, ATT_W), ps["gates"].reshape(b_s, t_s, -1),
                             cache_k, cache_v, page_table, p["lam"], p["subln_g"], layer=p["layer"],
                             out_scale=p["out_scale"], pages_per_step=pages_per_step)
    if math.prod(rw_p["grid"]) == math.prod(at_s["grid"]):
        y_rwkv_p, wkv_p, y_att_s = _rwkv_with_sample_attention(rw_p, at_s, page_table)
    else:
        y_rwkv_p, wkv_p = _rwkv(rw_p)
        y_att_s = _sample_attention(page_table, at_s)
    y_att_p = _prompt_attention(pp["q16"], pp["k16"], pp["v16"], pp["gates"], p["lam"], p["subln_g"],
                                batch=pp["b"], seq=pp["t"], out_scale=p["out_scale"])
    y_rwkv_s, wkv_s_new = _rwkv(rw_s)
    return (*_finish_layer(pp, y_att_p, y_rwkv_p, p), wkv_p), (*_finish_layer(ps, y_att_s, y_rwkv_s, p), wkv_s_new)


def kernel(x_prompt, x_sample, cache_k, cache_v, state_shift, state_wkv, page_table, ln_g, w_in, q_norm_g,
           k_norm_g, lambda_q1, lambda_k1, lambda_q2, lambda_k2, subln_g, time_mix, w0, w2, a0, a2, k_k,
           k_a, r_k, lnx_g, lnx_b, w_out):
    depth = w_in.shape[0]
    s_p, s_s = x_prompt.shape[1], x_sample.shape[1]
    past_len = page_table.shape[1] * PAGE_SIZE
    pos_p = jnp.arange(s_p)
    pos_s = past_len + jnp.arange(s_s)
    yp, ys = x_prompt, x_sample
    outs = [[] for _ in range(8)]
    for l in range(depth):
        p = _prep_layer_params(l, ln_g, w_in, q_norm_g, k_norm_g, lambda_q1, lambda_k1, lambda_q2,
                               lambda_k2, subln_g, time_mix, w0, w2, a0, a2, k_k, k_a, r_k, lnx_g, lnx_b,
                               w_out)
        (yp, kp, vp, shp, wkvp), (ys, ks, vs, shs, wkvs) = _layer_pair(
            yp, ys, pos_p, pos_s, state_shift[l], state_wkv[l], cache_k, cache_v, page_table, p)
        for lst, val in zip(outs, (kp, vp, shp, wkvp, ks, vs, shs, wkvs)):
            lst.append(val)
    stacked = [jnp.stack(lst, 0) for lst in outs]
    return (yp, ys, *stacked)
```

```python
import functools
import math

import jax
import jax.numpy as jnp
from jax import lax
from jax.experimental import pallas as pl
from jax.experimental.pallas import tpu as pltpu

F32 = jnp.float32
BF16 = jnp.bfloat16

D_MODEL = 4096
PAGE_SIZE = 128
ATT_W = 2048
RWKV_W = 2048
DV = 128
N_ATT_HEADS = ATT_W // DV
DQK = DV // 2
ROT_DIM = DQK // 4
ROPE_THETA = 500000.0
RWKV_HEAD = 64
N_RWKV_HEADS = RWKV_W // RWKV_HEAD
N_RWKV_PAIRS = N_RWKV_HEADS // 2
LORA = 64
SHIFT_W = 3 * RWKV_W + 2 * LORA
OFF_K = ATT_W
OFF_V = 2 * ATT_W
OFF_SHIFT = 3 * ATT_W
OFF_GA = OFF_SHIFT + SHIFT_W
OFF_GR = OFF_GA + ATT_W
IN_W = OFF_GR + RWKV_W
NORM_EPS = 1e-6
SUBLN_EPS = 1e-5
LNX_EPS = 64e-5

LANES = 128
VMEM_LIMIT_BYTES = 48 * 1024 * 1024
FUSED_VMEM_LIMIT_BYTES = 56 * 1024 * 1024
NEG_BIG = -1e30


def _cparams(sem, vmem_limit_bytes=VMEM_LIMIT_BYTES):
    return pltpu.CompilerParams(dimension_semantics=sem, vmem_limit_bytes=vmem_limit_bytes)


def _run(staged_body):
    for _ in staged_body:
        pass


def _interleave(*staged_bodies):
    active, finals = list(staged_bodies), []
    while active:
        for g in list(active):
            tag = next(g, "done")
            if tag == "done":
                active.remove(g)
            elif tag == "final":
                active.remove(g)
                finals.append(g)
    for g in finals:
        _run(g)


def _split_dot(x, w2_bf16):
    hi = x.astype(BF16)
    lo = (x - hi.astype(F32)).astype(BF16)
    return jnp.dot(jnp.concatenate([hi, lo], axis=1), w2_bf16, preferred_element_type=F32)


def _segment_sums(xs, w2_bf16):
    rows = xs[0].shape[0]
    out = _split_dot(jnp.concatenate(xs, axis=0), w2_bf16)
    return [out[i * rows:(i + 1) * rows] for i in range(len(xs))]


def _half_ones():
    r = (lax.broadcasted_iota(jnp.int32, (2 * LANES, LANES), 0) % LANES) // RWKV_HEAD
    c = lax.broadcasted_iota(jnp.int32, (2 * LANES, LANES), 1) // RWKV_HEAD
    return jnp.where(r == c, 1.0, 0.0).astype(BF16)


def _rmsnorm_kernel(x_ref, g_ref, o_ref):
    rows_per_iter = 16

    def body(i, carry):
        rows = pl.ds(pl.multiple_of(i * rows_per_iter, rows_per_iter), rows_per_iter)
        x = x_ref[rows, :]
        ms = jnp.mean(x * x, axis=-1, keepdims=True)
        o_ref[rows, :] = (x * lax.rsqrt(ms + NORM_EPS) * g_ref[...]).astype(BF16)
        return carry

    n_iter = x_ref.shape[0] // rows_per_iter
    lax.fori_loop(0, n_iter, body, 0, unroll=math.gcd(n_iter, 8))


def _rmsnorm(x2, g, tm=512):
    m, d = x2.shape
    tm = min(tm, m)
    return pl.pallas_call(
        _rmsnorm_kernel,
        grid=(m // tm,),
        in_specs=[pl.BlockSpec((tm, d), lambda i: (i, 0)),
                  pl.BlockSpec((1, d), lambda i: (0, 0))],
        out_specs=pl.BlockSpec((tm, d), lambda i: (i, 0)),
        out_shape=jax.ShapeDtypeStruct((m, d), BF16),
        compiler_params=_cparams(("parallel",)),
        name="rmsnorm",
    )(x2, g.reshape(1, d))


def _mm_full_k(lhs_refs, w_ref):
    acc, off = None, 0
    for l_ref in lhs_refs:
        k = l_ref.shape[1]
        part = jnp.dot(l_ref[...], w_ref[off:off + k, :], preferred_element_type=F32)
        acc = part if acc is None else acc + part
        off += k
    return acc


def _mm_plain_kernel(h_ref, w_ref, o_ref):
    o_ref[...] = _mm_full_k([h_ref], w_ref).astype(o_ref.dtype)


def _mm_dual_kernel(h_ref, w_ref, o32_ref, o16_ref):
    a = _mm_full_k([h_ref], w_ref)
    o16_ref[...] = a.astype(BF16)
    for hh in range(a.shape[1] // LANES):
        o32_ref[:, hh, :] = a[:, hh * LANES:(hh + 1) * LANES]


def _mm_silu_kernel(h_ref, w_ref, o_ref):
    a = _mm_full_k([h_ref], w_ref)
    o_ref[...] = (a * jax.nn.sigmoid(a)).astype(o_ref.dtype)


def _mm_residual_kernel(ya_ref, yr_ref, w_ref, x_ref, o_ref):
    o_ref[...] = x_ref[...] + _mm_full_k([ya_ref, yr_ref], w_ref)


def _mm_qk_kernel(h_ref, w_ref, g_ref, cos_ref, s1_ref, s2_ref, *rest, scale, rows_per_chunk):
    outs, acc_ref = rest[:-1], rest[-1]
    acc_ref[...] = _mm_full_k([h_ref], w_ref)
    tm, tn = acc_ref.shape
    ones = _half_ones()
    gain = g_ref[...]

    def chunk(r, carry):
        rows = pl.ds(pl.multiple_of(r * rows_per_chunk, rows_per_chunk), rows_per_chunk)
        c = cos_ref[rows, :]
        s1 = s1_ref[rows, :]
        s2 = s2_ref[rows, :]
        for hh in range(tn // LANES):
            cols = slice(hh * LANES, (hh + 1) * LANES)
            x = acc_ref[rows, cols]
            ss = _split_dot(x * x, ones)
            y = x * lax.rsqrt(ss * (1.0 / DQK) + NORM_EPS) * gain
            y = y * c + pltpu.roll(y, LANES - ROT_DIM // 2, 1) * s1 + pltpu.roll(y, ROT_DIM // 2, 1) * s2
            if scale != 1.0:
                y = y * scale
            for o_ref in outs:
                if len(o_ref.shape) == 3:
                    o_ref[rows, hh, :] = y.astype(o_ref.dtype)
                else:
                    o_ref[rows, cols] = y.astype(o_ref.dtype)
        return carry

    lax.fori_loop(0, tm // rows_per_chunk, chunk, 0)


def _matmul(lhs, w, *, col_block_off, n_cols, kernel, out_dtypes, extra=(), extra_specs=(),
            scratch=False, tm=1024, tn=512):
    m = lhs[0].shape[0]
    tm = min(tm, m)
    tn = min(tn, n_cols)
    assert m % tm == 0 and n_cols % tn == 0 and sum(l.shape[1] for l in lhs) == w.shape[0]
    assert (col_block_off * LANES) % tn == 0
    joff = col_block_off * LANES // tn
    in_specs = [pl.BlockSpec((tm, l.shape[1]), lambda i, j: (i, 0)) for l in lhs]
    in_specs.append(pl.BlockSpec((w.shape[0], tn), lambda i, j: (0, j + joff)))
    in_specs += list(extra_specs)
    out_specs, out_shape = [], []
    for dt in out_dtypes:
        if dt == "f32_head_major":
            out_specs.append(pl.BlockSpec((tm, tn // LANES, LANES), lambda i, j: (i, j, 0)))
            out_shape.append(jax.ShapeDtypeStruct((m, n_cols // LANES, LANES), F32))
        else:
            out_specs.append(pl.BlockSpec((tm, tn), lambda i, j: (i, j)))
            out_shape.append(jax.ShapeDtypeStruct((m, n_cols), dt))
    return pl.pallas_call(
        kernel,
        grid=(m // tm, n_cols // tn),
        in_specs=in_specs,
        out_specs=out_specs,
        out_shape=out_shape,
        scratch_shapes=[pltpu.VMEM((tm, tn), F32)] if scratch else [],
        compiler_params=_cparams(("parallel", "arbitrary")),
        name="proj_matmul",
    )(*lhs, w, *extra)


def _rope_tables(pos):
    half = ROT_DIM // 2
    inv_freq = 1.0 / (ROPE_THETA ** (jnp.arange(0, ROT_DIM, 2, dtype=F32) / ROT_DIM))
    ang = pos.astype(F32)[:, None] * inv_freq[None, :]
    cos = jnp.cos(ang)
    sin = jnp.sin(ang)
    t = pos.shape[0]
    one = jnp.ones((t, DQK - ROT_DIM), F32)
    zero = jnp.zeros((t, DQK - ROT_DIM), F32)
    zh = jnp.zeros((t, half), F32)
    c64 = jnp.concatenate([cos, cos, one], axis=1)
    s1_64 = jnp.concatenate([-sin, zh, zero], axis=1)
    s2_64 = jnp.concatenate([zh, sin, zero], axis=1)
    tile2 = lambda z: jnp.concatenate([z, z], axis=1)
    return tile2(c64), tile2(s1_64), tile2(s2_64)


def _stack_maps(q):
    lane = lax.broadcasted_iota(jnp.int32, q.shape, 1)
    z = jnp.zeros_like(q)
    return jnp.concatenate([jnp.where(lane < DQK, q, z), jnp.where(lane >= DQK, q, z)], axis=0)


def _online_softmax_steps(ss, vs, ms, ls, accs):
    def bcast(m, n):
        return jnp.tile(m, (1, n // LANES)) if n >= LANES else m[:, :n]

    m_news = [jnp.maximum(m, jnp.max(s, axis=-1, keepdims=True)) for s, m in zip(ss, ms)]
    alphas = [jnp.exp2(m - mn) for m, mn in zip(ms, m_news)]
    ps = [jnp.exp2(s - bcast(mn, s.shape[1])) for s, mn in zip(ss, m_news)]
    l_news = [a * l + jnp.sum(p, axis=-1, keepdims=True) for a, l, p in zip(alphas, ls, ps)]
    pvs = [jnp.dot(p.astype(BF16), v, preferred_element_type=F32) for p, v in zip(ps, vs)]
    acc_news = [a * acc + pv for a, acc, pv in zip(alphas, accs, pvs)]
    return m_news, l_news, acc_news


def _attn_finish(acc, l, lam, subln_g, gate, out_scale):
    t = acc.shape[0] // 2
    o = acc[:t] / l[:t] - lam * (acc[t:] / l[t:])
    ms = jnp.mean(o * o, axis=-1, keepdims=True)
    o = o * lax.rsqrt(ms + SUBLN_EPS) * subln_g * out_scale
    return (o * gate.astype(F32)).astype(BF16)


def _prompt_attn_kernel(lam_ref, q_ref, k_ref, v_ref, g_ref, sg_ref, o_ref,
                        q12_ref, m_ref, l_ref, acc_ref, *, tq, heads, out_scale):
    qi = pl.program_id(2)
    for hp in range(heads):
        q12_ref[hp] = _stack_maps(q_ref[:, hp * DV:(hp + 1) * DV])
    m_ref[...] = jnp.full_like(m_ref, NEG_BIG)
    l_ref[...] = jnp.zeros_like(l_ref)
    acc_ref[...] = jnp.zeros_like(acc_ref)

    def tile(kj, masked):
        rows = pl.ds(pl.multiple_of(kj * tq, tq), tq)
        hcols = [slice(hp * DV, (hp + 1) * DV) for hp in range(heads)]
        ss = [lax.dot_general(q12_ref[hp], k_ref[rows, hcols[hp]], (((1,), (1,)), ((), ())),
                              preferred_element_type=F32) for hp in range(heads)]
        if masked:
            r = lax.broadcasted_iota(jnp.int32, ss[0].shape, 0) % tq
            c = lax.broadcasted_iota(jnp.int32, ss[0].shape, 1)
            ss = [jnp.where(c <= r, s, NEG_BIG) for s in ss]
        ms, ls, accs = _online_softmax_steps(
            ss, [v_ref[rows, hcols[hp]] for hp in range(heads)],
            [m_ref[hp] for hp in range(heads)], [l_ref[hp] for hp in range(heads)],
            [acc_ref[hp] for hp in range(heads)])
        for hp in range(heads):
            m_ref[hp] = ms[hp]
            l_ref[hp] = ls[hp]
            acc_ref[hp] = accs[hp]

    def body(kj, carry):
        tile(kj, False)
        return carry

    lax.fori_loop(0, qi, body, 0)
    tile(qi, True)
    lam = lam_ref[0, 0]
    for hp in range(heads):
        cols = slice(hp * DV, (hp + 1) * DV)
        o_ref[:, cols] = _attn_finish(acc_ref[hp], l_ref[hp], lam, sg_ref[...], g_ref[:, cols], out_scale)


def _prompt_attention(q, k16, v16, gates, lam, subln_g, *, batch, seq, out_scale, tq=512, heads=4):
    tq = min(tq, seq)
    nq = seq // tq
    w = heads * DV
    kern = functools.partial(_prompt_attn_kernel, tq=tq, heads=heads, out_scale=out_scale)
    return pl.pallas_call(
        kern,
        grid=(batch, N_ATT_HEADS // heads, nq),
        in_specs=[
            pl.BlockSpec(memory_space=pltpu.SMEM),
            pl.BlockSpec((tq, w), lambda b, h, i: (b * nq + i, h)),
            pl.BlockSpec((seq, w), lambda b, h, i: (b, h)),
            pl.BlockSpec((seq, w), lambda b, h, i: (b, h)),
            pl.BlockSpec((tq, w), lambda b, h, i: (b * nq + i, h)),
            pl.BlockSpec((1, DV), lambda b, h, i: (0, 0)),
        ],
        out_specs=pl.BlockSpec((tq, w), lambda b, h, i: (b * nq + i, h)),
        out_shape=jax.ShapeDtypeStruct((batch * seq, ATT_W), BF16),
        scratch_shapes=[pltpu.VMEM((heads, 2 * tq, DV), BF16),
                        pltpu.VMEM((heads, 2 * tq, LANES), F32),
                        pltpu.VMEM((heads, 2 * tq, LANES), F32),
                        pltpu.VMEM((heads, 2 * tq, DV), F32)],
        compiler_params=_cparams(("parallel", "parallel", "arbitrary")),
        name="prompt_attention",
    )(lam, q, k16, v16, gates, subln_g.reshape(1, DV))


QUADS = 4
HEADS_PER_QUAD = N_ATT_HEADS // QUADS


def _sample_attn_body(j, n_steps, lam_ref, q_ref, refs, *, t_new, pages_per_step, out_scale):
    kc_refs = refs[:pages_per_step]
    vc_refs = refs[pages_per_step:2 * pages_per_step]
    kn_ref, vn_ref, g_ref, sg_ref, o_ref, q12_ref, m_ref, l_ref, acc_ref = refs[2 * pages_per_step:]
    rows = 2 * t_new
    qrows = HEADS_PER_QUAD * rows
    ncols = PAGE_SIZE * HEADS_PER_QUAD

    @pl.when(j == 0)
    def _():
        for h in range(N_ATT_HEADS):
            c, i = h % QUADS, h // QUADS
            q12_ref[c, i * rows:(i + 1) * rows, :] = _stack_maps(q_ref[0, :, h * DV:(h + 1) * DV])
        m_ref[...] = jnp.full_like(m_ref, NEG_BIG)
        l_ref[...] = jnp.zeros_like(l_ref)
        acc_ref[...] = jnp.zeros_like(acc_ref)

    yield "init"
    own = (lax.broadcasted_iota(jnp.int32, (qrows, ncols), 0) // rows
           == lax.broadcasted_iota(jnp.int32, (qrows, ncols), 1) % HEADS_PER_QUAD)
    nt = (((1,), (1,)), ((), ()))
    quads = range(QUADS)
    ms = [m_ref[c] for c in quads]
    ls = [l_ref[c] for c in quads]
    accs = [acc_ref[c] for c in quads]
    for u in range(pages_per_step):
        scores = [jnp.where(own, lax.dot_general(q12_ref[c], kc_refs[u][pl.ds(c, ncols, stride=QUADS), :].astype(BF16),
                                                 nt, preferred_element_type=F32), NEG_BIG) for c in quads]
        yield "scores"
        vals = [vc_refs[u][pl.ds(c, ncols, stride=QUADS), :].astype(BF16) for c in quads]
        ms, ls, accs = _online_softmax_steps(scores, vals, ms, ls, accs)
        yield "softmax"
    for c in quads:
        m_ref[c] = ms[c]
        l_ref[c] = ls[c]
        acc_ref[c] = accs[c]

    yield "final"

    @pl.when(j == n_steps - 1)
    def _():
        r = lax.broadcasted_iota(jnp.int32, (rows, t_new), 0) % t_new
        cc = lax.broadcasted_iota(jnp.int32, (rows, t_new), 1)
        causal = cc <= r
        lam = lam_ref[0, 0]
        for h in range(N_ATT_HEADS):
            c, i = h % QUADS, h // QUADS
            sl = slice(i * rows, (i + 1) * rows)
            cols = slice(h * DV, (h + 1) * DV)
            s = lax.dot_general(q12_ref[c, sl, :], kn_ref[0, :, cols], (((1,), (1,)), ((), ())),
                                preferred_element_type=F32)
            s = jnp.where(causal, s, NEG_BIG)
            _, (l,), (acc,) = _online_softmax_steps([s], [vn_ref[0, :, cols]], [m_ref[c, sl, :]],
                                                    [l_ref[c, sl, :]], [acc_ref[c, sl, :]])
            o_ref[0, :, cols] = _attn_finish(acc, l, lam, sg_ref[...], g_ref[0, :, cols], out_scale)


def _sample_attn_kernel(pt_ref, lam_ref, q_ref, *refs, **kw):
    _run(_sample_attn_body(pl.program_id(1), pl.num_programs(1), lam_ref, q_ref, refs, **kw))


def _sample_attn_call(q, k16, v16, gates, cache_k, cache_v, page_table, lam, subln_g, *, layer, out_scale,
                      pages_per_step=8):
    b, t_new, _ = q.shape
    n_pages = page_table.shape[1]
    n_pool = cache_k.shape[1]
    pps = math.gcd(pages_per_step, n_pages)
    page_rows = PAGE_SIZE * N_ATT_HEADS
    kc2 = cache_k.reshape(-1, DV)
    vc2 = cache_v.reshape(-1, DV)
    tok_spec = pl.BlockSpec((1, t_new, ATT_W), lambda i, j, pt: (i, 0, 0))

    def page_spec(u):
        return pl.BlockSpec((page_rows, DV), lambda i, j, pt: (layer * n_pool + pt[i, j * pps + u], 0))

    page_specs = [page_spec(u) for u in range(pps)]
    qrows = HEADS_PER_QUAD * 2 * t_new
    return dict(
        grid=(b, n_pages // pps),
        kw=dict(t_new=t_new, pages_per_step=pps, out_scale=out_scale),
        in_specs=[pl.BlockSpec(memory_space=pltpu.SMEM), tok_spec] + page_specs + page_specs
                 + [tok_spec, tok_spec, tok_spec, pl.BlockSpec((1, DV), lambda i, j, pt: (0, 0))],
        args=[lam, q] + [kc2] * pps + [vc2] * pps + [k16, v16, gates, subln_g.reshape(1, DV)],
        out_specs=[tok_spec],
        out_shape=[jax.ShapeDtypeStruct((b, t_new, ATT_W), BF16)],
        scratch=[pltpu.VMEM((QUADS, qrows, DV), BF16),
                 pltpu.VMEM((QUADS, qrows, LANES), F32),
                 pltpu.VMEM((QUADS, qrows, LANES), F32),
                 pltpu.VMEM((QUADS, qrows, DV), F32)],
    )


def _sample_attention(page_table, call):
    grid_spec = pltpu.PrefetchScalarGridSpec(
        num_scalar_prefetch=1, grid=call["grid"], in_specs=call["in_specs"], out_specs=call["out_specs"],
        scratch_shapes=call["scratch"])
    return pl.pallas_call(
        functools.partial(_sample_attn_kernel, **call["kw"]),
        grid_spec=grid_spec,
        out_shape=call["out_shape"],
        compiler_params=_cparams(("parallel", "arbitrary")),
        name="sample_attention",
    )(page_table, *call["args"])[0]


def _stack_heads(x):
    lane = lax.broadcasted_iota(jnp.int32, x.shape, 1)
    z = jnp.zeros_like(x)
    return jnp.concatenate([jnp.where(lane < RWKV_HEAD, x, z), jnp.where(lane >= RWKV_HEAD, x, z)], axis=0)


def _each(f, *lists):
    return [f(*args) for args in zip(*lists)]


def _mm(a, b):
    return jnp.dot(a, b, preferred_element_type=F32)


def _rwkv_chunk(rs, krs, vs, lo, wls, w0s, a0s, k_ks, k_as, r_ks, lnx_gs, lnx_bs, gates, h0s, consts):
    ones, tril, strict, incl, eye = consts
    C = lo.shape[0]
    n2 = 2 * C
    lane = lax.broadcasted_iota(jnp.int32, (C, LANES), 1)
    lo_act = jnp.where(lane < LORA, jnp.tanh(lo), lo).astype(BF16)
    pres = [_mm(lo_act, wl) for wl in wls]
    logws = _each(lambda pre, w0: -math.exp(-0.5) * jax.nn.sigmoid(w0 + pre[:, :LANES]), pres, w0s)
    a_sigs = _each(lambda pre, a0: jax.nn.sigmoid(a0 + pre[:, LANES:]), pres, a0s)
    yield "stage"

    kks = _each(lambda kr, k_k: kr * k_k, krs, k_ks)
    kk_ss = _segment_sums([kk * kk for kk in kks], ones)
    kks = _each(lambda kk, ss: kk / jnp.maximum(jnp.sqrt(ss), 1e-12), kks, kk_ss)
    k_hs = _each(lambda kr, a_sig, k_a: kr * (1.0 + (a_sig - 1.0) * k_a), krs, a_sigs, k_as)
    rk_ss = _segment_sums([r * k_h * r_k for r, k_h, r_k in zip(rs, k_hs, r_ks)], ones)
    bonuses = _each(lambda ss, v: ss * v, rk_ss, vs)
    yield "stage"

    def split3(x):
        l1 = x.astype(BF16)
        rem = x - l1.astype(F32)
        l2 = rem.astype(BF16)
        return jnp.concatenate([l1, l2, (rem - l2.astype(F32)).astype(BF16)], axis=0)

    cums = [_mm(tril, split3(logw)) for logw in logws]
    yield "stage"
    g_incls = [jnp.exp(cum) for cum in cums]
    g_excls = _each(lambda cum, logw: jnp.exp(cum - logw), cums, logws)
    g_invs = [jnp.exp(-cum) for cum in cums]
    g_lasts = [g[C - 1:C, :] for g in g_incls]

    a_ts = _each(lambda kk, g: _stack_heads(-kk * g).astype(BF16), kks, g_excls)
    r_ts = _each(lambda r, g: _stack_heads(r * g).astype(BF16), rs, g_incls)
    v_sts = [_stack_heads(v).astype(BF16) for v in vs]
    bks = _each(lambda kk, a_sig, k_h, g: jnp.concatenate([_stack_heads(kk * a_sig * g), _stack_heads(k_h * g)],
                                                          axis=0), kks, a_sigs, k_hs, g_invs)
    gmats = _each(lambda a_t, r_t, bk: lax.dot_general(jnp.concatenate([a_t, r_t], axis=0), bk.astype(BF16),
                                                       (((1,), (1,)), ((), ())), preferred_element_type=F32),
                  a_ts, r_ts, bks)
    yield "stage"
    a_abs = [jnp.where(strict, g[:n2, :n2], 0.0) for g in gmats]
    a_aks = [jnp.where(strict, g[:n2, n2:], 0.0).astype(BF16) for g in gmats]
    a_rbs = [jnp.where(incl, g[n2:, :n2], 0.0).astype(BF16) for g in gmats]
    a_rks = [jnp.where(incl, g[n2:, n2:], 0.0).astype(BF16) for g in gmats]
    yield "stage"

    eye2 = jnp.where(lax.broadcasted_iota(jnp.int32, (n2, n2), 0)
                     == lax.broadcasted_iota(jnp.int32, (n2, n2), 1), 1.0, 0.0)
    n_factors = int(math.log2(C))
    pows = [a_abs]
    pows16 = [[a.astype(BF16) for a in a_abs]]
    for _ in range(n_factors - 1):
        pows.append([_mm(pw, pw) for pw in pows16[-1]])
        pows16.append([x.astype(BF16) for x in pows[-1]])
        yield "stage"
    prods = [_each(lambda x, y, x16, y16: eye2 + x + y + _mm(x16, y16), pows[i], pows[i + 1], pows16[i], pows16[i + 1])
             for i in range(0, n_factors - 1, 2)]
    if n_factors % 2:
        prods.append([eye2 + x for x in pows[-1]])
    yield "stage"
    while len(prods) > 1:
        nxt = [_each(lambda a, b: _mm(a.astype(BF16), b.astype(BF16)), prods[i], prods[i + 1])
               for i in range(0, len(prods) - 1, 2)]
        prods = nxt + ([prods[-1]] if len(prods) % 2 else [])
        yield "stage"
    t_invs = prods[0]

    h16s = [h0.astype(BF16) for h0 in h0s]
    if n2 % LANES == 0:
        cat = jnp.concatenate
        xs = _each(lambda a_t, h16, a_ak, v_st: _mm(cat([a_t, a_ak], axis=1), cat([h16, v_st], axis=0)),
                   a_ts, h16s, a_aks, v_sts)
    else:
        xs = _each(lambda a_t, h16, a_ak, v_st: _mm(a_t, h16) + _mm(a_ak, v_st), a_ts, h16s, a_aks, v_sts)
    u16s = _each(lambda t, x: _mm(t.astype(BF16), x.astype(BF16)).astype(BF16), t_invs, xs)
    yield "stage"
    if n2 % LANES == 0:
        o_sts = _each(lambda r_t, h16, a_rb, u16, a_rk, v_st:
                      _mm(cat([r_t, a_rb, a_rk], axis=1), cat([h16, u16, v_st], axis=0)),
                      r_ts, h16s, a_rbs, u16s, a_rks, v_sts)
    else:
        o_sts = _each(lambda r_t, h16, a_rb, u16, a_rk, v_st: _mm(r_t, h16) + _mm(a_rb, u16) + _mm(a_rk, v_st),
                      r_ts, h16s, a_rbs, u16s, a_rks, v_sts)
    os_ = [o_st[:C] + o_st[C:] for o_st in o_sts]
    yield "stage"

    bk_ts = _each(lambda bk, g_last: (bk * g_last).T.astype(BF16), bks, g_lasts)
    g_cols = [jnp.sum(jnp.where(eye, jnp.broadcast_to(g_last, (LANES, LANES)), 0.0), axis=1, keepdims=True)
              for g_last in g_lasts]
    h_news = _each(lambda g_col, h0, bk_t, u16, v_st: g_col * h0 + _mm(bk_t, jnp.concatenate([u16, v_st], axis=0)),
                   g_cols, h0s, bk_ts, u16s, v_sts)
    yield "stage"

    mus = [z * (1.0 / RWKV_HEAD) for z in _segment_sums(os_, ones)]
    ds = _each(lambda o, mu: o - mu, os_, mus)
    vars_ = [z * (1.0 / RWKV_HEAD) for z in _segment_sums([d * d for d in ds], ones)]
    ys = _each(lambda d, var, g, b, bonus, gate:
               (((d * lax.rsqrt(var + LNX_EPS) * g + b) + bonus) * gate.astype(F32)).astype(BF16),
               ds, vars_, lnx_gs, lnx_bs, bonuses, gates)
    return ys, h_news


def _rwkv_body(c, n_chunks, refs, *, chunk, pairs, use_s0):
    (r_ref, k_ref, v_ref, lo_ref, pr_ref, pk_ref, pv_ref, plo_ref,
     tmr_ref, tmk_ref, tmv_ref, tmlo_ref, wl_ref, w0_ref, a0_ref, kk_ref, ka_ref, rk_ref,
     lg_ref, lb_ref, gate_ref, s0_ref, y_ref, sout_ref,
     h_ref, cr_ref, ck_ref, cv_ref, clo_ref) = refs
    C = chunk
    n2 = 2 * C

    @pl.when(c == 0)
    def _():
        cr_ref[...] = pr_ref[0]
        ck_ref[...] = pk_ref[0]
        cv_ref[...] = pv_ref[0]
        clo_ref[...] = plo_ref[0]
        if use_s0:
            z = jnp.zeros((RWKV_HEAD, RWKV_HEAD), F32)
            for u in range(pairs):
                st = jnp.concatenate([jnp.concatenate([s0_ref[0, 2 * u], z], axis=1),
                                      jnp.concatenate([z, s0_ref[0, 2 * u + 1]], axis=1)], axis=0)
                h_ref[u] = st.T
        else:
            h_ref[...] = jnp.zeros_like(h_ref)

    yield "init"
    ri = lax.broadcasted_iota(jnp.int32, (n2, n2), 0)
    ci = lax.broadcasted_iota(jnp.int32, (n2, n2), 1)
    tril = (lax.broadcasted_iota(jnp.int32, (C, 3 * C), 1) % C <= lax.broadcasted_iota(jnp.int32, (C, 3 * C), 0))
    eye = (lax.broadcasted_iota(jnp.int32, (LANES, LANES), 0)
           == lax.broadcasted_iota(jnp.int32, (LANES, LANES), 1))
    consts = (_half_ones(), jnp.where(tril, 1.0, 0.0).astype(BF16), ri > ci, ri >= ci, eye)

    def token_shift(cur_ref3, carry_ref, tm_ref):
        cur = cur_ref3[0]
        row0 = lax.broadcasted_iota(jnp.int32, cur.shape, 0) == 0
        prev = jnp.where(row0, carry_ref[...], pltpu.roll(cur, 1, 0))
        carry_ref[...] = cur[C - 1:C, :]
        return cur + tm_ref[...] * (prev - cur)

    lo = token_shift(lo_ref, clo_ref, tmlo_ref)
    r_all = token_shift(r_ref, cr_ref, tmr_ref)
    k_all = token_shift(k_ref, ck_ref, tmk_ref)
    v_all = token_shift(v_ref, cv_ref, tmv_ref)
    col = lambda u: slice(u * LANES, (u + 1) * LANES)
    per_pair = lambda x: [x[:, col(u)] for u in range(pairs)]
    ys, h_news = yield from _rwkv_chunk(
        per_pair(r_all), per_pair(k_all), per_pair(v_all), lo, [wl_ref[u] for u in range(pairs)],
        per_pair(w0_ref[...]), per_pair(a0_ref[...]), per_pair(kk_ref[...]), per_pair(ka_ref[...]),
        per_pair(rk_ref[...]), per_pair(lg_ref[...]), per_pair(lb_ref[...]), per_pair(gate_ref[0]),
        [h_ref[u] for u in range(pairs)], consts)
    for u in range(pairs):
        y_ref[0, :, col(u)] = ys[u]
        h_ref[u] = h_news[u]

    yield "final"

    @pl.when(c == n_chunks - 1)
    def _():
        for u in range(pairs):
            st = h_news[u].T
            sout_ref[0, 2 * u] = st[:RWKV_HEAD, :RWKV_HEAD]
            sout_ref[0, 2 * u + 1] = st[RWKV_HEAD:, RWKV_HEAD:]


def _rwkv_kernel(*refs, **kw):
    _run(_rwkv_body(pl.program_id(2), pl.num_programs(2), refs, **kw))


def _rwkv_call(rkv, lora, gates, prevs, s0, p, *, chunk, pairs):
    b, t, _ = lora.shape
    n_chunks = t // chunk
    ng = N_RWKV_PAIRS // pairs
    w = pairs * LANES
    use_s0 = s0 is not None
    sblk = (1, 2 * pairs, RWKV_HEAD, RWKV_HEAD)
    if s0 is None:
        s0 = jnp.zeros(sblk, F32)
        s0_spec = pl.BlockSpec(sblk, lambda i, j, c: (0, 0, 0, 0))
    else:
        s0_spec = pl.BlockSpec(sblk, lambda i, j, c: (i, j, 0, 0))

    tok = pl.BlockSpec((1, chunk, w), lambda i, j, c: (i, c, j))
    prev = pl.BlockSpec((1, 1, w), lambda i, j, c: (i, 0, j))

    def par():
        return pl.BlockSpec((1, w), lambda i, j, c: (0, j))

    return dict(
        grid=(b, ng, n_chunks),
        kw=dict(chunk=chunk, pairs=pairs, use_s0=use_s0),
        in_specs=[
            tok, tok, tok,
            pl.BlockSpec((1, chunk, LANES), lambda i, j, c: (i, c, 0)),
            prev, prev, prev,
            pl.BlockSpec((1, 1, LANES), lambda i, j, c: (i, 0, 0)),
            par(), par(), par(), pl.BlockSpec((1, LANES), lambda i, j, c: (0, 0)),
            pl.BlockSpec((pairs, LANES, 2 * LANES), lambda i, j, c: (j, 0, 0)),
            par(), par(), par(), par(), par(), par(), par(),
            pl.BlockSpec((1, chunk, w), lambda i, j, c: (i, c, ng + j)),
            s0_spec,
        ],
        args=[*rkv, lora, *prevs, p["tm_r"], p["tm_k"], p["tm_v"], p["tm_lora"], p["w_lora"],
              p["w0"], p["a0"], p["k_k"], p["k_a"], p["r_k"], p["lnx_g"], p["lnx_b"], gates, s0],
        out_specs=[pl.BlockSpec((1, chunk, w), lambda i, j, c: (i, c, j)),
                   pl.BlockSpec(sblk, lambda i, j, c: (i, j, 0, 0))],
        out_shape=[jax.ShapeDtypeStruct((b, t, RWKV_W), BF16),
                   jax.ShapeDtypeStruct((b, N_RWKV_HEADS, RWKV_HEAD, RWKV_HEAD), F32)],
        scratch=[pltpu.VMEM((pairs, LANES, LANES), F32),
                 pltpu.VMEM((1, w), F32), pltpu.VMEM((1, w), F32),
                 pltpu.VMEM((1, w), F32), pltpu.VMEM((1, LANES), F32)],
    )


def _rwkv(call):
    return pl.pallas_call(
        functools.partial(_rwkv_kernel, **call["kw"]),
        grid=call["grid"],
        in_specs=call["in_specs"],
        out_specs=call["out_specs"],
        out_shape=call["out_shape"],
        scratch_shapes=call["scratch"],
        compiler_params=_cparams(("parallel", "parallel", "arbitrary")),
        name="rwkv7_chunked",
    )(*call["args"])


def _remap(spec, decode):
    if spec.block_shape is None:
        return spec
    return pl.BlockSpec(spec.block_shape, lambda *a: spec.index_map(*decode(*a)))


def _fused_kernel(pt_ref, *refs, n_in, n_out, n_scr, rwkv_kw, attn_kw, n_chunks, attn_steps):
    (ri, ai), (ro, ao), (rs, as_) = n_in, n_out, n_scr
    r_in, a_in = refs[:ri], refs[ri:ri + ai]
    r_out, a_out = refs[ri + ai:ri + ai + ro], refs[ri + ai + ro:ri + ai + ro + ao]
    r_scr, a_scr = refs[ri + ai + ro + ao:ri + ai + ro + ao + rs], refs[ri + ai + ro + ao + rs:]
    s = pl.program_id(0)
    _interleave(
        _rwkv_body(s % n_chunks, n_chunks, (*r_in, *r_out, *r_scr), **rwkv_kw),
        _sample_attn_body(s % attn_steps, attn_steps, a_in[0], a_in[1], (*a_in[2:], *a_out, *a_scr), **attn_kw))


def _rwkv_with_sample_attention(rwkv_call, attn_call, page_table):
    b_r, ng, nc = rwkv_call["grid"]
    b_a, nj = attn_call["grid"]
    n_steps = b_r * ng * nc
    assert n_steps == b_a * nj
    dec_r = lambda s, pt: (s // (ng * nc), (s // nc) % ng, s % nc)
    dec_a = lambda s, pt: (s // nj, s % nj, pt)
    in_specs = [_remap(sp, dec_r) for sp in rwkv_call["in_specs"]] + [_remap(sp, dec_a) for sp in attn_call["in_specs"]]
    out_specs = [_remap(sp, dec_r) for sp in rwkv_call["out_specs"]] + [_remap(sp, dec_a) for sp in attn_call["out_specs"]]
    kern = functools.partial(
        _fused_kernel,
        n_in=(len(rwkv_call["in_specs"]), len(attn_call["in_specs"])),
        n_out=(len(rwkv_call["out_specs"]), len(attn_call["out_specs"])),
        n_scr=(len(rwkv_call["scratch"]), len(attn_call["scratch"])),
        rwkv_kw=rwkv_call["kw"], attn_kw=attn_call["kw"], n_chunks=nc, attn_steps=nj)
    grid_spec = pltpu.PrefetchScalarGridSpec(
        num_scalar_prefetch=1, grid=(n_steps,), in_specs=in_specs, out_specs=out_specs,
        scratch_shapes=rwkv_call["scratch"] + attn_call["scratch"])
    y_rwkv, s_out, y_att = pl.pallas_call(
        kern,
        grid_spec=grid_spec,
        out_shape=rwkv_call["out_shape"] + attn_call["out_shape"],
        compiler_params=_cparams(("arbitrary",), FUSED_VMEM_LIMIT_BYTES),
        name="rwkv7_and_paged_attention",
    )(page_table, *rwkv_call["args"], *attn_call["args"])
    return y_rwkv, s_out, y_att


def _prep_layer_params(l, ln_g, w_in, q_norm_g, k_norm_g, lambda_q1, lambda_k1, lambda_q2, lambda_k2,
                       subln_g, time_mix, w0, w2, a0, a2, k_k, k_a, r_k, lnx_g, lnx_b, w_out):
    w = w_in[l]
    sh = OFF_SHIFT
    o1, o2, o3, o4 = RWKV_W, RWKV_W + LORA, 2 * RWKV_W + LORA, 3 * RWKV_W + LORA
    w_qkvr = w[:, :sh + o1].astype(BF16)
    w_rk = w[:, sh + o2:sh + o3].astype(BF16)
    w_rv = w[:, sh + o3:sh + o4].astype(BF16)
    w_gates = w[:, OFF_GA:].astype(BF16)
    w_lora_in = jnp.concatenate([w[:, sh + o1:sh + o2], w[:, sh + o4:sh + SHIFT_W]], axis=1).astype(BF16)
    tmix = time_mix[l]
    tm_lora = jnp.concatenate([tmix[o1:o2], tmix[o4:]]).reshape(1, 2 * LORA)
    w2p = w2[l].reshape(LORA, N_RWKV_PAIRS, LANES).transpose(1, 0, 2)
    a2p = a2[l].reshape(LORA, N_RWKV_PAIRS, LANES).transpose(1, 0, 2)
    z = jnp.zeros_like(w2p)
    w_lora = jnp.concatenate([jnp.concatenate([w2p, z], axis=2), jnp.concatenate([z, a2p], axis=2)],
                             axis=1).astype(BF16)
    lam_init = 0.8 - 0.6 * math.exp(-0.3 * l)
    lam = (jnp.exp(jnp.sum(lambda_q1[l] * lambda_k1[l])) - jnp.exp(jnp.sum(lambda_q2[l] * lambda_k2[l]))
           + lam_init).reshape(1, 1).astype(F32)
    row = lambda z_: z_.reshape(1, -1)
    return dict(
        layer=l, ln_g=ln_g[l], w_qkvr=w_qkvr, w_rk=w_rk, w_rv=w_rv, w_gates=w_gates, w_lora_in=w_lora_in,
        w_out=w_out[l].astype(BF16),
        q_gain=jnp.tile(q_norm_g[l], 2).reshape(1, DV), k_gain=jnp.tile(k_norm_g[l], 2).reshape(1, DV),
        lam=lam, out_scale=1.0 - lam_init, subln_g=subln_g[l],
        tm_r=row(tmix[:o1]), tm_k=row(tmix[o2:o3]), tm_v=row(tmix[o3:o4]), tm_lora=tm_lora, w_lora=w_lora,
        w0=row(w0[l]), a0=row(a0[l]), k_k=row(k_k[l]), k_a=row(k_a[l]), r_k=row(r_k[l]),
        lnx_g=row(lnx_g[l]), lnx_b=row(lnx_b[l]),
    )


def _project(x, pos, p):
    b, t, d = x.shape
    m = b * t
    x2 = x.reshape(m, d)
    h = _rmsnorm(x2, p["ln_g"])

    tm = min(1024, m)
    cos, s1, s2 = _rope_tables(pos)
    if t < tm:
        reps = tm // t
        cos, s1, s2 = (jnp.tile(z, (reps, 1)) for z in (cos, s1, s2))
    gain_spec = pl.BlockSpec((1, LANES), lambda i, j: (0, 0))
    blk = lambda cols: cols // LANES
    tm_hm, tn_hm = min(512, m), 1024

    def qk(col_off, gain, scale, dts):
        tm_, tn_ = (tm_hm, tn_hm) if "f32_head_major" in dts else (tm, 512)
        n_tab = cos.shape[0] // tm_
        tab_spec = pl.BlockSpec((tm_, LANES), lambda i, j: (i % n_tab, 0))
        kern = functools.partial(_mm_qk_kernel, scale=scale, rows_per_chunk=min(256, tm_))
        return _matmul([h], p["w_qkvr"], col_block_off=blk(col_off), n_cols=ATT_W, kernel=kern,
                       out_dtypes=dts, extra=(gain, cos, s1, s2),
                       extra_specs=(gain_spec, tab_spec, tab_spec, tab_spec), scratch=True, tm=tm_, tn=tn_)

    def plain(w, col_off, n_cols):
        return _matmul([h], w, col_block_off=blk(col_off), n_cols=n_cols, kernel=_mm_plain_kernel,
                       out_dtypes=(F32,), tm=tm, tn=1024)[0]

    (q16,) = qk(0, p["q_gain"], DQK ** -0.5 * math.log2(math.e), (BF16,))
    k32, k16 = qk(OFF_K, p["k_gain"], 1.0, ("f32_head_major", BF16))
    v32, v16 = _matmul([h], p["w_qkvr"], col_block_off=blk(OFF_V), n_cols=ATT_W,
                       kernel=_mm_dual_kernel, out_dtypes=("f32_head_major", BF16), tm=tm_hm, tn=tn_hm)
    rkv = [plain(p["w_qkvr"], OFF_SHIFT, RWKV_W).reshape(b, t, RWKV_W),
           plain(p["w_rk"], 0, RWKV_W).reshape(b, t, RWKV_W),
           plain(p["w_rv"], 0, RWKV_W).reshape(b, t, RWKV_W)]
    lora = plain(p["w_lora_in"], 0, 2 * LORA).reshape(b, t, 2 * LORA)
    (gates,) = _matmul([h], p["w_gates"], col_block_off=0, n_cols=ATT_W + RWKV_W,
                       kernel=_mm_silu_kernel, out_dtypes=(BF16,), tm=tm, tn=1024)
    return dict(b=b, t=t, m=m, tm=tm, x2=x2, q16=q16, k32=k32, k16=k16, v32=v32, v16=v16, rkv=rkv, lora=lora,
                gates=gates)


def _rwkv_call_for(pr, shift_prev, wkv_prev, p, *, chunk, pairs):
    b = pr["b"]
    o1, o2, o3, o4 = RWKV_W, RWKV_W + LORA, 2 * RWKV_W + LORA, 3 * RWKV_W + LORA
    prev_lora = jnp.concatenate([shift_prev[:, o1:o2], shift_prev[:, o4:]], axis=1)
    prevs = [z.reshape(b, 1, -1) for z in (shift_prev[:, :o1], shift_prev[:, o2:o3], shift_prev[:, o3:o4],
                                           prev_lora)]
    return _rwkv_call(pr["rkv"], pr["lora"], pr["gates"].reshape(b, pr["t"], -1), prevs, wkv_prev, p,
                      chunk=chunk, pairs=pairs)


def _finish_layer(pr, y_att, y_rwkv, p):
    b, t, m, tm = pr["b"], pr["t"], pr["m"], pr["tm"]
    d = pr["x2"].shape[1]
    last_lora = pr["lora"][:, -1]
    shift_new = jnp.concatenate([pr["rkv"][0][:, -1], last_lora[:, :LORA], pr["rkv"][1][:, -1],
                                 pr["rkv"][2][:, -1], last_lora[:, LORA:]], axis=1)
    tn_out = 512
    x_spec = pl.BlockSpec((tm, tn_out), lambda i, j: (i, j))
    (out,) = _matmul([y_att.reshape(m, ATT_W), y_rwkv.reshape(m, RWKV_W)], p["w_out"], col_block_off=0,
                     n_cols=d, kernel=_mm_residual_kernel, out_dtypes=(F32,), extra=(pr["x2"],),
                     extra_specs=(x_spec,), tm=tm, tn=tn_out)
    return (out.reshape(b, t, d), pr["k32"].reshape(b, t, N_ATT_HEADS, DV),
            pr["v32"].reshape(b, t, N_ATT_HEADS, DV), shift_new)


def _layer_pair(xp, xs, pos_p, pos_s, shift_s, wkv_s, cache_k, cache_v, page_table, p, *,
                chunk_p=64, pairs_p=8, pairs_s=16, pages_per_step=8):
    pp = _project(xp, pos_p, p)
    ps = _project(xs, pos_s, p)
    b_s, t_s = ps["b"], ps["t"]
    rw_p = _rwkv_call_for(pp, jnp.zeros((pp["b"], SHIFT_W), F32), None, p, chunk=chunk_p, pairs=pairs_p)
    rw_s = _rwkv_call_for(ps, shift_s, wkv_s, p, chunk=t_s, pairs=pairs_s)
    at_s = _sample_attn_call(ps["q16"].reshape(b_s, t_s, ATT_W), ps["k16"].reshape(b_s, t_s, ATT_W),
                             ps["v16"].reshape(b_s, t_s, ATT_W), ps["gates"].reshape(b_s, t_s, -1),
                             cache_k, cache_v, page_table, p["lam"], p["subln_g"], layer=p["layer"],
                             out_scale=p["out_scale"], pages_per_step=pages_per_step)
    if math.prod(rw_p["grid"]) == math.prod(at_s["grid"]):
        y_rwkv_p, wkv_p, y_att_s = _rwkv_with_sample_attention(rw_p, at_s, page_table)
    else:
        y_rwkv_p, wkv_p = _rwkv(rw_p)
        y_att_s = _sample_attention(page_table, at_s)
    y_att_p = _prompt_attention(pp["q16"], pp["k16"], pp["v16"], pp["gates"], p["lam"], p["subln_g"],
                                batch=pp["b"], seq=pp["t"], out_scale=p["out_scale"])
    y_rwkv_s, wkv_s_new = _rwkv(rw_s)
    return (*_finish_layer(pp, y_att_p, y_rwkv_p, p), wkv_p), (*_finish_layer(ps, y_att_s, y_rwkv_s, p), wkv_s_new)


def kernel(x_prompt, x_sample, cache_k, cache_v, state_shift, state_wkv, page_table, ln_g, w_in, q_norm_g,
           k_norm_g, lambda_q1, lambda_k1, lambda_q2, lambda_k2, subln_g, time_mix, w0, w2, a0, a2, k_k,
           k_a, r_k, lnx_g, lnx_b, w_out):
    depth = w_in.shape[0]
    s_p, s_s = x_prompt.shape[1], x_sample.shape[1]
    past_len = page_table.shape[1] * PAGE_SIZE
    pos_p = jnp.arange(s_p)
    pos_s = past_len + jnp.arange(s_s)
    yp, ys = x_prompt, x_sample
    outs = [[] for _ in range(8)]
    for l in range(depth):
        p = _prep_layer_params(l, ln_g, w_in, q_norm_g, k_norm_g, lambda_q1, lambda_k1, lambda_q2,
                               lambda_k2, subln_g, time_mix, w0, w2, a0, a2, k_k, k_a, r_k, lnx_g, lnx_b,
                               w_out)
        (yp, kp, vp, shp, wkvp), (ys, ks, vs, shs, wkvs) = _layer_pair(
            yp, ys, pos_p, pos_s, state_shift[l], state_wkv[l], cache_k, cache_v, page_table, p)
        for lst, val in zip(outs, (kp, vp, shp, wkvp, ks, vs, shs, wkvs)):
            lst.append(val)
    stacked = [jnp.stack(lst, 0) for lst in outs]
    return (yp, ys, *stacked)
```

```python
import functools
import math

import jax
import jax.numpy as jnp
from jax import lax
from jax.experimental import pallas as pl
from jax.experimental.pallas import tpu as pltpu

F32 = jnp.float32
BF16 = jnp.bfloat16

D_MODEL = 4096
PAGE_SIZE = 128
ATT_W = 2048
RWKV_W = 2048
DV = 128
N_ATT_HEADS = ATT_W // DV
DQK = DV // 2
ROT_DIM = DQK // 4
ROPE_THETA = 500000.0
RWKV_HEAD = 64
N_RWKV_HEADS = RWKV_W // RWKV_HEAD
N_RWKV_PAIRS = N_RWKV_HEADS // 2
LORA = 64
SHIFT_W = 3 * RWKV_W + 2 * LORA
OFF_K = ATT_W
OFF_V = 2 * ATT_W
OFF_SHIFT = 3 * ATT_W
OFF_GA = OFF_SHIFT + SHIFT_W
OFF_GR = OFF_GA + ATT_W
IN_W = OFF_GR + RWKV_W
NORM_EPS = 1e-6
SUBLN_EPS = 1e-5
LNX_EPS = 64e-5

LANES = 128
VMEM_LIMIT_BYTES = 48 * 1024 * 1024
FUSED_VMEM_LIMIT_BYTES = 56 * 1024 * 1024
NEG_BIG = -1e30


def _cparams(sem, vmem_limit_bytes=VMEM_LIMIT_BYTES):
    return pltpu.CompilerParams(dimension_semantics=sem, vmem_limit_bytes=vmem_limit_bytes)


def _run(staged_body):
    for _ in staged_body:
        pass


def _interleave(*staged_bodies):
    active, finals = list(staged_bodies), []
    while active:
        for g in list(active):
            tag = next(g, "done")
            if tag == "done":
                active.remove(g)
            elif tag == "final":
                active.remove(g)
                finals.append(g)
    for g in finals:
        _run(g)


def _split_dot(x, w2_bf16):
    hi = x.astype(BF16)
    lo = (x - hi.astype(F32)).astype(BF16)
    return jnp.dot(jnp.concatenate([hi, lo], axis=1), w2_bf16, preferred_element_type=F32)


def _segment_sums(xs, w2_bf16):
    rows = xs[0].shape[0]
    out = _split_dot(jnp.concatenate(xs, axis=0), w2_bf16)
    return [out[i * rows:(i + 1) * rows] for i in range(len(xs))]


def _half_ones():
    r = (lax.broadcasted_iota(jnp.int32, (2 * LANES, LANES), 0) % LANES) // RWKV_HEAD
    c = lax.broadcasted_iota(jnp.int32, (2 * LANES, LANES), 1) // RWKV_HEAD
    return jnp.where(r == c, 1.0, 0.0).astype(BF16)


def _rmsnorm_kernel(x_ref, g_ref, o_ref):
    rows_per_iter = 16

    def body(i, carry):
        rows = pl.ds(pl.multiple_of(i * rows_per_iter, rows_per_iter), rows_per_iter)
        x = x_ref[rows, :]
        ms = jnp.mean(x * x, axis=-1, keepdims=True)
        o_ref[rows, :] = (x * lax.rsqrt(ms + NORM_EPS) * g_ref[...]).astype(BF16)
        return carry

    n_iter = x_ref.shape[0] // rows_per_iter
    lax.fori_loop(0, n_iter, body, 0, unroll=math.gcd(n_iter, 8))


def _rmsnorm(x2, g, tm=512):
    m, d = x2.shape
    tm = min(tm, m)
    return pl.pallas_call(
        _rmsnorm_kernel,
        grid=(m // tm,),
        in_specs=[pl.BlockSpec((tm, d), lambda i: (i, 0)),
                  pl.BlockSpec((1, d), lambda i: (0, 0))],
        out_specs=pl.BlockSpec((tm, d), lambda i: (i, 0)),
        out_shape=jax.ShapeDtypeStruct((m, d), BF16),
        compiler_params=_cparams(("parallel",)),
        name="rmsnorm",
    )(x2, g.reshape(1, d))


def _mm_full_k(lhs_refs, w_ref):
    acc, off = None, 0
    for l_ref in lhs_refs:
        k = l_ref.shape[1]
        part = jnp.dot(l_ref[...], w_ref[off:off + k, :], preferred_element_type=F32)
        acc = part if acc is None else acc + part
        off += k
    return acc


def _mm_plain_kernel(h_ref, w_ref, o_ref):
    o_ref[...] = _mm_full_k([h_ref], w_ref).astype(o_ref.dtype)


def _mm_dual_kernel(h_ref, w_ref, o32_ref, o16_ref):
    a = _mm_full_k([h_ref], w_ref)
    o16_ref[...] = a.astype(BF16)
    for hh in range(a.shape[1] // LANES):
        o32_ref[:, hh, :] = a[:, hh * LANES:(hh + 1) * LANES]


def _mm_silu_kernel(h_ref, w_ref, o_ref):
    a = _mm_full_k([h_ref], w_ref)
    o_ref[...] = (a * jax.nn.sigmoid(a)).astype(o_ref.dtype)


def _mm_residual_kernel(ya_ref, yr_ref, w_ref, x_ref, o_ref):
    o_ref[...] = x_ref[...] + _mm_full_k([ya_ref, yr_ref], w_ref)


def _mm_qk_kernel(h_ref, w_ref, g_ref, cos_ref, s1_ref, s2_ref, *rest, scale, rows_per_chunk):
    outs, acc_ref = rest[:-1], rest[-1]
    acc_ref[...] = _mm_full_k([h_ref], w_ref)
    tm, tn = acc_ref.shape
    ones = _half_ones()
    gain = g_ref[...]

    def chunk(r, carry):
        rows = pl.ds(pl.multiple_of(r * rows_per_chunk, rows_per_chunk), rows_per_chunk)
        c = cos_ref[rows, :]
        s1 = s1_ref[rows, :]
        s2 = s2_ref[rows, :]
        for hh in range(tn // LANES):
            cols = slice(hh * LANES, (hh + 1) * LANES)
            x = acc_ref[rows, cols]
            ss = _split_dot(x * x, ones)
            y = x * lax.rsqrt(ss * (1.0 / DQK) + NORM_EPS) * gain
            y = y * c + pltpu.roll(y, LANES - ROT_DIM // 2, 1) * s1 + pltpu.roll(y, ROT_DIM // 2, 1) * s2
            if scale != 1.0:
                y = y * scale
            for o_ref in outs:
                if len(o_ref.shape) == 3:
                    o_ref[rows, hh, :] = y.astype(o_ref.dtype)
                else:
                    o_ref[rows, cols] = y.astype(o_ref.dtype)
        return carry

    lax.fori_loop(0, tm // rows_per_chunk, chunk, 0)


def _matmul(lhs, w, *, col_block_off, n_cols, kernel, out_dtypes, extra=(), extra_specs=(),
            scratch=False, tm=1024, tn=512):
    m = lhs[0].shape[0]
    tm = min(tm, m)
    tn = min(tn, n_cols)
    assert m % tm == 0 and n_cols % tn == 0 and sum(l.shape[1] for l in lhs) == w.shape[0]
    assert (col_block_off * LANES) % tn == 0
    joff = col_block_off * LANES // tn
    in_specs = [pl.BlockSpec((tm, l.shape[1]), lambda i, j: (i, 0)) for l in lhs]
    in_specs.append(pl.BlockSpec((w.shape[0], tn), lambda i, j: (0, j + joff)))
    in_specs += list(extra_specs)
    out_specs, out_shape = [], []
    for dt in out_dtypes:
        if dt == "f32_head_major":
            out_specs.append(pl.BlockSpec((tm, tn // LANES, LANES), lambda i, j: (i, j, 0)))
            out_shape.append(jax.ShapeDtypeStruct((m, n_cols // LANES, LANES), F32))
        else:
            out_specs.append(pl.BlockSpec((tm, tn), lambda i, j: (i, j)))
            out_shape.append(jax.ShapeDtypeStruct((m, n_cols), dt))
    return pl.pallas_call(
        kernel,
        grid=(m // tm, n_cols // tn),
        in_specs=in_specs,
        out_specs=out_specs,
        out_shape=out_shape,
        scratch_shapes=[pltpu.VMEM((tm, tn), F32)] if scratch else [],
        compiler_params=_cparams(("parallel", "arbitrary")),
        name="proj_matmul",
    )(*lhs, w, *extra)


def _rope_tables(pos):
    half = ROT_DIM // 2
    inv_freq = 1.0 / (ROPE_THETA ** (jnp.arange(0, ROT_DIM, 2, dtype=F32) / ROT_DIM))
    ang = pos.astype(F32)[:, None] * inv_freq[None, :]
    cos = jnp.cos(ang)
    sin = jnp.sin(ang)
    t = pos.shape[0]
    one = jnp.ones((t, DQK - ROT_DIM), F32)
    zero = jnp.zeros((t, DQK - ROT_DIM), F32)
    zh = jnp.zeros((t, half), F32)
    c64 = jnp.concatenate([cos, cos, one], axis=1)
    s1_64 = jnp.concatenate([-sin, zh, zero], axis=1)
    s2_64 = jnp.concatenate([zh, sin, zero], axis=1)
    tile2 = lambda z: jnp.concatenate([z, z], axis=1)
    return tile2(c64), tile2(s1_64), tile2(s2_64)


def _stack_maps(q):
    lane = lax.broadcasted_iota(jnp.int32, q.shape, 1)
    z = jnp.zeros_like(q)
    return jnp.concatenate([jnp.where(lane < DQK, q, z), jnp.where(lane >= DQK, q, z)], axis=0)


def _online_softmax_steps(ss, vs, ms, ls, accs):
    def bcast(m, n):
        return jnp.tile(m, (1, n // LANES)) if n >= LANES else m[:, :n]

    m_news = [jnp.maximum(m, jnp.max(s, axis=-1, keepdims=True)) for s, m in zip(ss, ms)]
    alphas = [jnp.exp2(m - mn) for m, mn in zip(ms, m_news)]
    ps = [jnp.exp2(s - bcast(mn, s.shape[1])) for s, mn in zip(ss, m_news)]
    l_news = [a * l + jnp.sum(p, axis=-1, keepdims=True) for a, l, p in zip(alphas, ls, ps)]
    pvs = [jnp.dot(p.astype(BF16), v, preferred_element_type=F32) for p, v in zip(ps, vs)]
    acc_news = [a * acc + pv for a, acc, pv in zip(alphas, accs, pvs)]
    return m_news, l_news, acc_news


def _attn_finish(acc, l, lam, subln_g, gate, out_scale):
    t = acc.shape[0] // 2
    o = acc[:t] / l[:t] - lam * (acc[t:] / l[t:])
    ms = jnp.mean(o * o, axis=-1, keepdims=True)
    o = o * lax.rsqrt(ms + SUBLN_EPS) * subln_g * out_scale
    return (o * gate.astype(F32)).astype(BF16)


def _prompt_attn_kernel(lam_ref, q_ref, k_ref, v_ref, g_ref, sg_ref, o_ref,
                        q12_ref, m_ref, l_ref, acc_ref, *, tq, heads, out_scale):
    qi = pl.program_id(2)
    for hp in range(heads):
        q12_ref[hp] = _stack_maps(q_ref[:, hp * DV:(hp + 1) * DV])
    m_ref[...] = jnp.full_like(m_ref, NEG_BIG)
    l_ref[...] = jnp.zeros_like(l_ref)
    acc_ref[...] = jnp.zeros_like(acc_ref)

    def tile(kj, masked):
        rows = pl.ds(pl.multiple_of(kj * tq, tq), tq)
        hcols = [slice(hp * DV, (hp + 1) * DV) for hp in range(heads)]
        ss = [lax.dot_general(q12_ref[hp], k_ref[rows, hcols[hp]], (((1,), (1,)), ((), ())),
                              preferred_element_type=F32) for hp in range(heads)]
        if masked:
            r = lax.broadcasted_iota(jnp.int32, ss[0].shape, 0) % tq
            c = lax.broadcasted_iota(jnp.int32, ss[0].shape, 1)
            ss = [jnp.where(c <= r, s, NEG_BIG) for s in ss]
        ms, ls, accs = _online_softmax_steps(
            ss, [v_ref[rows, hcols[hp]] for hp in range(heads)],
            [m_ref[hp] for hp in range(heads)], [l_ref[hp] for hp in range(heads)],
            [acc_ref[hp] for hp in range(heads)])
        for hp in range(heads):
            m_ref[hp] = ms[hp]
            l_ref[hp] = ls[hp]
            acc_ref[hp] = accs[hp]

    def body(kj, carry):
        tile(kj, False)
        return carry

    lax.fori_loop(0, qi, body, 0)
    tile(qi, True)
    lam = lam_ref[0, 0]
    for hp in range(heads):
        cols = slice(hp * DV, (hp + 1) * DV)
        o_ref[:, cols] = _attn_finish(acc_ref[hp], l_ref[hp], lam, sg_ref[...], g_ref[:, cols], out_scale)


def _prompt_attention(q, k16, v16, gates, lam, subln_g, *, batch, seq, out_scale, tq=512, heads=4):
    tq = min(tq, seq)
    nq = seq // tq
    w = heads * DV
    kern = functools.partial(_prompt_attn_kernel, tq=tq, heads=heads, out_scale=out_scale)
    return pl.pallas_call(
        kern,
        grid=(batch, N_ATT_HEADS // heads, nq),
        in_specs=[
            pl.BlockSpec(memory_space=pltpu.SMEM),
            pl.BlockSpec((tq, w), lambda b, h, i: (b * nq + i, h)),
            pl.BlockSpec((seq, w), lambda b, h, i: (b, h)),
            pl.BlockSpec((seq, w), lambda b, h, i: (b, h)),
            pl.BlockSpec((tq, w), lambda b, h, i: (b * nq + i, h)),
            pl.BlockSpec((1, DV), lambda b, h, i: (0, 0)),
        ],
        out_specs=pl.BlockSpec((tq, w), lambda b, h, i: (b * nq + i, h)),
        out_shape=jax.ShapeDtypeStruct((batch * seq, ATT_W), BF16),
        scratch_shapes=[pltpu.VMEM((heads, 2 * tq, DV), BF16),
                        pltpu.VMEM((heads, 2 * tq, LANES), F32),
                        pltpu.VMEM((heads, 2 * tq, LANES), F32),
                        pltpu.VMEM((heads, 2 * tq, DV), F32)],
        compiler_params=_cparams(("parallel", "parallel", "arbitrary")),
        name="prompt_attention",
    )(lam, q, k16, v16, gates, subln_g.reshape(1, DV))


QUADS = 4
HEADS_PER_QUAD = N_ATT_HEADS // QUADS


def _sample_attn_body(j, n_steps, lam_ref, q_ref, refs, *, t_new, pages_per_step, out_scale):
    kc_refs = refs[:pages_per_step]
    vc_refs = refs[pages_per_step:2 * pages_per_step]
    kn_ref, vn_ref, g_ref, sg_ref, o_ref, q12_ref, m_ref, l_ref, acc_ref = refs[2 * pages_per_step:]
    rows = 2 * t_new
    qrows = HEADS_PER_QUAD * rows
    ncols = PAGE_SIZE * HEADS_PER_QUAD

    @pl.when(j == 0)
    def _():
        for h in range(N_ATT_HEADS):
            c, i = h % QUADS, h // QUADS
            q12_ref[c, i * rows:(i + 1) * rows, :] = _stack_maps(q_ref[0, :, h * DV:(h + 1) * DV])
        m_ref[...] = jnp.full_like(m_ref, NEG_BIG)
        l_ref[...] = jnp.zeros_like(l_ref)
        acc_ref[...] = jnp.zeros_like(acc_ref)

    yield "init"
    own = (lax.broadcasted_iota(jnp.int32, (qrows, ncols), 0) // rows
           == lax.broadcasted_iota(jnp.int32, (qrows, ncols), 1) % HEADS_PER_QUAD)
    nt = (((1,), (1,)), ((), ()))
    quads = range(QUADS)
    ms = [m_ref[c] for c in quads]
    ls = [l_ref[c] for c in quads]
    accs = [acc_ref[c] for c in quads]
    for u in range(pages_per_step):
        scores = [jnp.where(own, lax.dot_general(q12_ref[c], kc_refs[u][pl.ds(c, ncols, stride=QUADS), :].astype(BF16),
                                                 nt, preferred_element_type=F32), NEG_BIG) for c in quads]
        yield "scores"
        vals = [vc_refs[u][pl.ds(c, ncols, stride=QUADS), :].astype(BF16) for c in quads]
        ms, ls, accs = _online_softmax_steps(scores, vals, ms, ls, accs)
        yield "softmax"
    for c in quads:
        m_ref[c] = ms[c]
        l_ref[c] = ls[c]
        acc_ref[c] = accs[c]

    yield "final"

    @pl.when(j == n_steps - 1)
    def _():
        r = lax.broadcasted_iota(jnp.int32, (rows, t_new), 0) % t_new
        cc = lax.broadcasted_iota(jnp.int32, (rows, t_new), 1)
        causal = cc <= r
        lam = lam_ref[0, 0]
        for h in range(N_ATT_HEADS):
            c, i = h % QUADS, h // QUADS
            sl = slice(i * rows, (i + 1) * rows)
            cols = slice(h * DV, (h + 1) * DV)
            s = lax.dot_general(q12_ref[c, sl, :], kn_ref[0, :, cols], (((1,), (1,)), ((), ())),
                                preferred_element_type=F32)
            s = jnp.where(causal, s, NEG_BIG)
            _, (l,), (acc,) = _online_softmax_steps([s], [vn_ref[0, :, cols]], [m_ref[c, sl, :]],
                                                    [l_ref[c, sl, :]], [acc_ref[c, sl, :]])
            o_ref[0, :, cols] = _attn_finish(acc, l, lam, sg_ref[...], g_ref[0, :, cols], out_scale)


def _sample_attn_kernel(pt_ref, lam_ref, q_ref, *refs, **kw):
    _run(_sample_attn_body(pl.program_id(1), pl.num_programs(1), lam_ref, q_ref, refs, **kw))


def _sample_attn_call(q, k16, v16, gates, cache_k, cache_v, page_table, lam, subln_g, *, layer, out_scale,
                      pages_per_step=8):
    b, t_new, _ = q.shape
    n_pages = page_table.shape[1]
    n_pool = cache_k.shape[1]
    pps = math.gcd(pages_per_step, n_pages)
    page_rows = PAGE_SIZE * N_ATT_HEADS
    kc2 = cache_k.reshape(-1, DV)
    vc2 = cache_v.reshape(-1, DV)
    tok_spec = pl.BlockSpec((1, t_new, ATT_W), lambda i, j, pt: (i, 0, 0))

    def page_spec(u):
        return pl.BlockSpec((page_rows, DV), lambda i, j, pt: (layer * n_pool + pt[i, j * pps + u], 0))

    page_specs = [page_spec(u) for u in range(pps)]
    qrows = HEADS_PER_QUAD * 2 * t_new
    return dict(
        grid=(b, n_pages // pps),
        kw=dict(t_new=t_new, pages_per_step=pps, out_scale=out_scale),
        in_specs=[pl.BlockSpec(memory_space=pltpu.SMEM), tok_spec] + page_specs + page_specs
                 + [tok_spec, tok_spec, tok_spec, pl.BlockSpec((1, DV), lambda i, j, pt: (0, 0))],
        args=[lam, q] + [kc2] * pps + [vc2] * pps + [k16, v16, gates, subln_g.reshape(1, DV)],
        out_specs=[tok_spec],
        out_shape=[jax.ShapeDtypeStruct((b, t_new, ATT_W), BF16)],
        scratch=[pltpu.VMEM((QUADS, qrows, DV), BF16),
                 pltpu.VMEM((QUADS, qrows, LANES), F32),
                 pltpu.VMEM((QUADS, qrows, LANES), F32),
                 pltpu.VMEM((QUADS, qrows, DV), F32)],
    )


def _sample_attention(page_table, call):
    grid_spec = pltpu.PrefetchScalarGridSpec(
        num_scalar_prefetch=1, grid=call["grid"], in_specs=call["in_specs"], out_specs=call["out_specs"],
        scratch_shapes=call["scratch"])
    return pl.pallas_call(
        functools.partial(_sample_attn_kernel, **call["kw"]),
        grid_spec=grid_spec,
        out_shape=call["out_shape"],
        compiler_params=_cparams(("parallel", "arbitrary")),
        name="sample_attention",
    )(page_table, *call["args"])[0]


def _stack_heads(x):
    lane = lax.broadcasted_iota(jnp.int32, x.shape, 1)
    z = jnp.zeros_like(x)
    return jnp.concatenate([jnp.where(lane < RWKV_HEAD, x, z), jnp.where(lane >= RWKV_HEAD, x, z)], axis=0)


def _each(f, *lists):
    return [f(*args) for args in zip(*lists)]


def _mm(a, b):
    return jnp.dot(a, b, preferred_element_type=F32)


def _rwkv_chunk(rs, krs, vs, lo, wls, w0s, a0s, k_ks, k_as, r_ks, lnx_gs, lnx_bs, gates, h0s, consts):
    ones, tril, strict, incl, eye = consts
    C = lo.shape[0]
    n2 = 2 * C
    lane = lax.broadcasted_iota(jnp.int32, (C, LANES), 1)
    lo_act = jnp.where(lane < LORA, jnp.tanh(lo), lo).astype(BF16)
    pres = [_mm(lo_act, wl) for wl in wls]
    logws = _each(lambda pre, w0: -math.exp(-0.5) * jax.nn.sigmoid(w0 + pre[:, :LANES]), pres, w0s)
    a_sigs = _each(lambda pre, a0: jax.nn.sigmoid(a0 + pre[:, LANES:]), pres, a0s)
    yield "stage"

    kks = _each(lambda kr, k_k: kr * k_k, krs, k_ks)
    kk_ss = _segment_sums([kk * kk for kk in kks], ones)
    kks = _each(lambda kk, ss: kk / jnp.maximum(jnp.sqrt(ss), 1e-12), kks, kk_ss)
    k_hs = _each(lambda kr, a_sig, k_a: kr * (1.0 + (a_sig - 1.0) * k_a), krs, a_sigs, k_as)
    rk_ss = _segment_sums([r * k_h * r_k for r, k_h, r_k in zip(rs, k_hs, r_ks)], ones)
    bonuses = _each(lambda ss, v: ss * v, rk_ss, vs)
    yield "stage"

    def split3(x):
        l1 = x.astype(BF16)
        rem = x - l1.astype(F32)
        l2 = rem.astype(BF16)
        return jnp.concatenate([l1, l2, (rem - l2.astype(F32)).astype(BF16)], axis=0)

    cums = [_mm(tril, split3(logw)) for logw in logws]
    yield "stage"
    g_incls = [jnp.exp(cum) for cum in cums]
    g_excls = _each(lambda cum, logw: jnp.exp(cum - logw), cums, logws)
    g_invs = [jnp.exp(-cum) for cum in cums]
    g_lasts = [g[C - 1:C, :] for g in g_incls]

    a_ts = _each(lambda kk, g: _stack_heads(-kk * g).astype(BF16), kks, g_excls)
    r_ts = _each(lambda r, g: _stack_heads(r * g).astype(BF16), rs, g_incls)
    v_sts = [_stack_heads(v).astype(BF16) for v in vs]
    bks = _each(lambda kk, a_sig, k_h, g: jnp.concatenate([_stack_heads(kk * a_sig * g), _stack_heads(k_h * g)],
                                                          axis=0), kks, a_sigs, k_hs, g_invs)
    gmats = _each(lambda a_t, r_t, bk: lax.dot_general(jnp.concatenate([a_t, r_t], axis=0), bk.astype(BF16),
                                                       (((1,), (1,)), ((), ())), preferred_element_type=F32),
                  a_ts, r_ts, bks)
    yield "stage"
    a_abs = [jnp.where(strict, g[:n2, :n2], 0.0) for g in gmats]
    a_aks = [jnp.where(strict, g[:n2, n2:], 0.0).astype(BF16) for g in gmats]
    a_rbs = [jnp.where(incl, g[n2:, :n2], 0.0).astype(BF16) for g in gmats]
    a_rks = [jnp.where(incl, g[n2:, n2:], 0.0).astype(BF16) for g in gmats]
    yield "stage"

    eye2 = jnp.where(lax.broadcasted_iota(jnp.int32, (n2, n2), 0)
                     == lax.broadcasted_iota(jnp.int32, (n2, n2), 1), 1.0, 0.0)
    n_factors = int(math.log2(C))
    pows = [a_abs]
    pows16 = [[a.astype(BF16) for a in a_abs]]
    for _ in range(n_factors - 1):
        pows.append([_mm(pw, pw) for pw in pows16[-1]])
        pows16.append([x.astype(BF16) for x in pows[-1]])
        yield "stage"
    prods = [_each(lambda x, y, x16, y16: eye2 + x + y + _mm(x16, y16), pows[i], pows[i + 1], pows16[i], pows16[i + 1])
             for i in range(0, n_factors - 1, 2)]
    if n_factors % 2:
        prods.append([eye2 + x for x in pows[-1]])
    yield "stage"
    while len(prods) > 1:
        nxt = [_each(lambda a, b: _mm(a.astype(BF16), b.astype(BF16)), prods[i], prods[i + 1])
               for i in range(0, len(prods) - 1, 2)]
        prods = nxt + ([prods[-1]] if len(prods) % 2 else [])
        yield "stage"
    t_invs = prods[0]

    h16s = [h0.astype(BF16) for h0 in h0s]
    if n2 % LANES == 0:
        cat = jnp.concatenate
        xs = _each(lambda a_t, h16, a_ak, v_st: _mm(cat([a_t, a_ak], axis=1), cat([h16, v_st], axis=0)),
                   a_ts, h16s, a_aks, v_sts)
    else:
        xs = _each(lambda a_t, h16, a_ak, v_st: _mm(a_t, h16) + _mm(a_ak, v_st), a_ts, h16s, a_aks, v_sts)
    u16s = _each(lambda t, x: _mm(t.astype(BF16), x.astype(BF16)).astype(BF16), t_invs, xs)
    yield "stage"
    if n2 % LANES == 0:
        o_sts = _each(lambda r_t, h16, a_rb, u16, a_rk, v_st:
                      _mm(cat([r_t, a_rb, a_rk], axis=1), cat([h16, u16, v_st], axis=0)),
                      r_ts, h16s, a_rbs, u16s, a_rks, v_sts)
    else:
        o_sts = _each(lambda r_t, h16, a_rb, u16, a_rk, v_st: _mm(r_t, h16) + _mm(a_rb, u16) + _mm(a_rk, v_st),
                      r_ts, h16s, a_rbs, u16s, a_rks, v_sts)
    os_ = [o_st[:C] + o_st[C:] for o_st in o_sts]
    yield "stage"

    bk_ts = _each(lambda bk, g_last: (bk * g_last).T.astype(BF16), bks, g_lasts)
    g_cols = [jnp.sum(jnp.where(eye, jnp.broadcast_to(g_last, (LANES, LANES)), 0.0), axis=1, keepdims=True)
              for g_last in g_lasts]
    h_news = _each(lambda g_col, h0, bk_t, u16, v_st: g_col * h0 + _mm(bk_t, jnp.concatenate([u16, v_st], axis=0)),
                   g_cols, h0s, bk_ts, u16s, v_sts)
    yield "stage"

    mus = [z * (1.0 / RWKV_HEAD) for z in _segment_sums(os_, ones)]
    ds = _each(lambda o, mu: o - mu, os_, mus)
    vars_ = [z * (1.0 / RWKV_HEAD) for z in _segment_sums([d * d for d in ds], ones)]
    ys = _each(lambda d, var, g, b, bonus, gate:
               (((d * lax.rsqrt(var + LNX_EPS) * g + b) + bonus) * gate.astype(F32)).astype(BF16),
               ds, vars_, lnx_gs, lnx_bs, bonuses, gates)
    return ys, h_news


def _rwkv_body(c, n_chunks, refs, *, chunk, pairs, use_s0):
    (r_ref, k_ref, v_ref, lo_ref, pr_ref, pk_ref, pv_ref, plo_ref,
     tmr_ref, tmk_ref, tmv_ref, tmlo_ref, wl_ref, w0_ref, a0_ref, kk_ref, ka_ref, rk_ref,
     lg_ref, lb_ref, gate_ref, s0_ref, y_ref, sout_ref,
     h_ref, cr_ref, ck_ref, cv_ref, clo_ref) = refs
    C = chunk
    n2 = 2 * C

    @pl.when(c == 0)
    def _():
        cr_ref[...] = pr_ref[0]
        ck_ref[...] = pk_ref[0]
        cv_ref[...] = pv_ref[0]
        clo_ref[...] = plo_ref[0]
        if use_s0:
            z = jnp.zeros((RWKV_HEAD, RWKV_HEAD), F32)
            for u in range(pairs):
                st = jnp.concatenate([jnp.concatenate([s0_ref[0, 2 * u], z], axis=1),
                                      jnp.concatenate([z, s0_ref[0, 2 * u + 1]], axis=1)], axis=0)
                h_ref[u] = st.T
        else:
            h_ref[...] = jnp.zeros_like(h_ref)

    yield "init"
    ri = lax.broadcasted_iota(jnp.int32, (n2, n2), 0)
    ci = lax.broadcasted_iota(jnp.int32, (n2, n2), 1)
    tril = (lax.broadcasted_iota(jnp.int32, (C, 3 * C), 1) % C <= lax.broadcasted_iota(jnp.int32, (C, 3 * C), 0))
    eye = (lax.broadcasted_iota(jnp.int32, (LANES, LANES), 0)
           == lax.broadcasted_iota(jnp.int32, (LANES, LANES), 1))
    consts = (_half_ones(), jnp.where(tril, 1.0, 0.0).astype(BF16), ri > ci, ri >= ci, eye)

    def token_shift(cur_ref3, carry_ref, tm_ref):
        cur = cur_ref3[0]
        row0 = lax.broadcasted_iota(jnp.int32, cur.shape, 0) == 0
        prev = jnp.where(row0, carry_ref[...], pltpu.roll(cur, 1, 0))
        carry_ref[...] = cur[C - 1:C, :]
        return cur + tm_ref[...] * (prev - cur)

    lo = token_shift(lo_ref, clo_ref, tmlo_ref)
    r_all = token_shift(r_ref, cr_ref, tmr_ref)
    k_all = token_shift(k_ref, ck_ref, tmk_ref)
    v_all = token_shift(v_ref, cv_ref, tmv_ref)
    col = lambda u: slice(u * LANES, (u + 1) * LANES)
    per_pair = lambda x: [x[:, col(u)] for u in range(pairs)]
    ys, h_news = yield from _rwkv_chunk(
        per_pair(r_all), per_pair(k_all), per_pair(v_all), lo, [wl_ref[u] for u in range(pairs)],
        per_pair(w0_ref[...]), per_pair(a0_ref[...]), per_pair(kk_ref[...]), per_pair(ka_ref[...]),
        per_pair(rk_ref[...]), per_pair(lg_ref[...]), per_pair(lb_ref[...]), per_pair(gate_ref[0]),
        [h_ref[u] for u in range(pairs)], consts)
    for u in range(pairs):
        y_ref[0, :, col(u)] = ys[u]
        h_ref[u] = h_news[u]

    yield "final"

    @pl.when(c == n_chunks - 1)
    def _():
        for u in range(pairs):
            st = h_news[u].T
            sout_ref[0, 2 * u] = st[:RWKV_HEAD, :RWKV_HEAD]
            sout_ref[0, 2 * u + 1] = st[RWKV_HEAD:, RWKV_HEAD:]


def _rwkv_kernel(*refs, **kw):
    _run(_rwkv_body(pl.program_id(2), pl.num_programs(2), refs, **kw))


def _rwkv_call(rkv, lora, gates, prevs, s0, p, *, chunk, pairs):
    b, t, _ = lora.shape
    n_chunks = t // chunk
    ng = N_RWKV_PAIRS // pairs
    w = pairs * LANES
    use_s0 = s0 is not None
    sblk = (1, 2 * pairs, RWKV_HEAD, RWKV_HEAD)
    if s0 is None:
        s0 = jnp.zeros(sblk, F32)
        s0_spec = pl.BlockSpec(sblk, lambda i, j, c: (0, 0, 0, 0))
    else:
        s0_spec = pl.BlockSpec(sblk, lambda i, j, c: (i, j, 0, 0))

    tok = pl.BlockSpec((1, chunk, w), lambda i, j, c: (i, c, j))
    prev = pl.BlockSpec((1, 1, w), lambda i, j, c: (i, 0, j))

    def par():
        return pl.BlockSpec((1, w), lambda i, j, c: (0, j))

    return dict(
        grid=(b, ng, n_chunks),
        kw=dict(chunk=chunk, pairs=pairs, use_s0=use_s0),
        in_specs=[
            tok, tok, tok,
            pl.BlockSpec((1, chunk, LANES), lambda i, j, c: (i, c, 0)),
            prev, prev, prev,
            pl.BlockSpec((1, 1, LANES), lambda i, j, c: (i, 0, 0)),
            par(), par(), par(), pl.BlockSpec((1, LANES), lambda i, j, c: (0, 0)),
            pl.BlockSpec((pairs, LANES, 2 * LANES), lambda i, j, c: (j, 0, 0)),
            par(), par(), par(), par(), par(), par(), par(),
            pl.BlockSpec((1, chunk, w), lambda i, j, c: (i, c, ng + j)),
            s0_spec,
        ],
        args=[*rkv, lora, *prevs, p["tm_r"], p["tm_k"], p["tm_v"], p["tm_lora"], p["w_lora"],
              p["w0"], p["a0"], p["k_k"], p["k_a"], p["r_k"], p["lnx_g"], p["lnx_b"], gates, s0],
        out_specs=[pl.BlockSpec((1, chunk, w), lambda i, j, c: (i, c, j)),
                   pl.BlockSpec(sblk, lambda i, j, c: (i, j, 0, 0))],
        out_shape=[jax.ShapeDtypeStruct((b, t, RWKV_W), BF16),
                   jax.ShapeDtypeStruct((b, N_RWKV_HEADS, RWKV_HEAD, RWKV_HEAD), F32)],
        scratch=[pltpu.VMEM((pairs, LANES, LANES), F32),
                 pltpu.VMEM((1, w), F32), pltpu.VMEM((1, w), F32),
                 pltpu.VMEM((1, w), F32), pltpu.VMEM((1, LANES), F32)],
    )


def _rwkv(call):
    return pl.pallas_call(
        functools.partial(_rwkv_kernel, **call["kw"]),
        grid=call["grid"],
        in_specs=call["in_specs"],
        out_specs=call["out_specs"],
        out_shape=call["out_shape"],
        scratch_shapes=call["scratch"],
        compiler_params=_cparams(("parallel", "parallel", "arbitrary")),
        name="rwkv7_chunked",
    )(*call["args"])


def _remap(spec, decode):
    if spec.block_shape is None:
        return spec
    return pl.BlockSpec(spec.block_shape, lambda *a: spec.index_map(*decode(*a)))


def _fused_kernel(pt_ref, *refs, n_in, n_out, n_scr, rwkv_kw, attn_kw, n_chunks, attn_steps):
    (ri, ai), (ro, ao), (rs, as_) = n_in, n_out, n_scr
    r_in, a_in = refs[:ri], refs[ri:ri + ai]
    r_out, a_out = refs[ri + ai:ri + ai + ro], refs[ri + ai + ro:ri + ai + ro + ao]
    r_scr, a_scr = refs[ri + ai + ro + ao:ri + ai + ro + ao + rs], refs[ri + ai + ro + ao + rs:]
    s = pl.program_id(0)
    _interleave(
        _rwkv_body(s % n_chunks, n_chunks, (*r_in, *r_out, *r_scr), **rwkv_kw),
        _sample_attn_body(s % attn_steps, attn_steps, a_in[0], a_in[1], (*a_in[2:], *a_out, *a_scr), **attn_kw))


def _rwkv_with_sample_attention(rwkv_call, attn_call, page_table):
    b_r, ng, nc = rwkv_call["grid"]
    b_a, nj = attn_call["grid"]
    n_steps = b_r * ng * nc
    assert n_steps == b_a * nj
    dec_r = lambda s, pt: (s // (ng * nc), (s // nc) % ng, s % nc)
    dec_a = lambda s, pt: (s // nj, s % nj, pt)
    in_specs = [_remap(sp, dec_r) for sp in rwkv_call["in_specs"]] + [_remap(sp, dec_a) for sp in attn_call["in_specs"]]
    out_specs = [_remap(sp, dec_r) for sp in rwkv_call["out_specs"]] + [_remap(sp, dec_a) for sp in attn_call["out_specs"]]
    kern = functools.partial(
        _fused_kernel,
        n_in=(len(rwkv_call["in_specs"]), len(attn_call["in_specs"])),
        n_out=(len(rwkv_call["out_specs"]), len(attn_call["out_specs"])),
        n_scr=(len(rwkv_call["scratch"]), len(attn_call["scratch"])),
        rwkv_kw=rwkv_call["kw"], attn_kw=attn_call["kw"], n_chunks=nc, attn_steps=nj)
    grid_spec = pltpu.PrefetchScalarGridSpec(
        num_scalar_prefetch=1, grid=(n_steps,), in_specs=in_specs, out_specs=out_specs,
        scratch_shapes=rwkv_call["scratch"] + attn_call["scratch"])
    y_rwkv, s_out, y_att = pl.pallas_call(
        kern,
        grid_spec=grid_spec,
        out_shape=rwkv_call["out_shape"] + attn_call["out_shape"],
        compiler_params=_cparams(("arbitrary",), FUSED_VMEM_LIMIT_BYTES),
        name="rwkv7_and_paged_attention",
    )(page_table, *rwkv_call["args"], *attn_call["args"])
    return y_rwkv, s_out, y_att


def _prep_layer_params(l, ln_g, w_in, q_norm_g, k_norm_g, lambda_q1, lambda_k1, lambda_q2, lambda_k2,
                       subln_g, time_mix, w0, w2, a0, a2, k_k, k_a, r_k, lnx_g, lnx_b, w_out):
    w = w_in[l]
    sh = OFF_SHIFT
    o1, o2, o3, o4 = RWKV_W, RWKV_W + LORA, 2 * RWKV_W + LORA, 3 * RWKV_W + LORA
    w16 = w.astype(BF16)
    w_qkvr = w16
    w_rk = w16[:, sh + o2:sh + o3]
    w_rv = w16[:, sh + o3:sh + o4]
    w_gates = w16[:, OFF_GA:]
    w_lora_in = jnp.concatenate([w16[:, sh + o1:sh + o2], w16[:, sh + o4:sh + SHIFT_W]], axis=1)
    tmix = time_mix[l]
    tm_lora = jnp.concatenate([tmix[o1:o2], tmix[o4:]]).reshape(1, 2 * LORA)
    w2p = w2[l].reshape(LORA, N_RWKV_PAIRS, LANES).transpose(1, 0, 2)
    a2p = a2[l].reshape(LORA, N_RWKV_PAIRS, LANES).transpose(1, 0, 2)
    z = jnp.zeros_like(w2p)
    w_lora = jnp.concatenate([jnp.concatenate([w2p, z], axis=2), jnp.concatenate([z, a2p], axis=2)],
                             axis=1).astype(BF16)
    lam_init = 0.8 - 0.6 * math.exp(-0.3 * l)
    lam = (jnp.exp(jnp.sum(lambda_q1[l] * lambda_k1[l])) - jnp.exp(jnp.sum(lambda_q2[l] * lambda_k2[l]))
           + lam_init).reshape(1, 1).astype(F32)
    row = lambda z_: z_.reshape(1, -1)
    return dict(
        layer=l, ln_g=ln_g[l], w_qkvr=w_qkvr, w_rk=w_rk, w_rv=w_rv, w_gates=w_gates, w_lora_in=w_lora_in,
        w_out=w_out[l].astype(BF16),
        q_gain=jnp.tile(q_norm_g[l], 2).reshape(1, DV), k_gain=jnp.tile(k_norm_g[l], 2).reshape(1, DV),
        lam=lam, out_scale=1.0 - lam_init, subln_g=subln_g[l],
        tm_r=row(tmix[:o1]), tm_k=row(tmix[o2:o3]), tm_v=row(tmix[o3:o4]), tm_lora=tm_lora, w_lora=w_lora,
        w0=row(w0[l]), a0=row(a0[l]), k_k=row(k_k[l]), k_a=row(k_a[l]), r_k=row(r_k[l]),
        lnx_g=row(lnx_g[l]), lnx_b=row(lnx_b[l]),
    )


def _project(x, pos, p):
    b, t, d = x.shape
    m = b * t
    x2 = x.reshape(m, d)
    h = _rmsnorm(x2, p["ln_g"])

    tm = min(1024, m)
    cos, s1, s2 = _rope_tables(pos)
    if t < tm:
        reps = tm // t
        cos, s1, s2 = (jnp.tile(z, (reps, 1)) for z in (cos, s1, s2))
    gain_spec = pl.BlockSpec((1, LANES), lambda i, j: (0, 0))
    blk = lambda cols: cols // LANES
    tm_hm, tn_hm = min(512, m), 1024

    def qk(col_off, gain, scale, dts):
        tm_, tn_ = (tm_hm, tn_hm) if "f32_head_major" in dts else (tm, 512)
        n_tab = cos.shape[0] // tm_
        tab_spec = pl.BlockSpec((tm_, LANES), lambda i, j: (i % n_tab, 0))
        kern = functools.partial(_mm_qk_kernel, scale=scale, rows_per_chunk=min(256, tm_))
        return _matmul([h], p["w_qkvr"], col_block_off=blk(col_off), n_cols=ATT_W, kernel=kern,
                       out_dtypes=dts, extra=(gain, cos, s1, s2),
                       extra_specs=(gain_spec, tab_spec, tab_spec, tab_spec), scratch=True, tm=tm_, tn=tn_)

    def plain(w, col_off, n_cols):
        return _matmul([h], w, col_block_off=blk(col_off), n_cols=n_cols, kernel=_mm_plain_kernel,
                       out_dtypes=(F32,), tm=tm, tn=1024)[0]

    (q16,) = qk(0, p["q_gain"], DQK ** -0.5 * math.log2(math.e), (BF16,))
    k32, k16 = qk(OFF_K, p["k_gain"], 1.0, ("f32_head_major", BF16))
    v32, v16 = _matmul([h], p["w_qkvr"], col_block_off=blk(OFF_V), n_cols=ATT_W,
                       kernel=_mm_dual_kernel, out_dtypes=("f32_head_major", BF16), tm=tm_hm, tn=tn_hm)
    rkv = [plain(p["w_qkvr"], OFF_SHIFT, RWKV_W).reshape(b, t, RWKV_W),
           plain(p["w_rk"], 0, RWKV_W).reshape(b, t, RWKV_W),
           plain(p["w_rv"], 0, RWKV_W).reshape(b, t, RWKV_W)]
    lora = plain(p["w_lora_in"], 0, 2 * LORA).reshape(b, t, 2 * LORA)
    (gates,) = _matmul([h], p["w_gates"], col_block_off=0, n_cols=ATT_W + RWKV_W,
                       kernel=_mm_silu_kernel, out_dtypes=(BF16,), tm=tm, tn=1024)
    return dict(b=b, t=t, m=m, tm=tm, x2=x2, q16=q16, k32=k32, k16=k16, v32=v32, v16=v16, rkv=rkv, lora=lora,
                gates=gates)


def _rwkv_call_for(pr, shift_prev, wkv_prev, p, *, chunk, pairs):
    b = pr["b"]
    o1, o2, o3, o4 = RWKV_W, RWKV_W + LORA, 2 * RWKV_W + LORA, 3 * RWKV_W + LORA
    prev_lora = jnp.concatenate([shift_prev[:, o1:o2], shift_prev[:, o4:]], axis=1)
    prevs = [z.reshape(b, 1, -1) for z in (shift_prev[:, :o1], shift_prev[:, o2:o3], shift_prev[:, o3:o4],
                                           prev_lora)]
    return _rwkv_call(pr["rkv"], pr["lora"], pr["gates"].reshape(b, pr["t"], -1), prevs, wkv_prev, p,
                      chunk=chunk, pairs=pairs)


def _finish_layer(pr, y_att, y_rwkv, p):
    b, t, m, tm = pr["b"], pr["t"], pr["m"], pr["tm"]
    d = pr["x2"].shape[1]
    last_lora = pr["lora"][:, -1]
    shift_new = jnp.concatenate([pr["rkv"][0][:, -1], last_lora[:, :LORA], pr["rkv"][1][:, -1],
                                 pr["rkv"][2][:, -1], last_lora[:, LORA:]], axis=1)
    tn_out = 512
    x_spec = pl.BlockSpec((tm, tn_out), lambda i, j: (i, j))
    (out,) = _matmul([y_att.reshape(m, ATT_W), y_rwkv.reshape(m, RWKV_W)], p["w_out"], col_block_off=0,
                     n_cols=d, kernel=_mm_residual_kernel, out_dtypes=(F32,), extra=(pr["x2"],),
                     extra_specs=(x_spec,), tm=tm, tn=tn_out)
    return (out.reshape(b, t, d), pr["k32"].reshape(b, t, N_ATT_HEADS, DV),
            pr["v32"].reshape(b, t, N_ATT_HEADS, DV), shift_new)


def _layer_pair(xp, xs, pos_p, pos_s, shift_s, wkv_s, cache_k, cache_v, page_table, p, *,
                chunk_p=64, pairs_p=8, pairs_s=16, pages_per_step=8):
    pp = _project(xp, pos_p, p)
    ps = _project(xs, pos_s, p)
    b_s, t_s = ps["b"], ps["t"]
    rw_p = _rwkv_call_for(pp, jnp.zeros((pp["b"], SHIFT_W), F32), None, p, chunk=chunk_p, pairs=pairs_p)
    rw_s = _rwkv_call_for(ps, shift_s, wkv_s, p, chunk=t_s, pairs=pairs_s)
    at_s = _sample_attn_call(ps["q16"].reshape(b_s, t_s, ATT_W), ps["k16"].reshape(b_s, t_s, ATT_W),
                             ps["v16"].reshape(b_s, t_s, ATT_W), ps["gates"].reshape(b_s, t_s, -1),
                             cache_k, cache_v, page_table, p["lam"], p["subln_g"], layer=p["layer"],
                             out_scale=p["out_scale"], pages_per_step=pages_per_step)
    if math.prod(rw_p["grid"]) == math.prod(at_s["grid"]):
        y_rwkv_p, wkv_p, y_att_s = _rwkv_with_sample_attention(rw_p, at_s, page_table)
    else:
        y_rwkv_p, wkv_p = _rwkv(rw_p)
        y_att_s = _sample_attention(page_table, at_s)
    y_att_p = _prompt_attention(pp["q16"], pp["k16"], pp["v16"], pp["gates"], p["lam"], p["subln_g"],
                                batch=pp["b"], seq=pp["t"], out_scale=p["out_scale"])
    y_rwkv_s, wkv_s_new = _rwkv(rw_s)
    return (*_finish_layer(pp, y_att_p, y_rwkv_p, p), wkv_p), (*_finish_layer(ps, y_att_s, y_rwkv_s, p), wkv_s_new)


def kernel(x_prompt, x_sample, cache_k, cache_v, state_shift, state_wkv, page_table, ln_g, w_in, q_norm_g,
           k_norm_g, lambda_q1, lambda_k1, lambda_q2, lambda_k2, subln_g, time_mix, w0, w2, a0, a2, k_k,
           k_a, r_k, lnx_g, lnx_b, w_out):
    depth = w_in.shape[0]
    s_p, s_s = x_prompt.shape[1], x_sample.shape[1]
    past_len = page_table.shape[1] * PAGE_SIZE
    pos_p = jnp.arange(s_p)
    pos_s = past_len + jnp.arange(s_s)
    yp, ys = x_prompt, x_sample
    outs = [[] for _ in range(8)]
    for l in range(depth):
        p = _prep_layer_params(l, ln_g, w_in, q_norm_g, k_norm_g, lambda_q1, lambda_k1, lambda_q2,
                               lambda_k2, subln_g, time_mix, w0, w2, a0, a2, k_k, k_a, r_k, lnx_g, lnx_b,
                               w_out)
        (yp, kp, vp, shp, wkvp), (ys, ks, vs, shs, wkvs) = _layer_pair(
            yp, ys, pos_p, pos_s, state_shift[l], state_wkv[l], cache_k, cache_v, page_table, p)
        for lst, val in zip(outs, (kp, vp, shp, wkvp, ks, vs, shs, wkvs)):
            lst.append(val)
    stacked = [jnp.stack(lst, 0) for lst in outs]
    return (yp, ys, *stacked)
```

```python
import functools
import math

import jax
import jax.numpy as jnp
from jax import lax
from jax.experimental import pallas as pl
from jax.experimental.pallas import tpu as pltpu

F32 = jnp.float32
BF16 = jnp.bfloat16

D_MODEL = 4096
PAGE_SIZE = 128
ATT_W = 2048
RWKV_W = 2048
DV = 128
N_ATT_HEADS = ATT_W // DV
DQK = DV // 2
ROT_DIM = DQK // 4
ROPE_THETA = 500000.0
RWKV_HEAD = 64
N_RWKV_HEADS = RWKV_W // RWKV_HEAD
N_RWKV_PAIRS = N_RWKV_HEADS // 2
LORA = 64
SHIFT_W = 3 * RWKV_W + 2 * LORA
OFF_K = ATT_W
OFF_V = 2 * ATT_W
OFF_SHIFT = 3 * ATT_W
OFF_GA = OFF_SHIFT + SHIFT_W
OFF_GR = OFF_GA + ATT_W
IN_W = OFF_GR + RWKV_W
NORM_EPS = 1e-6
SUBLN_EPS = 1e-5
LNX_EPS = 64e-5

LANES = 128
VMEM_LIMIT_BYTES = 48 * 1024 * 1024
FUSED_VMEM_LIMIT_BYTES = 56 * 1024 * 1024
NEG_BIG = -1e30


def _cparams(sem, vmem_limit_bytes=VMEM_LIMIT_BYTES):
    return pltpu.CompilerParams(dimension_semantics=sem, vmem_limit_bytes=vmem_limit_bytes)


def _run(staged_body):
    for _ in staged_body:
        pass


def _interleave(*staged_bodies):
    active, finals = list(staged_bodies), []
    while active:
        for g in list(active):
            tag = next(g, "done")
            if tag == "done":
                active.remove(g)
            elif tag == "final":
                active.remove(g)
                finals.append(g)
    for g in finals:
        _run(g)


def _split_dot(x, w2_bf16):
    hi = x.astype(BF16)
    lo = (x - hi.astype(F32)).astype(BF16)
    return jnp.dot(jnp.concatenate([hi, lo], axis=1), w2_bf16, preferred_element_type=F32)


def _segment_sums(xs, w2_bf16):
    rows = xs[0].shape[0]
    out = _split_dot(jnp.concatenate(xs, axis=0), w2_bf16)
    return [out[i * rows:(i + 1) * rows] for i in range(len(xs))]


def _half_ones():
    r = (lax.broadcasted_iota(jnp.int32, (2 * LANES, LANES), 0) % LANES) // RWKV_HEAD
    c = lax.broadcasted_iota(jnp.int32, (2 * LANES, LANES), 1) // RWKV_HEAD
    return jnp.where(r == c, 1.0, 0.0).astype(BF16)


def _rmsnorm_kernel(x_ref, g_ref, o_ref):
    rows_per_iter = 16

    def body(i, carry):
        rows = pl.ds(pl.multiple_of(i * rows_per_iter, rows_per_iter), rows_per_iter)
        x = x_ref[rows, :]
        ms = jnp.mean(x * x, axis=-1, keepdims=True)
        o_ref[rows, :] = (x * lax.rsqrt(ms + NORM_EPS) * g_ref[...]).astype(BF16)
        return carry

    n_iter = x_ref.shape[0] // rows_per_iter
    lax.fori_loop(0, n_iter, body, 0, unroll=math.gcd(n_iter, 8))


def _rmsnorm(x2, g, tm=512):
    m, d = x2.shape
    tm = min(tm, m)
    return pl.pallas_call(
        _rmsnorm_kernel,
        grid=(m // tm,),
        in_specs=[pl.BlockSpec((tm, d), lambda i: (i, 0)),
                  pl.BlockSpec((1, d), lambda i: (0, 0))],
        out_specs=pl.BlockSpec((tm, d), lambda i: (i, 0)),
        out_shape=jax.ShapeDtypeStruct((m, d), BF16),
        compiler_params=_cparams(("parallel",)),
        name="rmsnorm",
    )(x2, g.reshape(1, d))


def _mm_full_k(lhs_refs, w_ref):
    acc, off = None, 0
    for l_ref in lhs_refs:
        k = l_ref.shape[1]
        part = jnp.dot(l_ref[...], w_ref[off:off + k, :], preferred_element_type=F32)
        acc = part if acc is None else acc + part
        off += k
    return acc


def _mm_plain_kernel(h_ref, w_ref, o_ref):
    o_ref[...] = _mm_full_k([h_ref], w_ref).astype(o_ref.dtype)


def _mm_dual_kernel(h_ref, w_ref, o32_ref, o16_ref):
    a = _mm_full_k([h_ref], w_ref)
    o16_ref[...] = a.astype(BF16)
    for hh in range(a.shape[1] // LANES):
        o32_ref[:, hh, :] = a[:, hh * LANES:(hh + 1) * LANES]


def _mm_silu_kernel(h_ref, w_ref, o_ref):
    a = _mm_full_k([h_ref], w_ref)
    o_ref[...] = (a * jax.nn.sigmoid(a)).astype(o_ref.dtype)


def _mm_residual_kernel(ya_ref, yr_ref, w_ref, x_ref, o_ref):
    o_ref[...] = x_ref[...] + _mm_full_k([ya_ref, yr_ref], w_ref)


def _mm_qk_kernel(h_ref, w_ref, g_ref, cos_ref, s1_ref, s2_ref, *rest, scale, rows_per_chunk):
    outs, acc_ref = rest[:-1], rest[-1]
    acc_ref[...] = _mm_full_k([h_ref], w_ref)
    tm, tn = acc_ref.shape
    ones = _half_ones()
    gain = g_ref[...]

    def chunk(r, carry):
        rows = pl.ds(pl.multiple_of(r * rows_per_chunk, rows_per_chunk), rows_per_chunk)
        c = cos_ref[rows, :]
        s1 = s1_ref[rows, :]
        s2 = s2_ref[rows, :]
        for hh in range(tn // LANES):
            cols = slice(hh * LANES, (hh + 1) * LANES)
            x = acc_ref[rows, cols]
            ss = _split_dot(x * x, ones)
            y = x * lax.rsqrt(ss * (1.0 / DQK) + NORM_EPS) * gain
            y = y * c + pltpu.roll(y, LANES - ROT_DIM // 2, 1) * s1 + pltpu.roll(y, ROT_DIM // 2, 1) * s2
            if scale != 1.0:
                y = y * scale
            for o_ref in outs:
                if len(o_ref.shape) == 3:
                    o_ref[rows, hh, :] = y.astype(o_ref.dtype)
                else:
                    o_ref[rows, cols] = y.astype(o_ref.dtype)
        return carry

    lax.fori_loop(0, tm // rows_per_chunk, chunk, 0)


def _matmul(lhs, w, *, col_block_off, n_cols, kernel, out_dtypes, extra=(), extra_specs=(),
            scratch=False, tm=1024, tn=512):
    m = lhs[0].shape[0]
    tm = min(tm, m)
    tn = min(tn, n_cols)
    assert m % tm == 0 and n_cols % tn == 0 and sum(l.shape[1] for l in lhs) == w.shape[0]
    assert (col_block_off * LANES) % tn == 0
    joff = col_block_off * LANES // tn
    in_specs = [pl.BlockSpec((tm, l.shape[1]), lambda i, j: (i, 0)) for l in lhs]
    in_specs.append(pl.BlockSpec((w.shape[0], tn), lambda i, j: (0, j + joff)))
    in_specs += list(extra_specs)
    out_specs, out_shape = [], []
    for dt in out_dtypes:
        if dt == "f32_head_major":
            out_specs.append(pl.BlockSpec((tm, tn // LANES, LANES), lambda i, j: (i, j, 0)))
            out_shape.append(jax.ShapeDtypeStruct((m, n_cols // LANES, LANES), F32))
        else:
            out_specs.append(pl.BlockSpec((tm, tn), lambda i, j: (i, j)))
            out_shape.append(jax.ShapeDtypeStruct((m, n_cols), dt))
    return pl.pallas_call(
        kernel,
        grid=(m // tm, n_cols // tn),
        in_specs=in_specs,
        out_specs=out_specs,
        out_shape=out_shape,
        scratch_shapes=[pltpu.VMEM((tm, tn), F32)] if scratch else [],
        compiler_params=_cparams(("parallel", "arbitrary")),
        name="proj_matmul",
    )(*lhs, w, *extra)


def _rope_tables(pos):
    half = ROT_DIM // 2
    inv_freq = 1.0 / (ROPE_THETA ** (jnp.arange(0, ROT_DIM, 2, dtype=F32) / ROT_DIM))
    ang = pos.astype(F32)[:, None] * inv_freq[None, :]
    cos = jnp.cos(ang)
    sin = jnp.sin(ang)
    t = pos.shape[0]
    one = jnp.ones((t, DQK - ROT_DIM), F32)
    zero = jnp.zeros((t, DQK - ROT_DIM), F32)
    zh = jnp.zeros((t, half), F32)
    c64 = jnp.concatenate([cos, cos, one], axis=1)
    s1_64 = jnp.concatenate([-sin, zh, zero], axis=1)
    s2_64 = jnp.concatenate([zh, sin, zero], axis=1)
    tile2 = lambda z: jnp.concatenate([z, z], axis=1)
    return tile2(c64), tile2(s1_64), tile2(s2_64)


def _stack_maps(q):
    lane = lax.broadcasted_iota(jnp.int32, q.shape, 1)
    z = jnp.zeros_like(q)
    return jnp.concatenate([jnp.where(lane < DQK, q, z), jnp.where(lane >= DQK, q, z)], axis=0)


def _online_softmax_steps(ss, vs, ms, ls, accs):
    def bcast(m, n):
        return jnp.tile(m, (1, n // LANES)) if n >= LANES else m[:, :n]

    m_news = [jnp.maximum(m, jnp.max(s, axis=-1, keepdims=True)) for s, m in zip(ss, ms)]
    alphas = [jnp.exp2(m - mn) for m, mn in zip(ms, m_news)]
    ps = [jnp.exp2(s - bcast(mn, s.shape[1])) for s, mn in zip(ss, m_news)]
    l_news = [a * l + jnp.sum(p, axis=-1, keepdims=True) for a, l, p in zip(alphas, ls, ps)]
    pvs = [jnp.dot(p.astype(BF16), v, preferred_element_type=F32) for p, v in zip(ps, vs)]
    acc_news = [a * acc + pv for a, acc, pv in zip(alphas, accs, pvs)]
    return m_news, l_news, acc_news


def _attn_finish(acc, l, lam, subln_g, gate, out_scale):
    t = acc.shape[0] // 2
    o = acc[:t] / l[:t] - lam * (acc[t:] / l[t:])
    ms = jnp.mean(o * o, axis=-1, keepdims=True)
    o = o * lax.rsqrt(ms + SUBLN_EPS) * subln_g * out_scale
    return (o * gate.astype(F32)).astype(BF16)


def _prompt_attn_kernel(lam_ref, q_ref, k_ref, v_ref, g_ref, sg_ref, o_ref,
                        q12_ref, m_ref, l_ref, acc_ref, *, tq, heads, out_scale):
    qi = pl.program_id(2)
    for hp in range(heads):
        q12_ref[hp] = _stack_maps(q_ref[:, hp * DV:(hp + 1) * DV])
    m_ref[...] = jnp.full_like(m_ref, NEG_BIG)
    l_ref[...] = jnp.zeros_like(l_ref)
    acc_ref[...] = jnp.zeros_like(acc_ref)

    def tile(kj, masked):
        rows = pl.ds(pl.multiple_of(kj * tq, tq), tq)
        hcols = [slice(hp * DV, (hp + 1) * DV) for hp in range(heads)]
        ss = [lax.dot_general(q12_ref[hp], k_ref[rows, hcols[hp]], (((1,), (1,)), ((), ())),
                              preferred_element_type=F32) for hp in range(heads)]
        if masked:
            r = lax.broadcasted_iota(jnp.int32, ss[0].shape, 0) % tq
            c = lax.broadcasted_iota(jnp.int32, ss[0].shape, 1)
            ss = [jnp.where(c <= r, s, NEG_BIG) for s in ss]
        ms, ls, accs = _online_softmax_steps(
            ss, [v_ref[rows, hcols[hp]] for hp in range(heads)],
            [m_ref[hp] for hp in range(heads)], [l_ref[hp] for hp in range(heads)],
            [acc_ref[hp] for hp in range(heads)])
        for hp in range(heads):
            m_ref[hp] = ms[hp]
            l_ref[hp] = ls[hp]
            acc_ref[hp] = accs[hp]

    def body(kj, carry):
        tile(kj, False)
        return carry

    lax.fori_loop(0, qi, body, 0)
    tile(qi, True)
    lam = lam_ref[0, 0]
    for hp in range(heads):
        cols = slice(hp * DV, (hp + 1) * DV)
        o_ref[:, cols] = _attn_finish(acc_ref[hp], l_ref[hp], lam, sg_ref[...], g_ref[:, cols], out_scale)


def _prompt_attention(q, k16, v16, gates, lam, subln_g, *, batch, seq, out_scale, tq=512, heads=4):
    tq = min(tq, seq)
    nq = seq // tq
    w = heads * DV
    kern = functools.partial(_prompt_attn_kernel, tq=tq, heads=heads, out_scale=out_scale)
    return pl.pallas_call(
        kern,
        grid=(batch, N_ATT_HEADS // heads, nq),
        in_specs=[
            pl.BlockSpec(memory_space=pltpu.SMEM),
            pl.BlockSpec((tq, w), lambda b, h, i: (b * nq + i, h)),
            pl.BlockSpec((seq, w), lambda b, h, i: (b, h)),
            pl.BlockSpec((seq, w), lambda b, h, i: (b, h)),
            pl.BlockSpec((tq, w), lambda b, h, i: (b * nq + i, h)),
            pl.BlockSpec((1, DV), lambda b, h, i: (0, 0)),
        ],
        out_specs=pl.BlockSpec((tq, w), lambda b, h, i: (b * nq + i, h)),
        out_shape=jax.ShapeDtypeStruct((batch * seq, ATT_W), BF16),
        scratch_shapes=[pltpu.VMEM((heads, 2 * tq, DV), BF16),
                        pltpu.VMEM((heads, 2 * tq, LANES), F32),
                        pltpu.VMEM((heads, 2 * tq, LANES), F32),
                        pltpu.VMEM((heads, 2 * tq, DV), F32)],
        compiler_params=_cparams(("parallel", "parallel", "arbitrary")),
        name="prompt_attention",
    )(lam, q, k16, v16, gates, subln_g.reshape(1, DV))


QUADS = 4
HEADS_PER_QUAD = N_ATT_HEADS // QUADS


def _sample_attn_body(j, n_steps, lam_ref, q_ref, refs, *, t_new, pages_per_step, out_scale):
    kc_refs = refs[:pages_per_step]
    vc_refs = refs[pages_per_step:2 * pages_per_step]
    kn_ref, vn_ref, g_ref, sg_ref, o_ref, q12_ref, m_ref, l_ref, acc_ref = refs[2 * pages_per_step:]
    rows = 2 * t_new
    qrows = HEADS_PER_QUAD * rows
    ncols = PAGE_SIZE * HEADS_PER_QUAD

    @pl.when(j == 0)
    def _():
        for h in range(N_ATT_HEADS):
            c, i = h % QUADS, h // QUADS
            q12_ref[c, i * rows:(i + 1) * rows, :] = _stack_maps(q_ref[0, :, h * DV:(h + 1) * DV])
        m_ref[...] = jnp.full_like(m_ref, NEG_BIG)
        l_ref[...] = jnp.zeros_like(l_ref)
        acc_ref[...] = jnp.zeros_like(acc_ref)

    yield "init"
    own = (lax.broadcasted_iota(jnp.int32, (qrows, ncols), 0) // rows
           == lax.broadcasted_iota(jnp.int32, (qrows, ncols), 1) % HEADS_PER_QUAD)
    nt = (((1,), (1,)), ((), ()))
    quads = range(QUADS)
    ms = [m_ref[c] for c in quads]
    ls = [l_ref[c] for c in quads]
    accs = [acc_ref[c] for c in quads]
    for u in range(pages_per_step):
        scores = [jnp.where(own, lax.dot_general(q12_ref[c], kc_refs[u][pl.ds(c, ncols, stride=QUADS), :].astype(BF16),
                                                 nt, preferred_element_type=F32), NEG_BIG) for c in quads]
        yield "scores"
        vals = [vc_refs[u][pl.ds(c, ncols, stride=QUADS), :].astype(BF16) for c in quads]
        ms, ls, accs = _online_softmax_steps(scores, vals, ms, ls, accs)
        yield "softmax"
    for c in quads:
        m_ref[c] = ms[c]
        l_ref[c] = ls[c]
        acc_ref[c] = accs[c]

    yield "final"

    @pl.when(j == n_steps - 1)
    def _():
        r = lax.broadcasted_iota(jnp.int32, (rows, t_new), 0) % t_new
        cc = lax.broadcasted_iota(jnp.int32, (rows, t_new), 1)
        causal = cc <= r
        lam = lam_ref[0, 0]
        for h in range(N_ATT_HEADS):
            c, i = h % QUADS, h // QUADS
            sl = slice(i * rows, (i + 1) * rows)
            cols = slice(h * DV, (h + 1) * DV)
            s = lax.dot_general(q12_ref[c, sl, :], kn_ref[0, :, cols], (((1,), (1,)), ((), ())),
                                preferred_element_type=F32)
            s = jnp.where(causal, s, NEG_BIG)
            _, (l,), (acc,) = _online_softmax_steps([s], [vn_ref[0, :, cols]], [m_ref[c, sl, :]],
                                                    [l_ref[c, sl, :]], [acc_ref[c, sl, :]])
            o_ref[0, :, cols] = _attn_finish(acc, l, lam, sg_ref[...], g_ref[0, :, cols], out_scale)


def _sample_attn_kernel(pt_ref, lam_ref, q_ref, *refs, **kw):
    _run(_sample_attn_body(pl.program_id(1), pl.num_programs(1), lam_ref, q_ref, refs, **kw))


def _sample_attn_call(q, k16, v16, gates, cache_k, cache_v, page_table, lam, subln_g, *, layer, out_scale,
                      pages_per_step=8):
    b, t_new, _ = q.shape
    n_pages = page_table.shape[1]
    n_pool = cache_k.shape[1]
    pps = math.gcd(pages_per_step, n_pages)
    page_rows = PAGE_SIZE * N_ATT_HEADS
    kc2 = cache_k.reshape(-1, DV)
    vc2 = cache_v.reshape(-1, DV)
    tok_spec = pl.BlockSpec((1, t_new, ATT_W), lambda i, j, pt: (i, 0, 0))

    def page_spec(u):
        return pl.BlockSpec((page_rows, DV), lambda i, j, pt: (layer * n_pool + pt[i, j * pps + u], 0))

    page_specs = [page_spec(u) for u in range(pps)]
    qrows = HEADS_PER_QUAD * 2 * t_new
    return dict(
        grid=(b, n_pages // pps),
        kw=dict(t_new=t_new, pages_per_step=pps, out_scale=out_scale),
        in_specs=[pl.BlockSpec(memory_space=pltpu.SMEM), tok_spec] + page_specs + page_specs
                 + [tok_spec, tok_spec, tok_spec, pl.BlockSpec((1, DV), lambda i, j, pt: (0, 0))],
        args=[lam, q] + [kc2] * pps + [vc2] * pps + [k16, v16, gates, subln_g.reshape(1, DV)],
        out_specs=[tok_spec],
        out_shape=[jax.ShapeDtypeStruct((b, t_new, ATT_W), BF16)],
        scratch=[pltpu.VMEM((QUADS, qrows, DV), BF16),
                 pltpu.VMEM((QUADS, qrows, LANES), F32),
                 pltpu.VMEM((QUADS, qrows, LANES), F32),
                 pltpu.VMEM((QUADS, qrows, DV), F32)],
    )


def _sample_attention(page_table, call):
    grid_spec = pltpu.PrefetchScalarGridSpec(
        num_scalar_prefetch=1, grid=call["grid"], in_specs=call["in_specs"], out_specs=call["out_specs"],
        scratch_shapes=call["scratch"])
    return pl.pallas_call(
        functools.partial(_sample_attn_kernel, **call["kw"]),
        grid_spec=grid_spec,
        out_shape=call["out_shape"],
        compiler_params=_cparams(("parallel", "arbitrary")),
        name="sample_attention",
    )(page_table, *call["args"])[0]


def _stack_heads(x):
    lane = lax.broadcasted_iota(jnp.int32, x.shape, 1)
    z = jnp.zeros_like(x)
    return jnp.concatenate([jnp.where(lane < RWKV_HEAD, x, z), jnp.where(lane >= RWKV_HEAD, x, z)], axis=0)


def _each(f, *lists):
    return [f(*args) for args in zip(*lists)]


def _mm(a, b):
    return jnp.dot(a, b, preferred_element_type=F32)


def _rwkv_chunk(rs, krs, vs, lo, wls, w0s, a0s, k_ks, k_as, r_ks, lnx_gs, lnx_bs, gates, h0s, consts):
    ones, tril, strict, incl, eye = consts
    C = lo.shape[0]
    n2 = 2 * C
    lane = lax.broadcasted_iota(jnp.int32, (C, LANES), 1)
    lo_act = jnp.where(lane < LORA, jnp.tanh(lo), lo).astype(BF16)
    pres = [_mm(lo_act, wl) for wl in wls]
    logws = _each(lambda pre, w0: -math.exp(-0.5) * jax.nn.sigmoid(w0 + pre[:, :LANES]), pres, w0s)
    a_sigs = _each(lambda pre, a0: jax.nn.sigmoid(a0 + pre[:, LANES:]), pres, a0s)
    yield "stage"

    kks = _each(lambda kr, k_k: kr * k_k, krs, k_ks)
    kk_ss = _segment_sums([kk * kk for kk in kks], ones)
    kks = _each(lambda kk, ss: kk / jnp.maximum(jnp.sqrt(ss), 1e-12), kks, kk_ss)
    k_hs = _each(lambda kr, a_sig, k_a: kr * (1.0 + (a_sig - 1.0) * k_a), krs, a_sigs, k_as)
    rk_ss = _segment_sums([r * k_h * r_k for r, k_h, r_k in zip(rs, k_hs, r_ks)], ones)
    bonuses = _each(lambda ss, v: ss * v, rk_ss, vs)
    yield "stage"

    def split3(x):
        l1 = x.astype(BF16)
        rem = x - l1.astype(F32)
        l2 = rem.astype(BF16)
        return jnp.concatenate([l1, l2, (rem - l2.astype(F32)).astype(BF16)], axis=0)

    cums = [_mm(tril, split3(logw)) for logw in logws]
    yield "stage"
    g_incls = [jnp.exp(cum) for cum in cums]
    g_excls = _each(lambda cum, logw: jnp.exp(cum - logw), cums, logws)
    g_invs = [jnp.exp(-cum) for cum in cums]
    g_lasts = [g[C - 1:C, :] for g in g_incls]

    a_ts = _each(lambda kk, g: _stack_heads(-kk * g).astype(BF16), kks, g_excls)
    r_ts = _each(lambda r, g: _stack_heads(r * g).astype(BF16), rs, g_incls)
    v_sts = [_stack_heads(v).astype(BF16) for v in vs]
    bks = _each(lambda kk, a_sig, k_h, g: jnp.concatenate([_stack_heads(kk * a_sig * g), _stack_heads(k_h * g)],
                                                          axis=0), kks, a_sigs, k_hs, g_invs)
    gmats = _each(lambda a_t, r_t, bk: lax.dot_general(jnp.concatenate([a_t, r_t], axis=0), bk.astype(BF16),
                                                       (((1,), (1,)), ((), ())), preferred_element_type=F32),
                  a_ts, r_ts, bks)
    yield "stage"
    a_abs = [jnp.where(strict, g[:n2, :n2], 0.0) for g in gmats]
    a_aks = [jnp.where(strict, g[:n2, n2:], 0.0).astype(BF16) for g in gmats]
    a_rbs = [jnp.where(incl, g[n2:, :n2], 0.0).astype(BF16) for g in gmats]
    a_rks = [jnp.where(incl, g[n2:, n2:], 0.0).astype(BF16) for g in gmats]
    yield "stage"

    eye2 = jnp.where(lax.broadcasted_iota(jnp.int32, (n2, n2), 0)
                     == lax.broadcasted_iota(jnp.int32, (n2, n2), 1), 1.0, 0.0)
    n_factors = int(math.log2(C))
    pows = [a_abs]
    pows16 = [[a.astype(BF16) for a in a_abs]]
    for _ in range(n_factors - 1):
        pows.append([_mm(pw, pw) for pw in pows16[-1]])
        pows16.append([x.astype(BF16) for x in pows[-1]])
        yield "stage"
    prods = [_each(lambda x, y, x16, y16: eye2 + x + y + _mm(x16, y16), pows[i], pows[i + 1], pows16[i], pows16[i + 1])
             for i in range(0, n_factors - 1, 2)]
    if n_factors % 2:
        prods.append([eye2 + x for x in pows[-1]])
    yield "stage"
    while len(prods) > 1:
        nxt = [_each(lambda a, b: _mm(a.astype(BF16), b.astype(BF16)), prods[i], prods[i + 1])
               for i in range(0, len(prods) - 1, 2)]
        prods = nxt + ([prods[-1]] if len(prods) % 2 else [])
        yield "stage"
    t_invs = prods[0]

    h16s = [h0.astype(BF16) for h0 in h0s]
    nt = (((1,), (1,)), ((), ()))
    if n2 % LANES == 0:
        cat = jnp.concatenate
        xs = _each(lambda a_t, h16, a_ak, v_st: _mm(cat([a_t, a_ak], axis=1), cat([h16, v_st], axis=0)),
                   a_ts, h16s, a_aks, v_sts)
    else:
        xs = _each(lambda a_t, s16, a_ak, v_st: lax.dot_general(a_t, s16, nt, preferred_element_type=F32)
                   + _mm(a_ak, v_st), a_ts, h16s, a_aks, v_sts)
    u32s = _each(lambda t, x: _mm(t.astype(BF16), x.astype(BF16)), t_invs, xs)
    u16s = [u.astype(BF16) for u in u32s]
    yield "stage"
    if n2 % LANES == 0:
        o_sts = _each(lambda r_t, h16, a_rb, u16, a_rk, v_st:
                      _mm(cat([r_t, a_rb, a_rk], axis=1), cat([h16, u16, v_st], axis=0)),
                      r_ts, h16s, a_rbs, u16s, a_rks, v_sts)
    else:
        o_sts = _each(lambda r_t, s16, a_rb, u16, a_rk, v_st: lax.dot_general(r_t, s16, nt, preferred_element_type=F32)
                      + _mm(a_rb, u16) + _mm(a_rk, v_st), r_ts, h16s, a_rbs, u16s, a_rks, v_sts)
    os_ = [o_st[:C] + o_st[C:] for o_st in o_sts]
    yield "stage"

    if n2 % LANES == 0:
        bk_ts = _each(lambda bk, g_last: (bk * g_last).T.astype(BF16), bks, g_lasts)
        g_cols = [jnp.sum(jnp.where(eye, jnp.broadcast_to(g_last, (LANES, LANES)), 0.0), axis=1, keepdims=True)
                  for g_last in g_lasts]
        h_news = _each(lambda g_col, h0, bk_t, u16, v_st: g_col * h0 + _mm(bk_t, jnp.concatenate([u16, v_st], axis=0)),
                       g_cols, h0s, bk_ts, u16s, v_sts)
    else:
        uv_ts = _each(lambda u, v: jnp.concatenate([u, _stack_heads(v)], axis=0).T.astype(BF16), u32s, vs)
        h_news = _each(lambda s0, g_last, uv_t, bk: s0 * g_last + _mm(uv_t, (bk * g_last).astype(BF16)),
                       h0s, g_lasts, uv_ts, bks)
    yield "stage"

    mus = [z * (1.0 / RWKV_HEAD) for z in _segment_sums(os_, ones)]
    ds = _each(lambda o, mu: o - mu, os_, mus)
    vars_ = [z * (1.0 / RWKV_HEAD) for z in _segment_sums([d * d for d in ds], ones)]
    ys = _each(lambda d, var, g, b, bonus, gate:
               (((d * lax.rsqrt(var + LNX_EPS) * g + b) + bonus) * gate.astype(F32)).astype(BF16),
               ds, vars_, lnx_gs, lnx_bs, bonuses, gates)
    return ys, h_news


def _rwkv_body(c, n_chunks, refs, *, chunk, pairs, use_s0):
    (r_ref, k_ref, v_ref, lo_ref, pr_ref, pk_ref, pv_ref, plo_ref,
     tmr_ref, tmk_ref, tmv_ref, tmlo_ref, wl_ref, w0_ref, a0_ref, kk_ref, ka_ref, rk_ref,
     lg_ref, lb_ref, gate_ref, s0_ref, y_ref, sout_ref,
     h_ref, cr_ref, ck_ref, cv_ref, clo_ref) = refs
    C = chunk
    n2 = 2 * C

    @pl.when(c == 0)
    def _():
        cr_ref[...] = pr_ref[0]
        ck_ref[...] = pk_ref[0]
        cv_ref[...] = pv_ref[0]
        clo_ref[...] = plo_ref[0]
        if use_s0:
            z = jnp.zeros((RWKV_HEAD, RWKV_HEAD), F32)
            for u in range(pairs):
                st = jnp.concatenate([jnp.concatenate([s0_ref[0, 2 * u], z], axis=1),
                                      jnp.concatenate([z, s0_ref[0, 2 * u + 1]], axis=1)], axis=0)
                h_ref[u] = st if (2 * C) % LANES else st.T
        else:
            h_ref[...] = jnp.zeros_like(h_ref)

    yield "init"
    ri = lax.broadcasted_iota(jnp.int32, (n2, n2), 0)
    ci = lax.broadcasted_iota(jnp.int32, (n2, n2), 1)
    tril = (lax.broadcasted_iota(jnp.int32, (C, 3 * C), 1) % C <= lax.broadcasted_iota(jnp.int32, (C, 3 * C), 0))
    eye = (lax.broadcasted_iota(jnp.int32, (LANES, LANES), 0)
           == lax.broadcasted_iota(jnp.int32, (LANES, LANES), 1))
    consts = (_half_ones(), jnp.where(tril, 1.0, 0.0).astype(BF16), ri > ci, ri >= ci, eye)

    def token_shift(cur_ref3, carry_ref, tm_ref):
        cur = cur_ref3[0]
        row0 = lax.broadcasted_iota(jnp.int32, cur.shape, 0) == 0
        prev = jnp.where(row0, carry_ref[...], pltpu.roll(cur, 1, 0))
        carry_ref[...] = cur[C - 1:C, :]
        return cur + tm_ref[...] * (prev - cur)

    lo = token_shift(lo_ref, clo_ref, tmlo_ref)
    r_all = token_shift(r_ref, cr_ref, tmr_ref)
    k_all = token_shift(k_ref, ck_ref, tmk_ref)
    v_all = token_shift(v_ref, cv_ref, tmv_ref)
    col = lambda u: slice(u * LANES, (u + 1) * LANES)
    per_pair = lambda x: [x[:, col(u)] for u in range(pairs)]
    ys, h_news = yield from _rwkv_chunk(
        per_pair(r_all), per_pair(k_all), per_pair(v_all), lo, [wl_ref[u] for u in range(pairs)],
        per_pair(w0_ref[...]), per_pair(a0_ref[...]), per_pair(kk_ref[...]), per_pair(ka_ref[...]),
        per_pair(rk_ref[...]), per_pair(lg_ref[...]), per_pair(lb_ref[...]), per_pair(gate_ref[0]),
        [h_ref[u] for u in range(pairs)], consts)
    for u in range(pairs):
        y_ref[0, :, col(u)] = ys[u]
        h_ref[u] = h_news[u]

    yield "final"

    @pl.when(c == n_chunks - 1)
    def _():
        for u in range(pairs):
            st = h_news[u] if (2 * C) % LANES else h_news[u].T
            sout_ref[0, 2 * u] = st[:RWKV_HEAD, :RWKV_HEAD]
            sout_ref[0, 2 * u + 1] = st[RWKV_HEAD:, RWKV_HEAD:]


def _rwkv_kernel(*refs, **kw):
    _run(_rwkv_body(pl.program_id(2), pl.num_programs(2), refs, **kw))


def _rwkv_call(rkv, lora, gates, prevs, s0, p, *, chunk, pairs):
    b, t, _ = lora.shape
    n_chunks = t // chunk
    ng = N_RWKV_PAIRS // pairs
    w = pairs * LANES
    use_s0 = s0 is not None
    sblk = (1, 2 * pairs, RWKV_HEAD, RWKV_HEAD)
    if s0 is None:
        s0 = jnp.zeros(sblk, F32)
        s0_spec = pl.BlockSpec(sblk, lambda i, j, c: (0, 0, 0, 0))
    else:
        s0_spec = pl.BlockSpec(sblk, lambda i, j, c: (i, j, 0, 0))

    tok = pl.BlockSpec((1, chunk, w), lambda i, j, c: (i, c, j))
    prev = pl.BlockSpec((1, 1, w), lambda i, j, c: (i, 0, j))

    def par():
        return pl.BlockSpec((1, w), lambda i, j, c: (0, j))

    return dict(
        grid=(b, ng, n_chunks),
        kw=dict(chunk=chunk, pairs=pairs, use_s0=use_s0),
        in_specs=[
            tok, tok, tok,
            pl.BlockSpec((1, chunk, LANES), lambda i, j, c: (i, c, 0)),
            prev, prev, prev,
            pl.BlockSpec((1, 1, LANES), lambda i, j, c: (i, 0, 0)),
            par(), par(), par(), pl.BlockSpec((1, LANES), lambda i, j, c: (0, 0)),
            pl.BlockSpec((pairs, LANES, 2 * LANES), lambda i, j, c: (j, 0, 0)),
            par(), par(), par(), par(), par(), par(), par(),
            pl.BlockSpec((1, chunk, w), lambda i, j, c: (i, c, ng + j)),
            s0_spec,
        ],
        args=[*rkv, lora, *prevs, p["tm_r"], p["tm_k"], p["tm_v"], p["tm_lora"], p["w_lora"],
              p["w0"], p["a0"], p["k_k"], p["k_a"], p["r_k"], p["lnx_g"], p["lnx_b"], gates, s0],
        out_specs=[pl.BlockSpec((1, chunk, w), lambda i, j, c: (i, c, j)),
                   pl.BlockSpec(sblk, lambda i, j, c: (i, j, 0, 0))],
        out_shape=[jax.ShapeDtypeStruct((b, t, RWKV_W), BF16),
                   jax.ShapeDtypeStruct((b, N_RWKV_HEADS, RWKV_HEAD, RWKV_HEAD), F32)],
        scratch=[pltpu.VMEM((pairs, LANES, LANES), F32),
                 pltpu.VMEM((1, w), F32), pltpu.VMEM((1, w), F32),
                 pltpu.VMEM((1, w), F32), pltpu.VMEM((1, LANES), F32)],
    )


def _rwkv(call):
    return pl.pallas_call(
        functools.partial(_rwkv_kernel, **call["kw"]),
        grid=call["grid"],
        in_specs=call["in_specs"],
        out_specs=call["out_specs"],
        out_shape=call["out_shape"],
        scratch_shapes=call["scratch"],
        compiler_params=_cparams(("parallel", "parallel", "arbitrary")),
        name="rwkv7_chunked",
    )(*call["args"])


def _remap(spec, decode):
    if spec.block_shape is None:
        return spec
    return pl.BlockSpec(spec.block_shape, lambda *a: spec.index_map(*decode(*a)))


def _fused_kernel(pt_ref, *refs, n_in, n_out, n_scr, rwkv_kw, attn_kw, n_chunks, attn_steps):
    (ri, ai), (ro, ao), (rs, as_) = n_in, n_out, n_scr
    r_in, a_in = refs[:ri], refs[ri:ri + ai]
    r_out, a_out = refs[ri + ai:ri + ai + ro], refs[ri + ai + ro:ri + ai + ro + ao]
    r_scr, a_scr = refs[ri + ai + ro + ao:ri + ai + ro + ao + rs], refs[ri + ai + ro + ao + rs:]
    s = pl.program_id(0)
    _interleave(
        _rwkv_body(s % n_chunks, n_chunks, (*r_in, *r_out, *r_scr), **rwkv_kw),
        _sample_attn_body(s % attn_steps, attn_steps, a_in[0], a_in[1], (*a_in[2:], *a_out, *a_scr), **attn_kw))


def _rwkv_with_sample_attention(rwkv_call, attn_call, page_table):
    b_r, ng, nc = rwkv_call["grid"]
    b_a, nj = attn_call["grid"]
    n_steps = b_r * ng * nc
    assert n_steps == b_a * nj
    dec_r = lambda s, pt: (s // (ng * nc), (s // nc) % ng, s % nc)
    dec_a = lambda s, pt: (s // nj, s % nj, pt)
    in_specs = [_remap(sp, dec_r) for sp in rwkv_call["in_specs"]] + [_remap(sp, dec_a) for sp in attn_call["in_specs"]]
    out_specs = [_remap(sp, dec_r) for sp in rwkv_call["out_specs"]] + [_remap(sp, dec_a) for sp in attn_call["out_specs"]]
    kern = functools.partial(
        _fused_kernel,
        n_in=(len(rwkv_call["in_specs"]), len(attn_call["in_specs"])),
        n_out=(len(rwkv_call["out_specs"]), len(attn_call["out_specs"])),
        n_scr=(len(rwkv_call["scratch"]), len(attn_call["scratch"])),
        rwkv_kw=rwkv_call["kw"], attn_kw=attn_call["kw"], n_chunks=nc, attn_steps=nj)
    grid_spec = pltpu.PrefetchScalarGridSpec(
        num_scalar_prefetch=1, grid=(n_steps,), in_specs=in_specs, out_specs=out_specs,
        scratch_shapes=rwkv_call["scratch"] + attn_call["scratch"])
    y_rwkv, s_out, y_att = pl.pallas_call(
        kern,
        grid_spec=grid_spec,
        out_shape=rwkv_call["out_shape"] + attn_call["out_shape"],
        compiler_params=_cparams(("arbitrary",), FUSED_VMEM_LIMIT_BYTES),
        name="rwkv7_and_paged_attention",
    )(page_table, *rwkv_call["args"], *attn_call["args"])
    return y_rwkv, s_out, y_att


def _prep_layer_params(l, ln_g, w_in, q_norm_g, k_norm_g, lambda_q1, lambda_k1, lambda_q2, lambda_k2,
                       subln_g, time_mix, w0, w2, a0, a2, k_k, k_a, r_k, lnx_g, lnx_b, w_out):
    w = w_in[l]
    sh = OFF_SHIFT
    o1, o2, o3, o4 = RWKV_W, RWKV_W + LORA, 2 * RWKV_W + LORA, 3 * RWKV_W + LORA
    w16 = w.astype(BF16)
    w_qkvr = w16
    w_rk = w16[:, sh + o2:sh + o3]
    w_rv = w16[:, sh + o3:sh + o4]
    w_gates = w16[:, OFF_GA:]
    w_lora_in = jnp.concatenate([w16[:, sh + o1:sh + o2], w16[:, sh + o4:sh + SHIFT_W]], axis=1)
    tmix = time_mix[l]
    tm_lora = jnp.concatenate([tmix[o1:o2], tmix[o4:]]).reshape(1, 2 * LORA)
    w2p = w2[l].reshape(LORA, N_RWKV_PAIRS, LANES).transpose(1, 0, 2)
    a2p = a2[l].reshape(LORA, N_RWKV_PAIRS, LANES).transpose(1, 0, 2)
    z = jnp.zeros_like(w2p)
    w_lora = jnp.concatenate([jnp.concatenate([w2p, z], axis=2), jnp.concatenate([z, a2p], axis=2)],
                             axis=1).astype(BF16)
    lam_init = 0.8 - 0.6 * math.exp(-0.3 * l)
    lam = (jnp.exp(jnp.sum(lambda_q1[l] * lambda_k1[l])) - jnp.exp(jnp.sum(lambda_q2[l] * lambda_k2[l]))
           + lam_init).reshape(1, 1).astype(F32)
    row = lambda z_: z_.reshape(1, -1)
    return dict(
        layer=l, ln_g=ln_g[l], w_qkvr=w_qkvr, w_rk=w_rk, w_rv=w_rv, w_gates=w_gates, w_lora_in=w_lora_in,
        w_out=w_out[l].astype(BF16),
        q_gain=jnp.tile(q_norm_g[l], 2).reshape(1, DV), k_gain=jnp.tile(k_norm_g[l], 2).reshape(1, DV),
        lam=lam, out_scale=1.0 - lam_init, subln_g=subln_g[l],
        tm_r=row(tmix[:o1]), tm_k=row(tmix[o2:o3]), tm_v=row(tmix[o3:o4]), tm_lora=tm_lora, w_lora=w_lora,
        w0=row(w0[l]), a0=row(a0[l]), k_k=row(k_k[l]), k_a=row(k_a[l]), r_k=row(r_k[l]),
        lnx_g=row(lnx_g[l]), lnx_b=row(lnx_b[l]),
    )


def _project(x, pos, p):
    b, t, d = x.shape
    m = b * t
    x2 = x.reshape(m, d)
    h = _rmsnorm(x2, p["ln_g"])

    tm = min(1024, m)
    cos, s1, s2 = _rope_tables(pos)
    if t < tm:
        reps = tm // t
        cos, s1, s2 = (jnp.tile(z, (reps, 1)) for z in (cos, s1, s2))
    gain_spec = pl.BlockSpec((1, LANES), lambda i, j: (0, 0))
    blk = lambda cols: cols // LANES
    tm_hm, tn_hm = min(512, m), 1024

    def qk(col_off, gain, scale, dts):
        tm_, tn_ = (tm_hm, tn_hm) if "f32_head_major" in dts else (tm, 512)
        n_tab = cos.shape[0] // tm_
        tab_spec = pl.BlockSpec((tm_, LANES), lambda i, j: (i % n_tab, 0))
        kern = functools.partial(_mm_qk_kernel, scale=scale, rows_per_chunk=min(256, tm_))
        return _matmul([h], p["w_qkvr"], col_block_off=blk(col_off), n_cols=ATT_W, kernel=kern,
                       out_dtypes=dts, extra=(gain, cos, s1, s2),
                       extra_specs=(gain_spec, tab_spec, tab_spec, tab_spec), scratch=True, tm=tm_, tn=tn_)

    def plain(w, col_off, n_cols):
        return _matmul([h], w, col_block_off=blk(col_off), n_cols=n_cols, kernel=_mm_plain_kernel,
                       out_dtypes=(F32,), tm=tm, tn=1024)[0]

    (q16,) = qk(0, p["q_gain"], DQK ** -0.5 * math.log2(math.e), (BF16,))
    k32, k16 = qk(OFF_K, p["k_gain"], 1.0, ("f32_head_major", BF16))
    v32, v16 = _matmul([h], p["w_qkvr"], col_block_off=blk(OFF_V), n_cols=ATT_W,
                       kernel=_mm_dual_kernel, out_dtypes=("f32_head_major", BF16), tm=tm_hm, tn=tn_hm)
    rkv = [plain(p["w_qkvr"], OFF_SHIFT, RWKV_W).reshape(b, t, RWKV_W),
           plain(p["w_rk"], 0, RWKV_W).reshape(b, t, RWKV_W),
           plain(p["w_rv"], 0, RWKV_W).reshape(b, t, RWKV_W)]
    lora = plain(p["w_lora_in"], 0, 2 * LORA).reshape(b, t, 2 * LORA)
    (gates,) = _matmul([h], p["w_gates"], col_block_off=0, n_cols=ATT_W + RWKV_W,
                       kernel=_mm_silu_kernel, out_dtypes=(BF16,), tm=tm, tn=1024)
    return dict(b=b, t=t, m=m, tm=tm, x2=x2, q16=q16, k32=k32, k16=k16, v32=v32, v16=v16, rkv=rkv, lora=lora,
                gates=gates)


def _rwkv_call_for(pr, shift_prev, wkv_prev, p, *, chunk, pairs):
    b = pr["b"]
    o1, o2, o3, o4 = RWKV_W, RWKV_W + LORA, 2 * RWKV_W + LORA, 3 * RWKV_W + LORA
    prev_lora = jnp.concatenate([shift_prev[:, o1:o2], shift_prev[:, o4:]], axis=1)
    prevs = [z.reshape(b, 1, -1) for z in (shift_prev[:, :o1], shift_prev[:, o2:o3], shift_prev[:, o3:o4],
                                           prev_lora)]
    return _rwkv_call(pr["rkv"], pr["lora"], pr["gates"].reshape(b, pr["t"], -1), prevs, wkv_prev, p,
                      chunk=chunk, pairs=pairs)


def _finish_layer(pr, y_att, y_rwkv, p):
    b, t, m, tm = pr["b"], pr["t"], pr["m"], pr["tm"]
    d = pr["x2"].shape[1]
    last_lora = pr["lora"][:, -1]
    shift_new = jnp.concatenate([pr["rkv"][0][:, -1], last_lora[:, :LORA], pr["rkv"][1][:, -1],
                                 pr["rkv"][2][:, -1], last_lora[:, LORA:]], axis=1)
    tn_out = 512
    x_spec = pl.BlockSpec((tm, tn_out), lambda i, j: (i, j))
    (out,) = _matmul([y_att.reshape(m, ATT_W), y_rwkv.reshape(m, RWKV_W)], p["w_out"], col_block_off=0,
                     n_cols=d, kernel=_mm_residual_kernel, out_dtypes=(F32,), extra=(pr["x2"],),
                     extra_specs=(x_spec,), tm=tm, tn=tn_out)
    return (out.reshape(b, t, d), pr["k32"].reshape(b, t, N_ATT_HEADS, DV),
            pr["v32"].reshape(b, t, N_ATT_HEADS, DV), shift_new)


def _layer_pair(xp, xs, pos_p, pos_s, shift_s, wkv_s, cache_k, cache_v, page_table, p, *,
                chunk_p=64, pairs_p=8, pairs_s=16, pages_per_step=8):
    pp = _project(xp, pos_p, p)
    ps = _project(xs, pos_s, p)
    b_s, t_s = ps["b"], ps["t"]
    rw_p = _rwkv_call_for(pp, jnp.zeros((pp["b"], SHIFT_W), F32), None, p, chunk=chunk_p, pairs=pairs_p)
    rw_s = _rwkv_call_for(ps, shift_s, wkv_s, p, chunk=t_s, pairs=pairs_s)
    at_s = _sample_attn_call(ps["q16"].reshape(b_s, t_s, ATT_W), ps["k16"].reshape(b_s, t_s, ATT_W),
                             ps["v16"].reshape(b_s, t_s, ATT_W), ps["gates"].reshape(b_s, t_s, -1),
                             cache_k, cache_v, page_table, p["lam"], p["subln_g"], layer=p["layer"],
                             out_scale=p["out_scale"], pages_per_step=pages_per_step)
    if math.prod(rw_p["grid"]) == math.prod(at_s["grid"]):
        y_rwkv_p, wkv_p, y_att_s = _rwkv_with_sample_attention(rw_p, at_s, page_table)
    else:
        y_rwkv_p, wkv_p = _rwkv(rw_p)
        y_att_s = _sample_attention(page_table, at_s)
    y_att_p = _prompt_attention(pp["q16"], pp["k16"], pp["v16"], pp["gates"], p["lam"], p["subln_g"],
                                batch=pp["b"], seq=pp["t"], out_scale=p["out_scale"])
    y_rwkv_s, wkv_s_new = _rwkv(rw_s)
    return (*_finish_layer(pp, y_att_p, y_rwkv_p, p), wkv_p), (*_finish_layer(ps, y_att_s, y_rwkv_s, p), wkv_s_new)


def kernel(x_prompt, x_sample, cache_k, cache_v, state_shift, state_wkv, page_table, ln_g, w_in, q_norm_g,
           k_norm_g, lambda_q1, lambda_k1, lambda_q2, lambda_k2, subln_g, time_mix, w0, w2, a0, a2, k_k,
           k_a, r_k, lnx_g, lnx_b, w_out):
    depth = w_in.shape[0]
    s_p, s_s = x_prompt.shape[1], x_sample.shape[1]
    past_len = page_table.shape[1] * PAGE_SIZE
    pos_p = jnp.arange(s_p)
    pos_s = past_len + jnp.arange(s_s)
    yp, ys = x_prompt, x_sample
    outs = [[] for _ in range(8)]
    for l in range(depth):
        p = _prep_layer_params(l, ln_g, w_in, q_norm_g, k_norm_g, lambda_q1, lambda_k1, lambda_q2,
                               lambda_k2, subln_g, time_mix, w0, w2, a0, a2, k_k, k_a, r_k, lnx_g, lnx_b,
                               w_out)
        (yp, kp, vp, shp, wkvp), (ys, ks, vs, shs, wkvs) = _layer_pair(
            yp, ys, pos_p, pos_s, state_shift[l], state_wkv[l], cache_k, cache_v, page_table, p)
        for lst, val in zip(outs, (kp, vp, shp, wkvp, ks, vs, shs, wkvs)):
            lst.append(val)
    stacked = [jnp.stack(lst, 0) for lst in outs]
    return (yp, ys, *stacked)
```

```python
import functools
import math

import jax
import jax.numpy as jnp
from jax import lax
from jax.experimental import pallas as pl
from jax.experimental.pallas import tpu as pltpu

F32 = jnp.float32
BF16 = jnp.bfloat16

D_MODEL = 4096
PAGE_SIZE = 128
ATT_W = 2048
RWKV_W = 2048
DV = 128
N_ATT_HEADS = ATT_W // DV
DQK = DV // 2
ROT_DIM = DQK // 4
ROPE_THETA = 500000.0
RWKV_HEAD = 64
N_RWKV_HEADS = RWKV_W // RWKV_HEAD
N_RWKV_PAIRS = N_RWKV_HEADS // 2
LORA = 64
SHIFT_W = 3 * RWKV_W + 2 * LORA
OFF_K = ATT_W
OFF_V = 2 * ATT_W
OFF_SHIFT = 3 * ATT_W
OFF_GA = OFF_SHIFT + SHIFT_W
OFF_GR = OFF_GA + ATT_W
IN_W = OFF_GR + RWKV_W
NORM_EPS = 1e-6
SUBLN_EPS = 1e-5
LNX_EPS = 64e-5

LANES = 128
VMEM_LIMIT_BYTES = 48 * 1024 * 1024
FUSED_VMEM_LIMIT_BYTES = 56 * 1024 * 1024
NEG_BIG = -1e30


def _cparams(sem, vmem_limit_bytes=VMEM_LIMIT_BYTES, allow_input_fusion=None):
    return pltpu.CompilerParams(dimension_semantics=sem, vmem_limit_bytes=vmem_limit_bytes,
                                allow_input_fusion=allow_input_fusion)


def _run(staged_body):
    for _ in staged_body:
        pass


def _interleave(*staged_bodies):
    active, finals = list(staged_bodies), []
    while active:
        for g in list(active):
            tag = next(g, "done")
            if tag == "done":
                active.remove(g)
            elif tag == "final":
                active.remove(g)
                finals.append(g)
    for g in finals:
        _run(g)


def _split_dot(x, w2_bf16):
    hi = x.astype(BF16)
    lo = (x - hi.astype(F32)).astype(BF16)
    return jnp.dot(jnp.concatenate([hi, lo], axis=1), w2_bf16, preferred_element_type=F32)


def _segment_sums(xs, w2_bf16):
    rows = xs[0].shape[0]
    out = _split_dot(jnp.concatenate(xs, axis=0), w2_bf16)
    return [out[i * rows:(i + 1) * rows] for i in range(len(xs))]


def _half_ones():
    r = (lax.broadcasted_iota(jnp.int32, (2 * LANES, LANES), 0) % LANES) // RWKV_HEAD
    c = lax.broadcasted_iota(jnp.int32, (2 * LANES, LANES), 1) // RWKV_HEAD
    return jnp.where(r == c, 1.0, 0.0).astype(BF16)


def _rmsnorm_kernel(x_ref, g_ref, o_ref):
    rows_per_iter = 16

    def body(i, carry):
        rows = pl.ds(pl.multiple_of(i * rows_per_iter, rows_per_iter), rows_per_iter)
        x = x_ref[rows, :]
        ms = jnp.mean(x * x, axis=-1, keepdims=True)
        o_ref[rows, :] = (x * lax.rsqrt(ms + NORM_EPS) * g_ref[...]).astype(BF16)
        return carry

    n_iter = x_ref.shape[0] // rows_per_iter
    lax.fori_loop(0, n_iter, body, 0, unroll=math.gcd(n_iter, 8))


def _rmsnorm(x2, g, tm=512):
    m, d = x2.shape
    tm = min(tm, m)
    return pl.pallas_call(
        _rmsnorm_kernel,
        grid=(m // tm,),
        in_specs=[pl.BlockSpec((tm, d), lambda i: (i, 0)),
                  pl.BlockSpec((1, d), lambda i: (0, 0))],
        out_specs=pl.BlockSpec((tm, d), lambda i: (i, 0)),
        out_shape=jax.ShapeDtypeStruct((m, d), BF16),
        compiler_params=_cparams(("parallel",)),
        name="rmsnorm",
    )(x2, g.reshape(1, d))


def _mm_full_k(lhs_refs, w_ref):
    acc, off = None, 0
    for l_ref in lhs_refs:
        k = l_ref.shape[1]
        part = jnp.dot(l_ref[...], w_ref[off:off + k, :], preferred_element_type=F32)
        acc = part if acc is None else acc + part
        off += k
    return acc


def _mm_plain_kernel(h_ref, w_ref, o_ref):
    o_ref[...] = _mm_full_k([h_ref], w_ref).astype(o_ref.dtype)


def _mm_dual_kernel(h_ref, w_ref, o32_ref, o16_ref):
    a = _mm_full_k([h_ref], w_ref)
    o16_ref[...] = a.astype(BF16)
    for hh in range(a.shape[1] // LANES):
        o32_ref[:, hh, :] = a[:, hh * LANES:(hh + 1) * LANES]


def _mm_silu_kernel(h_ref, w_ref, o_ref):
    a = _mm_full_k([h_ref], w_ref)
    o_ref[...] = (a * jax.nn.sigmoid(a)).astype(o_ref.dtype)


def _mm_residual_kernel(ya_ref, yr_ref, w_ref, x_ref, o_ref):
    o_ref[...] = x_ref[...] + _mm_full_k([ya_ref, yr_ref], w_ref)


def _mm_qk_kernel(h_ref, w_ref, g_ref, cos_ref, s1_ref, s2_ref, *rest, scale, rows_per_chunk):
    outs, acc_ref = rest[:-1], rest[-1]
    acc_ref[...] = _mm_full_k([h_ref], w_ref)
    tm, tn = acc_ref.shape
    ones = _half_ones()
    gain = g_ref[...]

    def chunk(r, carry):
        rows = pl.ds(pl.multiple_of(r * rows_per_chunk, rows_per_chunk), rows_per_chunk)
        c = cos_ref[rows, :]
        s1 = s1_ref[rows, :]
        s2 = s2_ref[rows, :]
        for hh in range(tn // LANES):
            cols = slice(hh * LANES, (hh + 1) * LANES)
            x = acc_ref[rows, cols]
            ss = _split_dot(x * x, ones)
            y = x * lax.rsqrt(ss * (1.0 / DQK) + NORM_EPS) * gain
            y = y * c + pltpu.roll(y, LANES - ROT_DIM // 2, 1) * s1 + pltpu.roll(y, ROT_DIM // 2, 1) * s2
            if scale != 1.0:
                y = y * scale
            for o_ref in outs:
                if len(o_ref.shape) == 3:
                    o_ref[rows, hh, :] = y.astype(o_ref.dtype)
                else:
                    o_ref[rows, cols] = y.astype(o_ref.dtype)
        return carry

    lax.fori_loop(0, tm // rows_per_chunk, chunk, 0)


def _matmul(lhs, w, *, col_block_off, n_cols, kernel, out_dtypes, extra=(), extra_specs=(),
            scratch=False, tm=1024, tn=512):
    m = lhs[0].shape[0]
    tm = min(tm, m)
    tn = min(tn, n_cols)
    assert m % tm == 0 and n_cols % tn == 0 and sum(l.shape[1] for l in lhs) == w.shape[0]
    assert (col_block_off * LANES) % tn == 0
    joff = col_block_off * LANES // tn
    in_specs = [pl.BlockSpec((tm, l.shape[1]), lambda i, j: (i, 0)) for l in lhs]
    in_specs.append(pl.BlockSpec((w.shape[0], tn), lambda i, j: (0, j + joff)))
    in_specs += list(extra_specs)
    out_specs, out_shape = [], []
    for dt in out_dtypes:
        if dt == "f32_head_major":
            out_specs.append(pl.BlockSpec((tm, tn // LANES, LANES), lambda i, j: (i, j, 0)))
            out_shape.append(jax.ShapeDtypeStruct((m, n_cols // LANES, LANES), F32))
        else:
            out_specs.append(pl.BlockSpec((tm, tn), lambda i, j: (i, j)))
            out_shape.append(jax.ShapeDtypeStruct((m, n_cols), dt))
    return pl.pallas_call(
        kernel,
        grid=(m // tm, n_cols // tn),
        in_specs=in_specs,
        out_specs=out_specs,
        out_shape=out_shape,
        scratch_shapes=[pltpu.VMEM((tm, tn), F32)] if scratch else [],
        compiler_params=_cparams(("parallel", "arbitrary"), allow_input_fusion=[
            i == len(lhs) for i in range(len(lhs) + 1 + len(extra))]),
        name="proj_matmul",
    )(*lhs, w, *extra)


def _rope_tables(pos):
    half = ROT_DIM // 2
    inv_freq = 1.0 / (ROPE_THETA ** (jnp.arange(0, ROT_DIM, 2, dtype=F32) / ROT_DIM))
    ang = pos.astype(F32)[:, None] * inv_freq[None, :]
    cos = jnp.cos(ang)
    sin = jnp.sin(ang)
    t = pos.shape[0]
    one = jnp.ones((t, DQK - ROT_DIM), F32)
    zero = jnp.zeros((t, DQK - ROT_DIM), F32)
    zh = jnp.zeros((t, half), F32)
    c64 = jnp.concatenate([cos, cos, one], axis=1)
    s1_64 = jnp.concatenate([-sin, zh, zero], axis=1)
    s2_64 = jnp.concatenate([zh, sin, zero], axis=1)
    tile2 = lambda z: jnp.concatenate([z, z], axis=1)
    return tile2(c64), tile2(s1_64), tile2(s2_64)


def _stack_maps(q):
    lane = lax.broadcasted_iota(jnp.int32, q.shape, 1)
    z = jnp.zeros_like(q)
    return jnp.concatenate([jnp.where(lane < DQK, q, z), jnp.where(lane >= DQK, q, z)], axis=0)


def _online_softmax_steps(ss, vs, ms, ls, accs):
    def bcast(m, n):
        return jnp.tile(m, (1, n // LANES)) if n >= LANES else m[:, :n]

    m_news = [jnp.maximum(m, jnp.max(s, axis=-1, keepdims=True)) for s, m in zip(ss, ms)]
    alphas = [jnp.exp2(m - mn) for m, mn in zip(ms, m_news)]
    ps = [jnp.exp2(s - bcast(mn, s.shape[1])) for s, mn in zip(ss, m_news)]
    l_news = [a * l + jnp.sum(p, axis=-1, keepdims=True) for a, l, p in zip(alphas, ls, ps)]
    pvs = [jnp.dot(p.astype(BF16), v, preferred_element_type=F32) for p, v in zip(ps, vs)]
    acc_news = [a * acc + pv for a, acc, pv in zip(alphas, accs, pvs)]
    return m_news, l_news, acc_news


def _attn_finish(acc, l, lam, subln_g, gate, out_scale):
    t = acc.shape[0] // 2
    o = acc[:t] / l[:t] - lam * (acc[t:] / l[t:])
    ms = jnp.mean(o * o, axis=-1, keepdims=True)
    o = o * lax.rsqrt(ms + SUBLN_EPS) * subln_g * out_scale
    return (o * gate.astype(F32)).astype(BF16)


def _prompt_attn_kernel(lam_ref, q_ref, k_ref, v_ref, g_ref, sg_ref, o_ref,
                        q12_ref, m_ref, l_ref, acc_ref, *, tq, heads, out_scale):
    qi = pl.program_id(2)
    for hp in range(heads):
        q12_ref[hp] = _stack_maps(q_ref[:, hp * DV:(hp + 1) * DV])
    m_ref[...] = jnp.full_like(m_ref, NEG_BIG)
    l_ref[...] = jnp.zeros_like(l_ref)
    acc_ref[...] = jnp.zeros_like(acc_ref)

    def tile(kj, masked):
        rows = pl.ds(pl.multiple_of(kj * tq, tq), tq)
        hcols = [slice(hp * DV, (hp + 1) * DV) for hp in range(heads)]
        ss = [lax.dot_general(q12_ref[hp], k_ref[rows, hcols[hp]], (((1,), (1,)), ((), ())),
                              preferred_element_type=F32) for hp in range(heads)]
        if masked:
            r = lax.broadcasted_iota(jnp.int32, ss[0].shape, 0) % tq
            c = lax.broadcasted_iota(jnp.int32, ss[0].shape, 1)
            ss = [jnp.where(c <= r, s, NEG_BIG) for s in ss]
        ms, ls, accs = _online_softmax_steps(
            ss, [v_ref[rows, hcols[hp]] for hp in range(heads)],
            [m_ref[hp] for hp in range(heads)], [l_ref[hp] for hp in range(heads)],
            [acc_ref[hp] for hp in range(heads)])
        for hp in range(heads):
            m_ref[hp] = ms[hp]
            l_ref[hp] = ls[hp]
            acc_ref[hp] = accs[hp]

    def body(kj, carry):
        tile(kj, False)
        return carry

    lax.fori_loop(0, qi, body, 0)
    tile(qi, True)
    lam = lam_ref[0, 0]
    for hp in range(heads):
        cols = slice(hp * DV, (hp + 1) * DV)
        o_ref[:, cols] = _attn_finish(acc_ref[hp], l_ref[hp], lam, sg_ref[...], g_ref[:, cols], out_scale)


def _prompt_attention(q, k16, v16, gates, lam, subln_g, *, batch, seq, out_scale, tq=512, heads=4):
    tq = min(tq, seq)
    nq = seq // tq
    w = heads * DV
    kern = functools.partial(_prompt_attn_kernel, tq=tq, heads=heads, out_scale=out_scale)
    return pl.pallas_call(
        kern,
        grid=(batch, N_ATT_HEADS // heads, nq),
        in_specs=[
            pl.BlockSpec(memory_space=pltpu.SMEM),
            pl.BlockSpec((tq, w), lambda b, h, i: (b * nq + i, h)),
            pl.BlockSpec((seq, w), lambda b, h, i: (b, h)),
            pl.BlockSpec((seq, w), lambda b, h, i: (b, h)),
            pl.BlockSpec((tq, w), lambda b, h, i: (b * nq + i, h)),
            pl.BlockSpec((1, DV), lambda b, h, i: (0, 0)),
        ],
        out_specs=pl.BlockSpec((tq, w), lambda b, h, i: (b * nq + i, h)),
        out_shape=jax.ShapeDtypeStruct((batch * seq, ATT_W), BF16),
        scratch_shapes=[pltpu.VMEM((heads, 2 * tq, DV), BF16),
                        pltpu.VMEM((heads, 2 * tq, LANES), F32),
                        pltpu.VMEM((heads, 2 * tq, LANES), F32),
                        pltpu.VMEM((heads, 2 * tq, DV), F32)],
        compiler_params=_cparams(("parallel", "parallel", "arbitrary")),
        name="prompt_attention",
    )(lam, q, k16, v16, gates, subln_g.reshape(1, DV))


QUADS = 4
HEADS_PER_QUAD = N_ATT_HEADS // QUADS


def _sample_attn_body(j, n_steps, lam_ref, q_ref, refs, *, t_new, pages_per_step, out_scale):
    kc_refs = refs[:pages_per_step]
    vc_refs = refs[pages_per_step:2 * pages_per_step]
    kn_ref, vn_ref, g_ref, sg_ref, o_ref, q12_ref, m_ref, l_ref, acc_ref = refs[2 * pages_per_step:]
    rows = 2 * t_new
    qrows = HEADS_PER_QUAD * rows
    ncols = PAGE_SIZE * HEADS_PER_QUAD

    @pl.when(j == 0)
    def _():
        for h in range(N_ATT_HEADS):
            c, i = h % QUADS, h // QUADS
            q12_ref[c, i * rows:(i + 1) * rows, :] = _stack_maps(q_ref[0, :, h * DV:(h + 1) * DV])
        m_ref[...] = jnp.full_like(m_ref, NEG_BIG)
        l_ref[...] = jnp.zeros_like(l_ref)
        acc_ref[...] = jnp.zeros_like(acc_ref)

    yield "init"
    own = (lax.broadcasted_iota(jnp.int32, (qrows, ncols), 0) // rows
           == lax.broadcasted_iota(jnp.int32, (qrows, ncols), 1) % HEADS_PER_QUAD)
    nt = (((1,), (1,)), ((), ()))
    quads = range(QUADS)
    ms = [m_ref[c] for c in quads]
    ls = [l_ref[c] for c in quads]
    accs = [acc_ref[c] for c in quads]
    for u in range(pages_per_step):
        scores = [jnp.where(own, lax.dot_general(q12_ref[c], kc_refs[u][pl.ds(c, ncols, stride=QUADS), :].astype(BF16),
                                                 nt, preferred_element_type=F32), NEG_BIG) for c in quads]
        yield "scores"
        vals = [vc_refs[u][pl.ds(c, ncols, stride=QUADS), :].astype(BF16) for c in quads]
        ms, ls, accs = _online_softmax_steps(scores, vals, ms, ls, accs)
        yield "softmax"
    for c in quads:
        m_ref[c] = ms[c]
        l_ref[c] = ls[c]
        acc_ref[c] = accs[c]

    yield "final"

    @pl.when(j == n_steps - 1)
    def _():
        r = lax.broadcasted_iota(jnp.int32, (rows, t_new), 0) % t_new
        cc = lax.broadcasted_iota(jnp.int32, (rows, t_new), 1)
        causal = cc <= r
        lam = lam_ref[0, 0]
        for h in range(N_ATT_HEADS):
            c, i = h % QUADS, h // QUADS
            sl = slice(i * rows, (i + 1) * rows)
            cols = slice(h * DV, (h + 1) * DV)
            s = lax.dot_general(q12_ref[c, sl, :], kn_ref[0, :, cols], (((1,), (1,)), ((), ())),
                                preferred_element_type=F32)
            s = jnp.where(causal, s, NEG_BIG)
            _, (l,), (acc,) = _online_softmax_steps([s], [vn_ref[0, :, cols]], [m_ref[c, sl, :]],
                                                    [l_ref[c, sl, :]], [acc_ref[c, sl, :]])
            o_ref[0, :, cols] = _attn_finish(acc, l, lam, sg_ref[...], g_ref[0, :, cols], out_scale)


def _sample_attn_kernel(pt_ref, lam_ref, q_ref, *refs, **kw):
    _run(_sample_attn_body(pl.program_id(1), pl.num_programs(1), lam_ref, q_ref, refs, **kw))


def _sample_attn_call(q, k16, v16, gates, cache_k, cache_v, page_table, lam, subln_g, *, layer, out_scale,
                      pages_per_step=8):
    b, t_new, _ = q.shape
    n_pages = page_table.shape[1]
    n_pool = cache_k.shape[1]
    pps = math.gcd(pages_per_step, n_pages)
    page_rows = PAGE_SIZE * N_ATT_HEADS
    kc2 = cache_k.reshape(-1, DV)
    vc2 = cache_v.reshape(-1, DV)
    tok_spec = pl.BlockSpec((1, t_new, ATT_W), lambda i, j, pt: (i, 0, 0))

    def page_spec(u):
        return pl.BlockSpec((page_rows, DV), lambda i, j, pt: (layer * n_pool + pt[i, j * pps + u], 0))

    page_specs = [page_spec(u) for u in range(pps)]
    qrows = HEADS_PER_QUAD * 2 * t_new
    return dict(
        grid=(b, n_pages // pps),
        kw=dict(t_new=t_new, pages_per_step=pps, out_scale=out_scale),
        in_specs=[pl.BlockSpec(memory_space=pltpu.SMEM), tok_spec] + page_specs + page_specs
                 + [tok_spec, tok_spec, tok_spec, pl.BlockSpec((1, DV), lambda i, j, pt: (0, 0))],
        args=[lam, q] + [kc2] * pps + [vc2] * pps + [k16, v16, gates, subln_g.reshape(1, DV)],
        out_specs=[tok_spec],
        out_shape=[jax.ShapeDtypeStruct((b, t_new, ATT_W), BF16)],
        scratch=[pltpu.VMEM((QUADS, qrows, DV), BF16),
                 pltpu.VMEM((QUADS, qrows, LANES), F32),
                 pltpu.VMEM((QUADS, qrows, LANES), F32),
                 pltpu.VMEM((QUADS, qrows, DV), F32)],
    )


def _sample_attention(page_table, call):
    grid_spec = pltpu.PrefetchScalarGridSpec(
        num_scalar_prefetch=1, grid=call["grid"], in_specs=call["in_specs"], out_specs=call["out_specs"],
        scratch_shapes=call["scratch"])
    return pl.pallas_call(
        functools.partial(_sample_attn_kernel, **call["kw"]),
        grid_spec=grid_spec,
        out_shape=call["out_shape"],
        compiler_params=_cparams(("parallel", "arbitrary")),
        name="sample_attention",
    )(page_table, *call["args"])[0]


def _stack_heads(x):
    lane = lax.broadcasted_iota(jnp.int32, x.shape, 1)
    z = jnp.zeros_like(x)
    return jnp.concatenate([jnp.where(lane < RWKV_HEAD, x, z), jnp.where(lane >= RWKV_HEAD, x, z)], axis=0)


def _each(f, *lists):
    return [f(*args) for args in zip(*lists)]


def _mm(a, b):
    return jnp.dot(a, b, preferred_element_type=F32)


def _rwkv_chunk(rs, krs, vs, lo, wls, w0s, a0s, k_ks, k_as, r_ks, lnx_gs, lnx_bs, gates, h0s, consts):
    ones, tril, strict, incl, eye = consts
    C = lo.shape[0]
    n2 = 2 * C
    lane = lax.broadcasted_iota(jnp.int32, (C, LANES), 1)
    lo_act = jnp.where(lane < LORA, jnp.tanh(lo), lo).astype(BF16)
    pres = [_mm(lo_act, wl) for wl in wls]
    logws = _each(lambda pre, w0: -math.exp(-0.5) * jax.nn.sigmoid(w0 + pre[:, :LANES]), pres, w0s)
    a_sigs = _each(lambda pre, a0: jax.nn.sigmoid(a0 + pre[:, LANES:]), pres, a0s)
    yield "stage"

    kks = _each(lambda kr, k_k: kr * k_k, krs, k_ks)
    kk_ss = _segment_sums([kk * kk for kk in kks], ones)
    kks = _each(lambda kk, ss: kk / jnp.maximum(jnp.sqrt(ss), 1e-12), kks, kk_ss)
    k_hs = _each(lambda kr, a_sig, k_a: kr * (1.0 + (a_sig - 1.0) * k_a), krs, a_sigs, k_as)
    rk_ss = _segment_sums([r * k_h * r_k for r, k_h, r_k in zip(rs, k_hs, r_ks)], ones)
    bonuses = _each(lambda ss, v: ss * v, rk_ss, vs)
    yield "stage"

    def split3(x):
        l1 = x.astype(BF16)
        rem = x - l1.astype(F32)
        l2 = rem.astype(BF16)
        return jnp.concatenate([l1, l2, (rem - l2.astype(F32)).astype(BF16)], axis=0)

    cums = [_mm(tril, split3(logw)) for logw in logws]
    yield "stage"
    g_incls = [jnp.exp(cum) for cum in cums]
    g_excls = _each(lambda cum, logw: jnp.exp(cum - logw), cums, logws)
    g_invs = [jnp.exp(-cum) for cum in cums]
    g_lasts = [g[C - 1:C, :] for g in g_incls]

    a_ts = _each(lambda kk, g: _stack_heads(-kk * g).astype(BF16), kks, g_excls)
    r_ts = _each(lambda r, g: _stack_heads(r * g).astype(BF16), rs, g_incls)
    v_sts = [_stack_heads(v).astype(BF16) for v in vs]
    bks = _each(lambda kk, a_sig, k_h, g: jnp.concatenate([_stack_heads(kk * a_sig * g), _stack_heads(k_h * g)],
                                                          axis=0), kks, a_sigs, k_hs, g_invs)
    gmats = _each(lambda a_t, r_t, bk: lax.dot_general(jnp.concatenate([a_t, r_t], axis=0), bk.astype(BF16),
                                                       (((1,), (1,)), ((), ())), preferred_element_type=F32),
                  a_ts, r_ts, bks)
    yield "stage"
    a_abs = [jnp.where(strict, g[:n2, :n2], 0.0) for g in gmats]
    a_aks = [jnp.where(strict, g[:n2, n2:], 0.0).astype(BF16) for g in gmats]
    a_rbs = [jnp.where(incl, g[n2:, :n2], 0.0).astype(BF16) for g in gmats]
    a_rks = [jnp.where(incl, g[n2:, n2:], 0.0).astype(BF16) for g in gmats]
    yield "stage"

    eye2 = jnp.where(lax.broadcasted_iota(jnp.int32, (n2, n2), 0)
                     == lax.broadcasted_iota(jnp.int32, (n2, n2), 1), 1.0, 0.0)
    n_factors = int(math.log2(C))
    pows = [a_abs]
    pows16 = [[a.astype(BF16) for a in a_abs]]
    for _ in range(n_factors - 1):
        pows.append([_mm(pw, pw) for pw in pows16[-1]])
        pows16.append([x.astype(BF16) for x in pows[-1]])
        yield "stage"
    prods = [_each(lambda x, y, x16, y16: eye2 + x + y + _mm(x16, y16), pows[i], pows[i + 1], pows16[i], pows16[i + 1])
             for i in range(0, n_factors - 1, 2)]
    if n_factors % 2:
        prods.append([eye2 + x for x in pows[-1]])
    yield "stage"
    while len(prods) > 1:
        nxt = [_each(lambda a, b: _mm(a.astype(BF16), b.astype(BF16)), prods[i], prods[i + 1])
               for i in range(0, len(prods) - 1, 2)]
        prods = nxt + ([prods[-1]] if len(prods) % 2 else [])
        yield "stage"
    t_invs = prods[0]

    h16s = [h0.astype(BF16) for h0 in h0s]
    nt = (((1,), (1,)), ((), ()))
    if n2 % LANES == 0:
        cat = jnp.concatenate
        xs = _each(lambda a_t, h16, a_ak, v_st: _mm(cat([a_t, a_ak], axis=1), cat([h16, v_st], axis=0)),
                   a_ts, h16s, a_aks, v_sts)
    else:
        xs = _each(lambda a_t, s16, a_ak, v_st: lax.dot_general(a_t, s16, nt, preferred_element_type=F32)
                   + _mm(a_ak, v_st), a_ts, h16s, a_aks, v_sts)
    u32s = _each(lambda t, x: _mm(t.astype(BF16), x.astype(BF16)), t_invs, xs)
    u16s = [u.astype(BF16) for u in u32s]
    yield "stage"
    if n2 % LANES == 0:
        o_sts = _each(lambda r_t, h16, a_rb, u16, a_rk, v_st:
                      _mm(cat([r_t, a_rb, a_rk], axis=1), cat([h16, u16, v_st], axis=0)),
                      r_ts, h16s, a_rbs, u16s, a_rks, v_sts)
    else:
        o_sts = _each(lambda r_t, s16, a_rb, u16, a_rk, v_st: lax.dot_general(r_t, s16, nt, preferred_element_type=F32)
                      + _mm(a_rb, u16) + _mm(a_rk, v_st), r_ts, h16s, a_rbs, u16s, a_rks, v_sts)
    os_ = [o_st[:C] + o_st[C:] for o_st in o_sts]
    yield "stage"

    if n2 % LANES == 0:
        bk_ts = _each(lambda bk, g_last: (bk * g_last).T.astype(BF16), bks, g_lasts)
        g_cols = [jnp.sum(jnp.where(eye, jnp.broadcast_to(g_last, (LANES, LANES)), 0.0), axis=1, keepdims=True)
                  for g_last in g_lasts]
        h_news = _each(lambda g_col, h0, bk_t, u16, v_st: g_col * h0 + _mm(bk_t, jnp.concatenate([u16, v_st], axis=0)),
                       g_cols, h0s, bk_ts, u16s, v_sts)
    else:
        uv_ts = _each(lambda u, v: jnp.concatenate([u, _stack_heads(v)], axis=0).T.astype(BF16), u32s, vs)
        h_news = _each(lambda s0, g_last, uv_t, bk: s0 * g_last + _mm(uv_t, (bk * g_last).astype(BF16)),
                       h0s, g_lasts, uv_ts, bks)
    yield "stage"

    mus = [z * (1.0 / RWKV_HEAD) for z in _segment_sums(os_, ones)]
    ds = _each(lambda o, mu: o - mu, os_, mus)
    vars_ = [z * (1.0 / RWKV_HEAD) for z in _segment_sums([d * d for d in ds], ones)]
    ys = _each(lambda d, var, g, b, bonus, gate:
               (((d * lax.rsqrt(var + LNX_EPS) * g + b) + bonus) * gate.astype(F32)).astype(BF16),
               ds, vars_, lnx_gs, lnx_bs, bonuses, gates)
    return ys, h_news


def _rwkv_body(c, n_chunks, refs, *, chunk, pairs, use_s0):
    (r_ref, k_ref, v_ref, lo_ref, pr_ref, pk_ref, pv_ref, plo_ref,
     tmr_ref, tmk_ref, tmv_ref, tmlo_ref, wl_ref, w0_ref, a0_ref, kk_ref, ka_ref, rk_ref,
     lg_ref, lb_ref, gate_ref, s0_ref, y_ref, sout_ref,
     h_ref, cr_ref, ck_ref, cv_ref, clo_ref) = refs
    C = chunk
    n2 = 2 * C

    @pl.when(c == 0)
    def _():
        cr_ref[...] = pr_ref[0]
        ck_ref[...] = pk_ref[0]
        cv_ref[...] = pv_ref[0]
        clo_ref[...] = plo_ref[0]
        if use_s0:
            z = jnp.zeros((RWKV_HEAD, RWKV_HEAD), F32)
            for u in range(pairs):
                st = jnp.concatenate([jnp.concatenate([s0_ref[0, 2 * u], z], axis=1),
                                      jnp.concatenate([z, s0_ref[0, 2 * u + 1]], axis=1)], axis=0)
                h_ref[u] = st if (2 * C) % LANES else st.T
        else:
            h_ref[...] = jnp.zeros_like(h_ref)

    yield "init"
    ri = lax.broadcasted_iota(jnp.int32, (n2, n2), 0)
    ci = lax.broadcasted_iota(jnp.int32, (n2, n2), 1)
    tril = (lax.broadcasted_iota(jnp.int32, (C, 3 * C), 1) % C <= lax.broadcasted_iota(jnp.int32, (C, 3 * C), 0))
    eye = (lax.broadcasted_iota(jnp.int32, (LANES, LANES), 0)
           == lax.broadcasted_iota(jnp.int32, (LANES, LANES), 1))
    consts = (_half_ones(), jnp.where(tril, 1.0, 0.0).astype(BF16), ri > ci, ri >= ci, eye)

    def token_shift(cur_ref3, carry_ref, tm_ref):
        cur = cur_ref3[0]
        row0 = lax.broadcasted_iota(jnp.int32, cur.shape, 0) == 0
        prev = jnp.where(row0, carry_ref[...], pltpu.roll(cur, 1, 0))
        carry_ref[...] = cur[C - 1:C, :]
        return cur + tm_ref[...] * (prev - cur)

    lo = token_shift(lo_ref, clo_ref, tmlo_ref)
    r_all = token_shift(r_ref, cr_ref, tmr_ref)
    k_all = token_shift(k_ref, ck_ref, tmk_ref)
    v_all = token_shift(v_ref, cv_ref, tmv_ref)
    col = lambda u: slice(u * LANES, (u + 1) * LANES)
    per_pair = lambda x: [x[:, col(u)] for u in range(pairs)]
    ys, h_news = yield from _rwkv_chunk(
        per_pair(r_all), per_pair(k_all), per_pair(v_all), lo, [wl_ref[u] for u in range(pairs)],
        per_pair(w0_ref[...]), per_pair(a0_ref[...]), per_pair(kk_ref[...]), per_pair(ka_ref[...]),
        per_pair(rk_ref[...]), per_pair(lg_ref[...]), per_pair(lb_ref[...]), per_pair(gate_ref[0]),
        [h_ref[u] for u in range(pairs)], consts)
    for u in range(pairs):
        y_ref[0, :, col(u)] = ys[u]
        h_ref[u] = h_news[u]

    yield "final"

    @pl.when(c == n_chunks - 1)
    def _():
        for u in range(pairs):
            st = h_news[u] if (2 * C) % LANES else h_news[u].T
            sout_ref[0, 2 * u] = st[:RWKV_HEAD, :RWKV_HEAD]
            sout_ref[0, 2 * u + 1] = st[RWKV_HEAD:, RWKV_HEAD:]


def _rwkv_kernel(*refs, **kw):
    _run(_rwkv_body(pl.program_id(2), pl.num_programs(2), refs, **kw))


def _rwkv_call(rkv, lora, gates, prevs, s0, p, *, chunk, pairs):
    b, t, _ = lora.shape
    n_chunks = t // chunk
    ng = N_RWKV_PAIRS // pairs
    w = pairs * LANES
    use_s0 = s0 is not None
    sblk = (1, 2 * pairs, RWKV_HEAD, RWKV_HEAD)
    if s0 is None:
        s0 = jnp.zeros(sblk, F32)
        s0_spec = pl.BlockSpec(sblk, lambda i, j, c: (0, 0, 0, 0))
    else:
        s0_spec = pl.BlockSpec(sblk, lambda i, j, c: (i, j, 0, 0))

    tok = pl.BlockSpec((1, chunk, w), lambda i, j, c: (i, c, j))
    prev = pl.BlockSpec((1, 1, w), lambda i, j, c: (i, 0, j))

    def par():
        return pl.BlockSpec((1, w), lambda i, j, c: (0, j))

    return dict(
        grid=(b, ng, n_chunks),
        kw=dict(chunk=chunk, pairs=pairs, use_s0=use_s0),
        in_specs=[
            tok, tok, tok,
            pl.BlockSpec((1, chunk, LANES), lambda i, j, c: (i, c, 0)),
            prev, prev, prev,
            pl.BlockSpec((1, 1, LANES), lambda i, j, c: (i, 0, 0)),
            par(), par(), par(), pl.BlockSpec((1, LANES), lambda i, j, c: (0, 0)),
            pl.BlockSpec((pairs, LANES, 2 * LANES), lambda i, j, c: (j, 0, 0)),
            par(), par(), par(), par(), par(), par(), par(),
            pl.BlockSpec((1, chunk, w), lambda i, j, c: (i, c, ng + j)),
            s0_spec,
        ],
        args=[*rkv, lora, *prevs, p["tm_r"], p["tm_k"], p["tm_v"], p["tm_lora"], p["w_lora"],
              p["w0"], p["a0"], p["k_k"], p["k_a"], p["r_k"], p["lnx_g"], p["lnx_b"], gates, s0],
        out_specs=[pl.BlockSpec((1, chunk, w), lambda i, j, c: (i, c, j)),
                   pl.BlockSpec(sblk, lambda i, j, c: (i, j, 0, 0))],
        out_shape=[jax.ShapeDtypeStruct((b, t, RWKV_W), BF16),
                   jax.ShapeDtypeStruct((b, N_RWKV_HEADS, RWKV_HEAD, RWKV_HEAD), F32)],
        scratch=[pltpu.VMEM((pairs, LANES, LANES), F32),
                 pltpu.VMEM((1, w), F32), pltpu.VMEM((1, w), F32),
                 pltpu.VMEM((1, w), F32), pltpu.VMEM((1, LANES), F32)],
    )


def _rwkv(call):
    return pl.pallas_call(
        functools.partial(_rwkv_kernel, **call["kw"]),
        grid=call["grid"],
        in_specs=call["in_specs"],
        out_specs=call["out_specs"],
        out_shape=call["out_shape"],
        scratch_shapes=call["scratch"],
        compiler_params=_cparams(("parallel", "parallel", "arbitrary")),
        name="rwkv7_chunked",
    )(*call["args"])


def _remap(spec, decode):
    if spec.block_shape is None:
        return spec
    return pl.BlockSpec(spec.block_shape, lambda *a: spec.index_map(*decode(*a)))


def _fused_kernel(pt_ref, *refs, n_in, n_out, n_scr, rwkv_kw, attn_kw, n_chunks, attn_steps):
    (ri, ai), (ro, ao), (rs, as_) = n_in, n_out, n_scr
    r_in, a_in = refs[:ri], refs[ri:ri + ai]
    r_out, a_out = refs[ri + ai:ri + ai + ro], refs[ri + ai + ro:ri + ai + ro + ao]
    r_scr, a_scr = refs[ri + ai + ro + ao:ri + ai + ro + ao + rs], refs[ri + ai + ro + ao + rs:]
    s = pl.program_id(0)
    _interleave(
        _rwkv_body(s % n_chunks, n_chunks, (*r_in, *r_out, *r_scr), **rwkv_kw),
        _sample_attn_body(s % attn_steps, attn_steps, a_in[0], a_in[1], (*a_in[2:], *a_out, *a_scr), **attn_kw))


def _rwkv_with_sample_attention(rwkv_call, attn_call, page_table):
    b_r, ng, nc = rwkv_call["grid"]
    b_a, nj = attn_call["grid"]
    n_steps = b_r * ng * nc
    assert n_steps == b_a * nj
    dec_r = lambda s, pt: (s // (ng * nc), (s // nc) % ng, s % nc)
    dec_a = lambda s, pt: (s // nj, s % nj, pt)
    in_specs = [_remap(sp, dec_r) for sp in rwkv_call["in_specs"]] + [_remap(sp, dec_a) for sp in attn_call["in_specs"]]
    out_specs = [_remap(sp, dec_r) for sp in rwkv_call["out_specs"]] + [_remap(sp, dec_a) for sp in attn_call["out_specs"]]
    kern = functools.partial(
        _fused_kernel,
        n_in=(len(rwkv_call["in_specs"]), len(attn_call["in_specs"])),
        n_out=(len(rwkv_call["out_specs"]), len(attn_call["out_specs"])),
        n_scr=(len(rwkv_call["scratch"]), len(attn_call["scratch"])),
        rwkv_kw=rwkv_call["kw"], attn_kw=attn_call["kw"], n_chunks=nc, attn_steps=nj)
    grid_spec = pltpu.PrefetchScalarGridSpec(
        num_scalar_prefetch=1, grid=(n_steps,), in_specs=in_specs, out_specs=out_specs,
        scratch_shapes=rwkv_call["scratch"] + attn_call["scratch"])
    y_rwkv, s_out, y_att = pl.pallas_call(
        kern,
        grid_spec=grid_spec,
        out_shape=rwkv_call["out_shape"] + attn_call["out_shape"],
        compiler_params=_cparams(("arbitrary",), FUSED_VMEM_LIMIT_BYTES),
        name="rwkv7_and_paged_attention",
    )(page_table, *rwkv_call["args"], *attn_call["args"])
    return y_rwkv, s_out, y_att


def _prep_layer_params(l, ln_g, w_in, q_norm_g, k_norm_g, lambda_q1, lambda_k1, lambda_q2, lambda_k2,
                       subln_g, time_mix, w0, w2, a0, a2, k_k, k_a, r_k, lnx_g, lnx_b, w_out):
    w = w_in[l]
    sh = OFF_SHIFT
    o1, o2, o3, o4 = RWKV_W, RWKV_W + LORA, 2 * RWKV_W + LORA, 3 * RWKV_W + LORA
    w16 = w.astype(BF16)
    w_qkvr = w16
    w_rk = w16[:, sh + o2:sh + o3]
    w_rv = w16[:, sh + o3:sh + o4]
    w_gates = w16[:, OFF_GA:]
    w_lora_in = jnp.concatenate([w16[:, sh + o1:sh + o2], w16[:, sh + o4:sh + SHIFT_W]], axis=1)
    tmix = time_mix[l]
    tm_lora = jnp.concatenate([tmix[o1:o2], tmix[o4:]]).reshape(1, 2 * LORA)
    w2p = w2[l].reshape(LORA, N_RWKV_PAIRS, LANES).transpose(1, 0, 2)
    a2p = a2[l].reshape(LORA, N_RWKV_PAIRS, LANES).transpose(1, 0, 2)
    z = jnp.zeros_like(w2p)
    w_lora = jnp.concatenate([jnp.concatenate([w2p, z], axis=2), jnp.concatenate([z, a2p], axis=2)],
                             axis=1).astype(BF16)
    lam_init = 0.8 - 0.6 * math.exp(-0.3 * l)
    lam = (jnp.exp(jnp.sum(lambda_q1[l] * lambda_k1[l])) - jnp.exp(jnp.sum(lambda_q2[l] * lambda_k2[l]))
           + lam_init).reshape(1, 1).astype(F32)
    row = lambda z_: z_.reshape(1, -1)
    return dict(
        layer=l, ln_g=ln_g[l], w_qkvr=w_qkvr, w_rk=w_rk, w_rv=w_rv, w_gates=w_gates, w_lora_in=w_lora_in,
        w_out=w_out[l].astype(BF16),
        q_gain=jnp.tile(q_norm_g[l], 2).reshape(1, DV), k_gain=jnp.tile(k_norm_g[l], 2).reshape(1, DV),
        lam=lam, out_scale=1.0 - lam_init, subln_g=subln_g[l],
        tm_r=row(tmix[:o1]), tm_k=row(tmix[o2:o3]), tm_v=row(tmix[o3:o4]), tm_lora=tm_lora, w_lora=w_lora,
        w0=row(w0[l]), a0=row(a0[l]), k_k=row(k_k[l]), k_a=row(k_a[l]), r_k=row(r_k[l]),
        lnx_g=row(lnx_g[l]), lnx_b=row(lnx_b[l]),
    )


def _project(x, pos, p):
    b, t, d = x.shape
    m = b * t
    x2 = x.reshape(m, d)
    h = _rmsnorm(x2, p["ln_g"])

    tm = min(1024, m)
    cos, s1, s2 = _rope_tables(pos)
    if t < tm:
        reps = tm // t
        cos, s1, s2 = (jnp.tile(z, (reps, 1)) for z in (cos, s1, s2))
    gain_spec = pl.BlockSpec((1, LANES), lambda i, j: (0, 0))
    blk = lambda cols: cols // LANES
    tm_hm, tn_hm = min(512, m), 1024

    def qk(col_off, gain, scale, dts):
        tm_, tn_ = (tm_hm, tn_hm) if "f32_head_major" in dts else (tm, 512)
        n_tab = cos.shape[0] // tm_
        tab_spec = pl.BlockSpec((tm_, LANES), lambda i, j: (i % n_tab, 0))
        kern = functools.partial(_mm_qk_kernel, scale=scale, rows_per_chunk=min(256, tm_))
        return _matmul([h], p["w_qkvr"], col_block_off=blk(col_off), n_cols=ATT_W, kernel=kern,
                       out_dtypes=dts, extra=(gain, cos, s1, s2),
                       extra_specs=(gain_spec, tab_spec, tab_spec, tab_spec), scratch=True, tm=tm_, tn=tn_)

    def plain(w, col_off, n_cols):
        return _matmul([h], w, col_block_off=blk(col_off), n_cols=n_cols, kernel=_mm_plain_kernel,
                       out_dtypes=(F32,), tm=tm, tn=1024)[0]

    (q16,) = qk(0, p["q_gain"], DQK ** -0.5 * math.log2(math.e), (BF16,))
    k32, k16 = qk(OFF_K, p["k_gain"], 1.0, ("f32_head_major", BF16))
    v32, v16 = _matmul([h], p["w_qkvr"], col_block_off=blk(OFF_V), n_cols=ATT_W,
                       kernel=_mm_dual_kernel, out_dtypes=("f32_head_major", BF16), tm=tm_hm, tn=tn_hm)
    rkv = [plain(p["w_qkvr"], OFF_SHIFT, RWKV_W).reshape(b, t, RWKV_W),
           plain(p["w_rk"], 0, RWKV_W).reshape(b, t, RWKV_W),
           plain(p["w_rv"], 0, RWKV_W).reshape(b, t, RWKV_W)]
    lora = plain(p["w_lora_in"], 0, 2 * LORA).reshape(b, t, 2 * LORA)
    (gates,) = _matmul([h], p["w_gates"], col_block_off=0, n_cols=ATT_W + RWKV_W,
                       kernel=_mm_silu_kernel, out_dtypes=(BF16,), tm=tm, tn=1024)
    return dict(b=b, t=t, m=m, tm=tm, x2=x2, q16=q16, k32=k32, k16=k16, v32=v32, v16=v16, rkv=rkv, lora=lora,
                gates=gates)


def _rwkv_call_for(pr, shift_prev, wkv_prev, p, *, chunk, pairs):
    b = pr["b"]
    o1, o2, o3, o4 = RWKV_W, RWKV_W + LORA, 2 * RWKV_W + LORA, 3 * RWKV_W + LORA
    prev_lora = jnp.concatenate([shift_prev[:, o1:o2], shift_prev[:, o4:]], axis=1)
    prevs = [z.reshape(b, 1, -1) for z in (shift_prev[:, :o1], shift_prev[:, o2:o3], shift_prev[:, o3:o4],
                                           prev_lora)]
    return _rwkv_call(pr["rkv"], pr["lora"], pr["gates"].reshape(b, pr["t"], -1), prevs, wkv_prev, p,
                      chunk=chunk, pairs=pairs)


def _finish_layer(pr, y_att, y_rwkv, p):
    b, t, m, tm = pr["b"], pr["t"], pr["m"], pr["tm"]
    d = pr["x2"].shape[1]
    last_lora = pr["lora"][:, -1]
    shift_new = jnp.concatenate([pr["rkv"][0][:, -1], last_lora[:, :LORA], pr["rkv"][1][:, -1],
                                 pr["rkv"][2][:, -1], last_lora[:, LORA:]], axis=1)
    tn_out = 512
    x_spec = pl.BlockSpec((tm, tn_out), lambda i, j: (i, j))
    (out,) = _matmul([y_att.reshape(m, ATT_W), y_rwkv.reshape(m, RWKV_W)], p["w_out"], col_block_off=0,
                     n_cols=d, kernel=_mm_residual_kernel, out_dtypes=(F32,), extra=(pr["x2"],),
                     extra_specs=(x_spec,), tm=tm, tn=tn_out)
    return (out.reshape(b, t, d), pr["k32"].reshape(b, t, N_ATT_HEADS, DV),
            pr["v32"].reshape(b, t, N_ATT_HEADS, DV), shift_new)


def _layer_pair(xp, xs, pos_p, pos_s, shift_s, wkv_s, cache_k, cache_v, page_table, p, *,
                chunk_p=64, pairs_p=8, pairs_s=16, pages_per_step=8):
    pp = _project(xp, pos_p, p)
    ps = _project(xs, pos_s, p)
    b_s, t_s = ps["b"], ps["t"]
    rw_p = _rwkv_call_for(pp, jnp.zeros((pp["b"], SHIFT_W), F32), None, p, chunk=chunk_p, pairs=pairs_p)
    rw_s = _rwkv_call_for(ps, shift_s, wkv_s, p, chunk=t_s, pairs=pairs_s)
    at_s = _sample_attn_call(ps["q16"].reshape(b_s, t_s, ATT_W), ps["k16"].reshape(b_s, t_s, ATT_W),
                             ps["v16"].reshape(b_s, t_s, ATT_W), ps["gates"].reshape(b_s, t_s, -1),
                             cache_k, cache_v, page_table, p["lam"], p["subln_g"], layer=p["layer"],
                             out_scale=p["out_scale"], pages_per_step=pages_per_step)
    if math.prod(rw_p["grid"]) == math.prod(at_s["grid"]):
        y_rwkv_p, wkv_p, y_att_s = _rwkv_with_sample_attention(rw_p, at_s, page_table)
    else:
        y_rwkv_p, wkv_p = _rwkv(rw_p)
        y_att_s = _sample_attention(page_table, at_s)
    y_att_p = _prompt_attention(pp["q16"], pp["k16"], pp["v16"], pp["gates"], p["lam"], p["subln_g"],
                                batch=pp["b"], seq=pp["t"], out_scale=p["out_scale"])
    y_rwkv_s, wkv_s_new = _rwkv(rw_s)
    return (*_finish_layer(pp, y_att_p, y_rwkv_p, p), wkv_p), (*_finish_layer(ps, y_att_s, y_rwkv_s, p), wkv_s_new)


def kernel(x_prompt, x_sample, cache_k, cache_v, state_shift, state_wkv, page_table, ln_g, w_in, q_norm_g,
           k_norm_g, lambda_q1, lambda_k1, lambda_q2, lambda_k2, subln_g, time_mix, w0, w2, a0, a2, k_k,
           k_a, r_k, lnx_g, lnx_b, w_out):
    depth = w_in.shape[0]
    s_p, s_s = x_prompt.shape[1], x_sample.shape[1]
    past_len = page_table.shape[1] * PAGE_SIZE
    pos_p = jnp.arange(s_p)
    pos_s = past_len + jnp.arange(s_s)
    yp, ys = x_prompt, x_sample
    outs = [[] for _ in range(8)]
    for l in range(depth):
        p = _prep_layer_params(l, ln_g, w_in, q_norm_g, k_norm_g, lambda_q1, lambda_k1, lambda_q2,
                               lambda_k2, subln_g, time_mix, w0, w2, a0, a2, k_k, k_a, r_k, lnx_g, lnx_b,
                               w_out)
        (yp, kp, vp, shp, wkvp), (ys, ks, vs, shs, wkvs) = _layer_pair(
            yp, ys, pos_p, pos_s, state_shift[l], state_wkv[l], cache_k, cache_v, page_table, p)
        for lst, val in zip(outs, (kp, vp, shp, wkvp, ks, vs, shs, wkvs)):
            lst.append(val)
    stacked = [jnp.stack(lst, 0) for lst in outs]
    return (yp, ys, *stacked)
```
